```python
import jax
import jax.numpy as jnp
from jax import lax
import numpy as np

D_MODEL = 2048
BATCH = 4
SEQ = 8192
DEPTH = 1

GRID_W = 64
CTX_LEN = 256

ATTN_HEADS = 16
ATTN_KV_HEADS = 4
ATTN_GROUP = ATTN_HEADS // ATTN_KV_HEADS
HEAD_DIM = 128
WINDOW = 128
BLOCK = 128
ROPE_BASE = 10000.0

MLSTM_HEADS = 8
MLSTM_QK_DIM = 128
MLSTM_V_DIM = 256
MLSTM_CHUNK = 128
N_DIRS = 2
FGATE_BIAS_LO = 3.0
FGATE_BIAS_HI = 6.0

D_FF = -(-8 * D_MODEL // (3 * 256)) * 256

NORM_EPS = 1e-6
N_MOD = 6

ATTN_Q_W = ATTN_HEADS * HEAD_DIM
ATTN_KV_W = ATTN_KV_HEADS * HEAD_DIM
MLSTM_QK_W = MLSTM_HEADS * MLSTM_QK_DIM
MLSTM_V_W = MLSTM_HEADS * MLSTM_V_DIM
N_GATE = N_DIRS * 2 * MLSTM_HEADS
KV_SIDE_SIZES = (ATTN_KV_W, ATTN_KV_W, MLSTM_QK_W, MLSTM_V_W, N_GATE)
Q_SIDE_SIZES = (ATTN_Q_W, MLSTM_QK_W, MLSTM_V_W, D_MODEL, D_MODEL)
N_KV_SIDE = sum(KV_SIDE_SIZES)
N_IN = N_KV_SIDE + sum(Q_SIDE_SIZES)

kernel_name = "hybrid_dit_gqa_mlstm_prefix"


def _split(p, sizes):
    idx = [int(i) for i in np.cumsum(sizes)[:-1]]
    return jnp.split(p, idx, axis=-1)


def rms_norm(x, g):
    xf = x.astype(jnp.float32)
    y = xf * lax.rsqrt(jnp.mean(xf * xf, axis=-1, keepdims=True) + NORM_EPS)
    return (y * g.astype(jnp.float32)).astype(x.dtype)


def modulate(h, shift, scale):
    return h * (1 + scale) + shift


def adaln_params(cvec, w, b, n_chunks):
    mod = jax.nn.silu(cvec) @ w[:, :n_chunks * D_MODEL] + b[:n_chunks * D_MODEL]
    return jnp.split(mod, n_chunks, axis=-1)


def _rope_1d(x, pos):
    half = x.shape[-1] // 2
    inv_freq = ROPE_BASE ** (-jnp.arange(half, dtype=jnp.float32) / half)
    ang = pos.astype(jnp.float32)[:, None] * inv_freq[None, :]
    cos = jnp.cos(ang)[:, None, :]
    sin = jnp.sin(ang)[:, None, :]
    x1 = x[..., :half].astype(jnp.float32)
    x2 = x[..., half:].astype(jnp.float32)
    return jnp.concatenate([x1 * cos - x2 * sin, x2 * cos + x1 * sin], axis=-1)


def rope_2d(x, rows, cols):
    half = x.shape[-1] // 2
    y = jnp.concatenate([_rope_1d(x[..., :half], rows), _rope_1d(x[..., half:], cols)], axis=-1)
    return y.astype(x.dtype)


def window_attention(q, k, v, kc, vc, sink):
    B, S = q.shape[:2]
    C = kc.shape[1]
    nb = S // BLOCK
    scale = HEAD_DIM ** -0.5
    qb = q.reshape(B, nb, BLOCK, ATTN_KV_HEADS, ATTN_GROUP, HEAD_DIM)

    def band(t):
        tp = jnp.pad(t, ((0, 0), (BLOCK, BLOCK), (0, 0), (0, 0)))
        tp = tp.reshape(B, nb + 2, BLOCK, ATTN_KV_HEADS, HEAD_DIM)
        return jnp.concatenate([tp[:, :-2], tp[:, 1:-1], tp[:, 2:]], axis=2)

    kb, vb = band(k), band(v)
    s_loc = jnp.einsum("bnqhgd,bnkhd->bhgnqk", qb, kb).astype(jnp.float32) * scale
    s_ctx = jnp.einsum("bnqhgd,bchd->bhgnqc", qb, kc).astype(jnp.float32) * scale
    qpos = jnp.arange(S).reshape(nb, BLOCK, 1)
    kpos = (jnp.arange(nb) * BLOCK - BLOCK)[:, None, None] + jnp.arange(3 * BLOCK)[None, None, :]
    valid = (jnp.abs(qpos - kpos) <= WINDOW) & (kpos >= 0) & (kpos < S)
    s_loc = jnp.where(valid, s_loc, -jnp.inf)
    sink_l = jnp.broadcast_to(sink.astype(jnp.float32).reshape(1, ATTN_KV_HEADS, ATTN_GROUP, 1, 1, 1),
                              s_loc.shape[:-1] + (1,))
    p = jax.nn.softmax(jnp.concatenate([s_loc, s_ctx, sink_l], axis=-1), axis=-1).astype(v.dtype)
    n_loc = 3 * BLOCK
    o = (jnp.einsum("bhgnqk,bnkhd->bnqhgd", p[..., :n_loc], vb)
         + jnp.einsum("bhgnqc,bchd->bnqhgd", p[..., n_loc:n_loc + C], vc))
    return o.reshape(B, S, ATTN_Q_W)


def context_attention(q, kc, vc, sink):
    B, C = q.shape[:2]
    qg = q.reshape(B, C, ATTN_KV_HEADS, ATTN_GROUP, HEAD_DIM)
    s = jnp.einsum("bqhgd,bkhd->bhgqk", qg, kc).astype(jnp.float32) * (HEAD_DIM ** -0.5)
    sink_l = jnp.broadcast_to(sink.astype(jnp.float32).reshape(1, ATTN_KV_HEADS, ATTN_GROUP, 1, 1),
                              s.shape[:-1] + (1,))
    p = jax.nn.softmax(jnp.concatenate([s, sink_l], axis=-1), axis=-1)[..., :C].astype(vc.dtype)
    return jnp.einsum("bhgqk,bkhd->bqhgd", p, vc).reshape(B, C, ATTN_Q_W)


def mlstm_chunk_states(k, v, logi, logf, state0):
    B, H, T, _ = k.shape
    N = T // MLSTM_CHUNK
    kc = k.reshape(B, H, N, MLSTM_CHUNK, MLSTM_QK_DIM)
    vc = v.reshape(B, H, N, MLSTM_CHUNK, MLSTM_V_DIM)
    b = jnp.cumsum(logf.reshape(B, H, N, MLSTM_CHUNK), axis=-1)
    b_end = b[..., -1]
    w = b_end[..., None] - b + logi.reshape(B, H, N, MLSTM_CHUNK)
    a = jnp.max(w, axis=-1)
    e = jnp.exp(w - a[..., None])
    c_loc = jnp.einsum("bhnl,bhnlk,bhnlv->bhnkv", e, kc, vc)
    n_loc = jnp.einsum("bhnl,bhnlk->bhnk", e, kc)

    def step(carry, xs):
        C, n, m = carry
        cl, nl, bl, al = xs
        m_new = jnp.maximum(bl + m, al)
        f_prev = jnp.exp(bl + m - m_new)
        f_loc = jnp.exp(al - m_new)
        C_new = f_prev[..., None, None] * C + f_loc[..., None, None] * cl
        n_new = f_prev[..., None] * n + f_loc[..., None] * nl
        return (C_new, n_new, m_new), (C, n, m)

    xs = tuple(jnp.moveaxis(t, 2, 0) for t in (c_loc, n_loc, b_end, a))
    final, starts = lax.scan(step, state0, xs)
    starts = tuple(jnp.moveaxis(t, 0, 2) for t in starts)
    return starts, final


def mlstm_chunk_outputs(q, k, v, logi, logf, starts):
    B, H, T, _ = q.shape
    N = T // MLSTM_CHUNK
    L = MLSTM_CHUNK
    qc = q.reshape(B, H, N, L, MLSTM_QK_DIM)
    kc = k.reshape(B, H, N, L, MLSTM_QK_DIM)
    vc = v.reshape(B, H, N, L, MLSTM_V_DIM)
    li = logi.reshape(B, H, N, L)
    b = jnp.cumsum(logf.reshape(B, H, N, L), axis=-1)
    C0, n0, m0 = starts
    order = jnp.arange(L)[:, None] >= jnp.arange(L)[None, :]
    w = jnp.where(order, b[..., :, None] - b[..., None, :] + li[..., None, :], -jnp.inf)
    inter = b + m0[..., None]
    m = jnp.maximum(inter, jnp.max(w, axis=-1))
    s = jnp.einsum("bhntk,bhnsk->bhnts", qc, kc) * jnp.exp(w - m[..., None])
    e_inter = jnp.exp(inter - m)
    num = (jnp.einsum("bhnts,bhnsv->bhntv", s, vc)
           + e_inter[..., None] * jnp.einsum("bhntk,bhnkv->bhntv", qc, C0))
    den = jnp.sum(s, axis=-1) + e_inter * jnp.einsum("bhntk,bhnk->bhnt", qc, n0)
    h = num / jnp.maximum(jnp.abs(den), jnp.exp(-m))[..., None]
    return h.reshape(B, H, T, MLSTM_V_DIM)


def _maybe_flip(t, rev):
    return jnp.flip(t, axis=2) if rev else t


def mlstm_mixer(q, k, v, g, kc, vc, gc, qc, b_gates):
    B = q.shape[0]
    H = MLSTM_HEADS
    scale = MLSTM_QK_DIM ** -0.5

    def heads(t, d):
        return t.astype(jnp.float32).reshape(t.shape[0], t.shape[1], H, d).transpose(0, 2, 1, 3)

    def gates(t):
        z = (t.astype(jnp.float32) + b_gates.astype(jnp.float32)).reshape(t.shape[0], t.shape[1], N_DIRS, 2, H)
        z = z.transpose(2, 3, 0, 4, 1)
        return z[:, 0], jax.nn.log_sigmoid(z[:, 1])

    ql, kl, vl = heads(q, MLSTM_QK_DIM) * scale, heads(k, MLSTM_QK_DIM), heads(v, MLSTM_V_DIM)
    li, lf = gates(g)
    kcc, vcc = heads(kc, MLSTM_QK_DIM), heads(vc, MLSTM_V_DIM)
    lic, lfc = gates(gc)
    qcc = None if qc is None else heads(qc, MLSTM_QK_DIM) * scale
    state0 = (jnp.zeros((B, H, MLSTM_QK_DIM, MLSTM_V_DIM), jnp.float32),
              jnp.zeros((B, H, MLSTM_QK_DIM), jnp.float32),
              jnp.zeros((B, H), jnp.float32))
    lat_out, ctx_out = [], []
    for d in range(N_DIRS):
        rev = d == 1
        c_starts, c_final = mlstm_chunk_states(_maybe_flip(kcc, rev), _maybe_flip(vcc, rev),
                                               _maybe_flip(lic[d], rev), _maybe_flip(lfc[d], rev), state0)
        l_starts, _ = mlstm_chunk_states(_maybe_flip(kl, rev), _maybe_flip(vl, rev),
                                         _maybe_flip(li[d], rev), _maybe_flip(lf[d], rev), c_final)
        lat_out.append(_maybe_flip(mlstm_chunk_outputs(
            _maybe_flip(ql, rev), _maybe_flip(kl, rev), _maybe_flip(vl, rev),
            _maybe_flip(li[d], rev), _maybe_flip(lf[d], rev), l_starts), rev))
        if qcc is not None:
            ctx_out.append(_maybe_flip(mlstm_chunk_outputs(
                _maybe_flip(qcc, rev), _maybe_flip(kcc, rev), _maybe_flip(vcc, rev),
                _maybe_flip(lic[d], rev), _maybe_flip(lfc[d], rev), c_starts), rev))
    h_lat = lat_out[0] + lat_out[1]
    h_ctx = (ctx_out[0] + ctx_out[1]) if qcc is not None else None
    return h_lat, h_ctx


def head_norm(h, g):
    h = h * lax.rsqrt(jnp.mean(h * h, axis=-1, keepdims=True) + NORM_EPS)
    B, H, T, dv = h.shape
    return h.transpose(0, 2, 1, 3).reshape(B, T, H * dv) * g.astype(jnp.float32)


def merge_branches(att, mem, g_att, g_mem, w_attn_proj, w_mlstm_proj, w_out):
    y = jax.nn.sigmoid(g_att) * (att @ w_attn_proj) + jax.nn.sigmoid(g_mem) * (mem @ w_mlstm_proj)
    return y @ w_out


def swiglu(h, w_ffn_in, w_ffn_out):
    gt, up = jnp.split(h @ w_ffn_in, 2, axis=-1)
    return (jax.nn.silu(gt) * up) @ w_ffn_out


def trunk_layer(x, xc, mod, mod_c, rows, cols, norm1_g, w_in, b_gates, attn_sink, mlstm_norm_g,
                w_attn_proj, w_mlstm_proj, w_out, norm2_g, w_ffn_in, w_ffn_out, ctx_out):
    B, S, _ = x.shape
    C = xc.shape[1]
    shift1, scale1, gate1, shift2, scale2, gate2 = mod
    h = modulate(rms_norm(x, norm1_g), shift1, scale1)
    hc = modulate(rms_norm(xc, norm1_g), mod_c[0], mod_c[1])
    p = h @ w_in
    pc = hc @ (w_in if ctx_out else w_in[:, :N_KV_SIDE])
    a_k, a_v, m_k, m_v, m_g = _split(p[..., :N_KV_SIDE], KV_SIDE_SIZES)
    a_q, m_q, m_o, g_att, g_mem = _split(p[..., N_KV_SIDE:], Q_SIDE_SIZES)
    ac_k, ac_v, mc_k, mc_v, mc_g = _split(pc[..., :N_KV_SIDE], KV_SIDE_SIZES)
    mc_q = None
    if ctx_out:
        ac_q, mc_q, mc_o, gc_att, gc_mem = _split(pc[..., N_KV_SIDE:], Q_SIDE_SIZES)

    q = rope_2d(a_q.reshape(B, S, ATTN_HEADS, HEAD_DIM), rows, cols)
    k = rope_2d(a_k.reshape(B, S, ATTN_KV_HEADS, HEAD_DIM), rows, cols)
    v = a_v.reshape(B, S, ATTN_KV_HEADS, HEAD_DIM)
    kc = ac_k.reshape(B, C, ATTN_KV_HEADS, HEAD_DIM)
    vc = ac_v.reshape(B, C, ATTN_KV_HEADS, HEAD_DIM)
    att = window_attention(q, k, v, kc, vc, attn_sink)

    h_mem, hc_mem = mlstm_mixer(m_q, m_k, m_v, m_g, mc_k, mc_v, mc_g, mc_q, b_gates)
    mem = (jax.nn.sigmoid(m_o.astype(jnp.float32)) * head_norm(h_mem, mlstm_norm_g)).astype(x.dtype)

    x = x + gate1 * merge_branches(att, mem, g_att, g_mem, w_attn_proj, w_mlstm_proj, w_out)
    x = x + gate2 * swiglu(modulate(rms_norm(x, norm2_g), shift2, scale2), w_ffn_in, w_ffn_out)

    if ctx_out:
        att_c = context_attention(ac_q.reshape(B, C, ATTN_HEADS, HEAD_DIM), kc, vc, attn_sink)
        mem_c = (jax.nn.sigmoid(mc_o.astype(jnp.float32)) * head_norm(hc_mem, mlstm_norm_g)).astype(xc.dtype)
        xc = xc + mod_c[2] * merge_branches(att_c, mem_c, gc_att, gc_mem, w_attn_proj, w_mlstm_proj, w_out)
        xc = xc + mod_c[5] * swiglu(modulate(rms_norm(xc, norm2_g), mod_c[3], mod_c[4]), w_ffn_in, w_ffn_out)
    return x, xc


def setup_inputs(seed: int = 0) -> dict:
    key = jax.random.key(seed)
    ks = jax.random.split(key, 20)
    nrm = jax.random.normal
    D = D_MODEL
    f_base = jnp.stack([jnp.zeros((MLSTM_HEADS,), jnp.float32),
                        jnp.linspace(FGATE_BIAS_LO, FGATE_BIAS_HI, MLSTM_HEADS, dtype=jnp.float32)])
    b_gates = (f_base[None, None] + 0.1 * nrm(ks[8], (DEPTH, N_DIRS, 2, MLSTM_HEADS), jnp.float32)
               ).reshape(DEPTH, N_GATE)
    return {
        "x": nrm(ks[0], (BATCH, SEQ, D), jnp.float32),
        "c": nrm(ks[1], (BATCH, D), jnp.float32),
        "ctx": nrm(ks[2], (BATCH, CTX_LEN, D), jnp.float32),
        "c_ctx": nrm(ks[3], (D,), jnp.float32),
        "w_ada": nrm(ks[4], (DEPTH, D, N_MOD * D), jnp.float32) * D ** -0.5,
        "b_ada": 0.01 * nrm(ks[5], (DEPTH, N_MOD * D), jnp.float32),
        "norm1_g": 1.0 + 0.01 * nrm(ks[6], (DEPTH, D), jnp.float32),
        "w_in": nrm(ks[7], (DEPTH, D, N_IN), jnp.float32) * D ** -0.5,
        "b_gates": b_gates,
        "attn_sink": 0.5 * nrm(ks[9], (DEPTH, ATTN_HEADS), jnp.float32),
        "mlstm_norm_g": 1.0 + 0.01 * nrm(ks[10], (DEPTH, MLSTM_V_W), jnp.float32),
        "w_attn_proj": nrm(ks[11], (DEPTH, ATTN_Q_W, D), jnp.float32) * ATTN_Q_W ** -0.5,
        "w_mlstm_proj": nrm(ks[12], (DEPTH, MLSTM_V_W, D), jnp.float32) * MLSTM_V_W ** -0.5,
        "w_out": nrm(ks[13], (DEPTH, D, D), jnp.float32) * D ** -0.5,
        "norm2_g": 1.0 + 0.01 * nrm(ks[14], (DEPTH, D), jnp.float32),
        "w_ffn_in": nrm(ks[15], (DEPTH, D, 2 * D_FF), jnp.float32) * D ** -0.5,
        "w_ffn_out": nrm(ks[16], (DEPTH, D_FF, D), jnp.float32) * D_FF ** -0.5,
        "final_norm_g": 1.0 + 0.01 * nrm(ks[17], (D,), jnp.float32),
    }


def reference(x, c, ctx, c_ctx, w_ada, b_ada, norm1_g, w_in, b_gates, attn_sink, mlstm_norm_g,
              w_attn_proj, w_mlstm_proj, w_out, norm2_g, w_ffn_in, w_ffn_out, final_norm_g):
    S = x.shape[1]
    rows_n = S // GRID_W
    rows, cols = jnp.meshgrid(jnp.arange(rows_n), jnp.arange(GRID_W), indexing="ij")
    rows = rows.reshape(-1)
    cols = cols.reshape(-1)
    xc = ctx
    for l in range(DEPTH):
        ctx_out = l < DEPTH - 1
        mod = adaln_params(c[:, None, :], w_ada[l], b_ada[l], N_MOD)
        mod_c = adaln_params(c_ctx, w_ada[l], b_ada[l], N_MOD if ctx_out else 2)
        x, xc = trunk_layer(x, xc, mod, mod_c, rows, cols, norm1_g[l], w_in[l], b_gates[l], attn_sink[l],
                            mlstm_norm_g[l], w_attn_proj[l], w_mlstm_proj[l], w_out[l], norm2_g[l],
                            w_ffn_in[l], w_ffn_out[l], ctx_out)
    return rms_norm(x, final_norm_g)
```

```python
import functools

import numpy as np
import jax
import jax.numpy as jnp
from jax import lax
from jax.experimental import pallas as pl
from jax.experimental.pallas import tpu as pltpu

F32 = jnp.float32
BF16 = jnp.bfloat16

GRID_W = 64
ATTN_HEADS = 16
ATTN_KV_HEADS = 4
ATTN_GROUP = ATTN_HEADS // ATTN_KV_HEADS
HEAD_DIM = 128
WINDOW = 128
ROPE_BASE = 10000.0
MLSTM_HEADS = 8
MLSTM_QK_DIM = 128
MLSTM_V_DIM = 256
CHUNK = 128
N_DIRS = 2
N_GATE = N_DIRS * 2 * MLSTM_HEADS
NORM_EPS = 1e-6
N_MOD = 6
QK_SCALE = HEAD_DIM ** -0.5

LANE = 128
V7X_VMEM_BYTES = 64 * 1024 * 1024
VMEM_LIMIT = V7X_VMEM_BYTES - 8 * 1024 * 1024

ATTN_Q_W = ATTN_HEADS * HEAD_DIM
ATTN_KV_W = ATTN_KV_HEADS * HEAD_DIM
MLSTM_QK_W = MLSTM_HEADS * MLSTM_QK_DIM
MLSTM_V_W = MLSTM_HEADS * MLSTM_V_DIM
C_AUG = MLSTM_V_DIM + LANE

NEG = -1e30


def _params(sem):
    return pltpu.CompilerParams(dimension_semantics=sem, vmem_limit_bytes=VMEM_LIMIT)


def _sigmoid(x):
    return 1.0 / (1.0 + jnp.exp(-x))


def _adaln_kernel(c_ref, w_ref, b_ref, o_ref):
    cc = c_ref[...]
    s = (cc * _sigmoid(cc)).astype(BF16)
    o_ref[...] = jnp.dot(s, w_ref[...].astype(BF16), preferred_element_type=F32) + b_ref[...]


def _adaln(cvecs, w, b):
    R, D = cvecs.shape
    N = w.shape[1]
    tn = 1024
    return pl.pallas_call(
        _adaln_kernel,
        out_shape=jax.ShapeDtypeStruct((R, N), F32),
        grid=(N // tn,),
        in_specs=[pl.BlockSpec((R, D), lambda j: (0, 0)),
                  pl.BlockSpec((D, tn), lambda j: (0, j)),
                  pl.BlockSpec((1, tn), lambda j: (0, j))],
        out_specs=pl.BlockSpec((R, tn), lambda j: (0, j)),
        compiler_params=_params(("arbitrary",)),
        name="adaln",
    )(cvecs, w, b.reshape(1, N))


def _inproj_kernel(*refs, groups, rope):
    if rope:
        (x_ref, shift_ref, scale_ref, g_ref, w_ref, wkg_ref, cos_ref, sin_ref,
         p_ref, kt_ref, gt_ref, h_scr) = refs
    else:
        x_ref, shift_ref, scale_ref, g_ref, w_ref, wkg_ref, p_ref, kt_ref, gt_ref, h_scr = refs
    j = pl.program_id(2)

    @pl.when(j == 0)
    def _():
        xf = x_ref[0]
        ms = jnp.mean(xf * xf, axis=-1, keepdims=True)
        y = xf * lax.rsqrt(ms + NORM_EPS) * g_ref[...]
        hb = (y * (1.0 + scale_ref[0]) + shift_ref[0]).astype(BF16)
        h_scr[...] = hb
        r = lax.dot_general(wkg_ref[...], hb, (((1,), (1,)), ((), ())), preferred_element_type=F32)
        kt_ref[0] = r[:MLSTM_QK_W].astype(BF16)
        gt_ref[0] = r[MLSTM_QK_W:]

    for lo, hi, kinds in groups:
        @pl.when((j >= lo) & (j < hi))
        def _(kinds=kinds):
            acc = jnp.dot(h_scr[...], w_ref[...], preferred_element_type=F32)
            for u, kind in enumerate(kinds):
                a = acc[:, u * LANE:(u + 1) * LANE]
                if "rope" in kind:
                    a = a * cos_ref[...] + pltpu.roll(a, HEAD_DIM // 2, 1) * sin_ref[...]
                if "scale" in kind:
                    a = a * QK_SCALE
                p_ref[0, :, u * LANE:(u + 1) * LANE] = a.astype(BF16)


def _inproj(x, shift, scale, gain, w_main, w_kg_t, kinds, rope_tabs, tm, tn):
    B, T, D = x.shape
    N = w_main.shape[1]
    nj = N // tn
    per = tn // LANE
    tiles = [tuple(kinds[t * per:(t + 1) * per]) for t in range(nj)]
    groups = []
    for t, tk in enumerate(tiles):
        if groups and groups[-1][2] == tk:
            groups[-1] = (groups[-1][0], t + 1, tk)
        else:
            groups.append((t, t + 1, tk))
    bm = shift.shape[0]
    mod_map = (lambda b, i, j: (b, 0, 0)) if bm == B else (lambda b, i, j: (0, 0, 0))
    rope = rope_tabs is not None
    in_specs = [pl.BlockSpec((1, tm, D), lambda b, i, j: (b, i, 0)),
                pl.BlockSpec((1, 1, D), mod_map),
                pl.BlockSpec((1, 1, D), mod_map),
                pl.BlockSpec((1, D), lambda b, i, j: (0, 0)),
                pl.BlockSpec((D, tn), lambda b, i, j: (0, j)),
                pl.BlockSpec(w_kg_t.shape, lambda b, i, j: (0, 0))]
    args = [x, shift, scale, gain.reshape(1, D), w_main, w_kg_t]
    if rope:
        in_specs += [pl.BlockSpec((tm, LANE), lambda b, i, j: (i, 0))] * 2
        args += list(rope_tabs)
    nkg = w_kg_t.shape[0]
    return pl.pallas_call(
        functools.partial(_inproj_kernel, groups=tuple(groups), rope=rope),
        out_shape=(jax.ShapeDtypeStruct((B, T, N), BF16),
                   jax.ShapeDtypeStruct((B, MLSTM_QK_W, T), BF16),
                   jax.ShapeDtypeStruct((B, nkg - MLSTM_QK_W, T), F32)),
        grid=(B, T // tm, nj),
        in_specs=in_specs,
        out_specs=(pl.BlockSpec((1, tm, tn), lambda b, i, j: (b, i, j)),
                   pl.BlockSpec((1, MLSTM_QK_W, tm), lambda b, i, j: (b, 0, i)),
                   pl.BlockSpec((1, nkg - MLSTM_QK_W, tm), lambda b, i, j: (b, 0, i))),
        scratch_shapes=[pltpu.VMEM((tm, D), BF16)],
        compiler_params=_params(("parallel", "parallel", "arbitrary")),
        name="inproj_rope" if rope else "inproj_ctx",
    )(*args)


def _scan_lanes(x, op, reverse, fill):
    lane = lax.broadcasted_iota(jnp.int32, x.shape, 1)
    k = 1
    while k < CHUNK:
        if reverse:
            sh = jnp.where(lane < CHUNK - k, pltpu.roll(x, CHUNK - k, 1), fill)
        else:
            sh = jnp.where(lane >= k, pltpu.roll(x, k, 1), fill)
        x = op(x, sh)
        k *= 2
    return x


def _log_sigmoid(z):
    return jnp.minimum(z, 0.0) - jnp.log(1.0 + jnp.exp(-jnp.abs(z)))


def _gate_prep_kernel(gt_ref, bias_ref, row_ref, col_ref, *, nchunk):
    H = MLSTM_HEADS
    for c in range(nchunk):
        sl = slice(c * CHUNK, (c + 1) * CHUNK)
        z = gt_ref[0, :, sl] + bias_ref[...]
        li_f, lf_f = z[0:H], _log_sigmoid(z[H:2 * H])
        li_r, lf_r = z[2 * H:3 * H], _log_sigmoid(z[3 * H:4 * H])
        b_f = _scan_lanes(lf_f, jnp.add, False, 0.0)
        b_r = _scan_lanes(lf_r, jnp.add, True, 0.0)
        u_f = li_f - b_f
        u_r = li_r - b_r
        r_f = _scan_lanes(u_f, jnp.maximum, False, -jnp.inf)
        r_r = _scan_lanes(u_r, jnp.maximum, True, -jnp.inf)
        row_ref[0, 0:H, sl] = u_f
        row_ref[0, H:2 * H, sl] = u_r
        stack = jnp.concatenate([r_f, r_r, b_f, b_r, jnp.zeros((LANE - 4 * H, CHUNK), F32)], axis=0)
        col_ref[0, sl, :] = stack.T


def _gate_prep(g_t, bias):
    B, G, T = g_t.shape
    tg = min(T, 8 * CHUNK)
    H = MLSTM_HEADS
    return pl.pallas_call(
        functools.partial(_gate_prep_kernel, nchunk=tg // CHUNK),
        out_shape=(jax.ShapeDtypeStruct((B, 2 * H, T), F32),
                   jax.ShapeDtypeStruct((B, T, LANE), F32)),
        grid=(B, T // tg),
        in_specs=[pl.BlockSpec((1, G, tg), lambda b, i: (b, 0, i)),
                  pl.BlockSpec((G, CHUNK), lambda b, i: (0, 0))],
        out_specs=(pl.BlockSpec((1, 2 * H, tg), lambda b, i: (b, 0, i)),
                   pl.BlockSpec((1, tg, LANE), lambda b, i: (b, i, 0))),
        compiler_params=_params(("parallel", "parallel")),
        name="gate_prep",
    )(g_t, jnp.broadcast_to(bias.reshape(G, 1), (G, CHUNK)))


def _attn_kernel(sink_ref, q_ref, kp_ref, kc_ref, kn_ref, vp_ref, vc_ref, vn_ref, kx_ref, vx_ref, band_ref,
                 o_ref, *, nb):
    g = pl.program_id(1)
    j = pl.program_id(2)
    q = q_ref[0]
    qs = jnp.concatenate([q[:, h * HEAD_DIM:(h + 1) * HEAD_DIM] for h in range(ATTN_GROUP)], axis=0)
    k = jnp.concatenate([kp_ref[0], kc_ref[0], kn_ref[0], kx_ref[0]], axis=0)
    v = jnp.concatenate([vp_ref[0], vc_ref[0], vn_ref[0], vx_ref[0]], axis=0)
    s = lax.dot_general(qs, k, (((1,), (1,)), ((), ())), preferred_element_type=F32)
    rows = ATTN_GROUP * WINDOW
    prev_bias = jnp.where(j > 0, 0.0, NEG)
    next_bias = jnp.where(j < nb - 1, 0.0, NEG)
    col = lax.broadcasted_iota(jnp.int32, (1, 3 * WINDOW), 1)
    edge = jnp.where(col < WINDOW, prev_bias, jnp.where(col >= 2 * WINDOW, next_bias, 0.0))
    s_loc = s[:, :3 * WINDOW] + band_ref[...] + edge
    s_ctx = s[:, 3 * WINDOW:]
    row = lax.broadcasted_iota(jnp.int32, (rows, 1), 0)
    sink = jnp.full((rows, 1), sink_ref[g * ATTN_GROUP], F32)
    for h in range(1, ATTN_GROUP):
        sink = jnp.where(row >= h * WINDOW, sink_ref[g * ATTN_GROUP + h], sink)
    m = jnp.maximum(jnp.maximum(jnp.max(s_loc, axis=1, keepdims=True), jnp.max(s_ctx, axis=1, keepdims=True)),
                    sink)
    p_loc = jnp.exp(s_loc - m)
    p_ctx = jnp.exp(s_ctx - m)
    den = jnp.sum(p_loc, axis=1, keepdims=True) + jnp.sum(p_ctx, axis=1, keepdims=True) + jnp.exp(sink - m)
    p = jnp.concatenate([p_loc, p_ctx], axis=1).astype(BF16)
    o = jnp.dot(p, v, preferred_element_type=F32) * (1.0 / den)
    for h in range(ATTN_GROUP):
        o_ref[0, :, h * HEAD_DIM:(h + 1) * HEAD_DIM] = o[h * WINDOW:(h + 1) * WINDOW].astype(BF16)


def _attention(p_lat, p_ctx, sink, q_blk0, k_blk0, v_blk0, kx_blk0, vx_blk0):
    B, S, _ = p_lat.shape
    C = p_ctx.shape[1]
    nb = S // WINDOW
    gw = ATTN_GROUP * HEAD_DIM
    rows = ATTN_GROUP * WINDOW
    t = np.arange(rows)[:, None] % WINDOW
    d = np.arange(3 * WINDOW)[None, :] - t
    band = jnp.asarray(np.where((d >= 0) & (d <= 2 * WINDOW), 0.0, NEG), F32)

    def kspec(off, blk0):
        return pl.BlockSpec((1, WINDOW, HEAD_DIM),
                            lambda b, g, j: (b, jnp.clip(j + off, 0, nb - 1), blk0 + g))

    return pl.pallas_call(
        functools.partial(_attn_kernel, nb=nb),
        out_shape=jax.ShapeDtypeStruct((B, S, ATTN_Q_W), BF16),
        grid=(B, ATTN_KV_HEADS, nb),
        in_specs=[pl.BlockSpec(memory_space=pltpu.SMEM),
                  pl.BlockSpec((1, WINDOW, gw), lambda b, g, j: (b, j, q_blk0 + g)),
                  kspec(-1, k_blk0), kspec(0, k_blk0), kspec(1, k_blk0),
                  kspec(-1, v_blk0), kspec(0, v_blk0), kspec(1, v_blk0),
                  pl.BlockSpec((1, C, HEAD_DIM), lambda b, g, j: (b, 0, kx_blk0 + g)),
                  pl.BlockSpec((1, C, HEAD_DIM), lambda b, g, j: (b, 0, vx_blk0 + g)),
                  pl.BlockSpec((rows, 3 * WINDOW), lambda b, g, j: (0, 0))],
        out_specs=pl.BlockSpec((1, WINDOW, gw), lambda b, g, j: (b, j, g)),
        compiler_params=_params(("parallel", "parallel", "arbitrary")),
        name="window_attn",
    )(sink, p_lat, p_lat, p_lat, p_lat, p_lat, p_lat, p_lat, p_ctx, p_ctx, band)


def _v_aug(v):
    lane = lax.broadcasted_iota(jnp.int32, (v.shape[0], LANE), 1)
    return jnp.concatenate([v, jnp.where(lane == 0, 1.0, 0.0).astype(BF16)], axis=1)


def _state_update(c_old, m_old, kt, vaug, u_row, r_end, b_end):
    m_end = jnp.maximum(m_old, r_end)
    kts = (kt.astype(F32) * jnp.exp(u_row - m_end)).astype(BF16)
    c_new = jnp.exp(m_old - m_end) * c_old + jnp.dot(kts, vaug, preferred_element_type=F32)
    return c_new, b_end + m_end


def _scan_kernel(kt_ref, v_ref, row_ref, col_ref, c0_ref, m0_ref, cs_ref, ms_ref, cf_ref, mf_ref,
                 c_scr, m_scr, *, reverse, nsteps):
    H = MLSTM_HEADS
    n = pl.program_id(1)

    @pl.when(n == 0)
    def _():
        c_scr[...] = c0_ref[0]
        m_scr[...] = m0_ref[0]

    cols = col_ref[0]
    e = 0 if reverse else CHUNK - 1
    for h in range(H):
        hd = (H + h) if reverse else h
        c_old = c_scr[h]
        m_old = m_scr[h:h + 1, 0:1]
        cs_ref[0, 0, h] = c_old.astype(BF16)
        ms_ref[0, 0, h:h + 1, :] = m_scr[h:h + 1, :]
        kt = kt_ref[0, h * MLSTM_QK_DIM:(h + 1) * MLSTM_QK_DIM, :]
        vaug = _v_aug(v_ref[0, :, h * MLSTM_V_DIM:(h + 1) * MLSTM_V_DIM])
        c_new, m_new = _state_update(c_old, m_old, kt, vaug, row_ref[0, hd:hd + 1, :],
                                     cols[e:e + 1, hd:hd + 1], cols[e:e + 1, 2 * H + hd:2 * H + hd + 1])
        c_scr[h] = c_new
        m_scr[h:h + 1, :] = jnp.broadcast_to(m_new, (1, LANE))

    @pl.when(n == nsteps - 1)
    def _():
        cf_ref[0] = c_scr[...]
        mf_ref[0] = m_scr[...]


def _state_scan(kt, p, v_blk, rows, cols, c0, m0, reverse):
    B, _, T = kt.shape
    nc = T // CHUNK
    H = MLSTM_HEADS
    cidx = (lambda n: nc - 1 - n) if reverse else (lambda n: n)
    return pl.pallas_call(
        functools.partial(_scan_kernel, reverse=reverse, nsteps=nc),
        out_shape=(jax.ShapeDtypeStruct((B, nc, H, MLSTM_QK_DIM, C_AUG), BF16),
                   jax.ShapeDtypeStruct((B, nc, H, LANE), F32),
                   jax.ShapeDtypeStruct((B, H, MLSTM_QK_DIM, C_AUG), F32),
                   jax.ShapeDtypeStruct((B, H, LANE), F32)),
        grid=(B, nc),
        in_specs=[pl.BlockSpec((1, MLSTM_QK_W, CHUNK), lambda b, n: (b, 0, cidx(n))),
                  pl.BlockSpec((1, CHUNK, MLSTM_V_W), lambda b, n: (b, cidx(n), v_blk)),
                  pl.BlockSpec((1, 2 * H, CHUNK), lambda b, n: (b, 0, cidx(n))),
                  pl.BlockSpec((1, CHUNK, LANE), lambda b, n: (b, cidx(n), 0)),
                  pl.BlockSpec((1, H, MLSTM_QK_DIM, C_AUG), lambda b, n: (b, 0, 0, 0)),
                  pl.BlockSpec((1, H, LANE), lambda b, n: (b, 0, 0))],
        out_specs=(pl.BlockSpec((1, 1, H, MLSTM_QK_DIM, C_AUG), lambda b, n: (b, cidx(n), 0, 0, 0)),
                   pl.BlockSpec((1, 1, H, LANE), lambda b, n: (b, cidx(n), 0, 0)),
                   pl.BlockSpec((1, H, MLSTM_QK_DIM, C_AUG), lambda b, n: (b, 0, 0, 0)),
                   pl.BlockSpec((1, H, LANE), lambda b, n: (b, 0, 0))),
        scratch_shapes=[pltpu.VMEM((H, MLSTM_QK_DIM, C_AUG), F32), pltpu.VMEM((H, LANE), F32)],
        compiler_params=_params(("parallel", "arbitrary")),
        name="mlstm_scan_rev" if reverse else "mlstm_scan_fwd",
    )(kt, p, rows, cols, c0, m0)


def _mlstm_out_kernel(q_ref, kt_ref, v_ref, mo_ref, row_ref, col_ref, cr_ref, mr_ref, c0_ref, m0_ref, g_ref,
                      o_ref, c_scr, m_scr):
    H = MLSTM_HEADS
    L = CHUNK
    n = pl.program_id(1)

    @pl.when(n == 0)
    def _():
        c_scr[...] = c0_ref[0]
        m_scr[...] = m0_ref[0]

    cols = col_ref[0]
    ti = lax.broadcasted_iota(jnp.int32, (L, L), 0)
    si = lax.broadcasted_iota(jnp.int32, (L, L), 1)
    lower = si <= ti
    upper = si >= ti
    for h in range(H):
        q = q_ref[0, :, h * MLSTM_QK_DIM:(h + 1) * MLSTM_QK_DIM]
        kt = kt_ref[0, h * MLSTM_QK_DIM:(h + 1) * MLSTM_QK_DIM, :]
        vaug = _v_aug(v_ref[0, :, h * MLSTM_V_DIM:(h + 1) * MLSTM_V_DIM])
        u_f = row_ref[0, h:h + 1, :]
        u_r = row_ref[0, H + h:H + h + 1, :]
        r_f, r_r = cols[:, h:h + 1], cols[:, H + h:H + h + 1]
        b_f, b_r = cols[:, 2 * H + h:2 * H + h + 1], cols[:, 3 * H + h:3 * H + h + 1]
        c_f = c_scr[h]
        m0_f = m_scr[h:h + 1, 0:1]
        m0_r = mr_ref[0, 0, h:h + 1, 0:1]
        mx_f = jnp.maximum(r_f, m0_f)
        mx_r = jnp.maximum(r_r, m0_r)
        pqk = jnp.dot(q, kt, preferred_element_type=F32)
        s_f = pqk * jnp.exp(jnp.where(lower, u_f - mx_f, -jnp.inf))
        s_r = pqk * jnp.exp(jnp.where(upper, u_r - mx_r, -jnp.inf))
        s2 = jnp.concatenate([s_f.astype(BF16), s_r.astype(BF16)], axis=0)
        intra = jnp.dot(s2, vaug, preferred_element_type=F32)
        inter_f = jnp.dot(q, c_f.astype(BF16), preferred_element_type=F32)
        inter_r = jnp.dot(q, cr_ref[0, 0, h], preferred_element_type=F32)
        tot_f = intra[:L] + jnp.exp(m0_f - mx_f) * inter_f
        tot_r = intra[L:] + jnp.exp(m0_r - mx_r) * inter_r
        den_f = jnp.maximum(jnp.abs(tot_f[:, MLSTM_V_DIM:MLSTM_V_DIM + 1]), jnp.exp(-b_f - mx_f))
        den_r = jnp.maximum(jnp.abs(tot_r[:, MLSTM_V_DIM:MLSTM_V_DIM + 1]), jnp.exp(-b_r - mx_r))
        hs = tot_f[:, :MLSTM_V_DIM] * (1.0 / den_f) + tot_r[:, :MLSTM_V_DIM] * (1.0 / den_r)
        vs = slice(h * MLSTM_V_DIM, (h + 1) * MLSTM_V_DIM)
        hn = hs * lax.rsqrt(jnp.mean(hs * hs, axis=-1, keepdims=True) + NORM_EPS) * g_ref[:, vs]
        o_ref[0, :, vs] = (_sigmoid(mo_ref[0, :, vs].astype(F32)) * hn).astype(BF16)
        c_new, m_new = _state_update(c_f, m0_f, kt, vaug, u_f, r_f[L - 1:L], b_f[L - 1:L])
        c_scr[h] = c_new
        m_scr[h:h + 1, :] = jnp.broadcast_to(m_new, (1, LANE))


def _mlstm_out(p, kt, rows, cols, c_rev, m_rev, c0, m0, gain, q_blk, v_blk, mo_blk):
    B, S, _ = p.shape
    nc = S // CHUNK
    H = MLSTM_HEADS
    return pl.pallas_call(
        _mlstm_out_kernel,
        out_shape=jax.ShapeDtypeStruct((B, S, MLSTM_V_W), BF16),
        grid=(B, nc),
        in_specs=[pl.BlockSpec((1, CHUNK, MLSTM_QK_W), lambda b, n: (b, n, q_blk)),
                  pl.BlockSpec((1, MLSTM_QK_W, CHUNK), lambda b, n: (b, 0, n)),
                  pl.BlockSpec((1, CHUNK, MLSTM_V_W), lambda b, n: (b, n, v_blk)),
                  pl.BlockSpec((1, CHUNK, MLSTM_V_W), lambda b, n: (b, n, mo_blk)),
                  pl.BlockSpec((1, 2 * H, CHUNK), lambda b, n: (b, 0, n)),
                  pl.BlockSpec((1, CHUNK, LANE), lambda b, n: (b, n, 0)),
                  pl.BlockSpec((1, 1, H, MLSTM_QK_DIM, C_AUG), lambda b, n: (b, n, 0, 0, 0)),
                  pl.BlockSpec((1, 1, H, LANE), lambda b, n: (b, n, 0, 0)),
                  pl.BlockSpec((1, H, MLSTM_QK_DIM, C_AUG), lambda b, n: (b, 0, 0, 0)),
                  pl.BlockSpec((1, H, LANE), lambda b, n: (b, 0, 0)),
                  pl.BlockSpec((1, MLSTM_V_W), lambda b, n: (0, 0))],
        out_specs=pl.BlockSpec((1, CHUNK, MLSTM_V_W), lambda b, n: (b, n, 0)),
        scratch_shapes=[pltpu.VMEM((H, MLSTM_QK_DIM, C_AUG), F32), pltpu.VMEM((H, LANE), F32)],
        compiler_params=_params(("parallel", "arbitrary")),
        name="mlstm_out",
    )(p, kt, p, p, rows, cols, c_rev, m_rev, c0, m0, gain.reshape(1, MLSTM_V_W))


def _merge_kernel(att_ref, mem_ref, ga_ref, gm_ref, x_ref, gate_ref, wa_ref, wm_ref, wo_ref, o_ref):
    a = jnp.dot(att_ref[0], wa_ref[...], preferred_element_type=F32)
    m = jnp.dot(mem_ref[0], wm_ref[...], preferred_element_type=F32)
    y = _sigmoid(ga_ref[0].astype(F32)) * a + _sigmoid(gm_ref[0].astype(F32)) * m
    z = jnp.dot(y.astype(BF16), wo_ref[...], preferred_element_type=F32)
    o_ref[0] = x_ref[0] + gate_ref[0] * z


def _merge(att, mem, p, ga_blk, gm_blk, x, gate1, w_ap, w_mp, w_out, tm):
    B, S, D = x.shape
    resident = functools.partial(pl.BlockSpec, pipeline_mode=pl.Buffered(1))
    return pl.pallas_call(
        _merge_kernel,
        out_shape=jax.ShapeDtypeStruct((B, S, D), F32),
        grid=(B, S // tm),
        in_specs=[pl.BlockSpec((1, tm, ATTN_Q_W), lambda b, i: (b, i, 0)),
                  pl.BlockSpec((1, tm, MLSTM_V_W), lambda b, i: (b, i, 0)),
                  pl.BlockSpec((1, tm, D), lambda b, i: (b, i, ga_blk)),
                  pl.BlockSpec((1, tm, D), lambda b, i: (b, i, gm_blk)),
                  pl.BlockSpec((1, tm, D), lambda b, i: (b, i, 0)),
                  pl.BlockSpec((1, 1, D), lambda b, i: (b, 0, 0)),
                  resident(w_ap.shape, lambda b, i: (0, 0)),
                  resident(w_mp.shape, lambda b, i: (0, 0)),
                  resident(w_out.shape, lambda b, i: (0, 0))],
        out_specs=pl.BlockSpec((1, tm, D), lambda b, i: (b, i, 0)),
        compiler_params=_params(("parallel", "parallel")),
        name="merge_outproj",
    )(att, mem, p, p, x, gate1, w_ap, w_mp, w_out)


def _ffn_kernel(x_ref, shift_ref, scale_ref, gate_ref, g2_ref, wg_ref, wu_ref, wo_ref, gf_ref, o_ref,
                h_scr, acc_scr, *, nf):
    f = pl.program_id(2)

    @pl.when(f == 0)
    def _():
        xf = x_ref[0]
        ms = jnp.mean(xf * xf, axis=-1, keepdims=True)
        y = xf * lax.rsqrt(ms + NORM_EPS) * g2_ref[...]
        h_scr[...] = (y * (1.0 + scale_ref[0]) + shift_ref[0]).astype(BF16)
        acc_scr[...] = jnp.zeros_like(acc_scr)

    h = h_scr[...]
    gt = jnp.dot(h, wg_ref[...], preferred_element_type=F32)
    up = jnp.dot(h, wu_ref[...], preferred_element_type=F32)
    act = (gt * _sigmoid(gt) * up).astype(BF16)
    acc_scr[...] += jnp.dot(act, wo_ref[...], preferred_element_type=F32)

    @pl.when(f == nf - 1)
    def _():
        x2 = x_ref[0] + gate_ref[0] * acc_scr[...]
        ms = jnp.mean(x2 * x2, axis=-1, keepdims=True)
        o_ref[0] = x2 * lax.rsqrt(ms + NORM_EPS) * gf_ref[...]


def _ffn(x1, shift2, scale2, gate2, norm2_g, w_in, w_out, final_g, tm, tf):
    B, S, D = x1.shape
    dff = w_out.shape[0]
    nf = dff // tf
    return pl.pallas_call(
        functools.partial(_ffn_kernel, nf=nf),
        out_shape=jax.ShapeDtypeStruct((B, S, D), F32),
        grid=(B, S // tm, nf),
        in_specs=[pl.BlockSpec((1, tm, D), lambda b, i, f: (b, i, 0)),
                  pl.BlockSpec((1, 1, D), lambda b, i, f: (b, 0, 0)),
                  pl.BlockSpec((1, 1, D), lambda b, i, f: (b, 0, 0)),
                  pl.BlockSpec((1, 1, D), lambda b, i, f: (b, 0, 0)),
                  pl.BlockSpec((1, D), lambda b, i, f: (0, 0)),
                  pl.BlockSpec((D, tf), lambda b, i, f: (0, f)),
                  pl.BlockSpec((D, tf), lambda b, i, f: (0, nf + f)),
                  pl.BlockSpec((tf, D), lambda b, i, f: (f, 0)),
                  pl.BlockSpec((1, D), lambda b, i, f: (0, 0))],
        out_specs=pl.BlockSpec((1, tm, D), lambda b, i, f: (b, i, 0)),
        scratch_shapes=[pltpu.VMEM((tm, D), BF16), pltpu.VMEM((tm, D), F32)],
        compiler_params=_params(("parallel", "parallel", "arbitrary")),
        name="ffn_final_norm",
    )(x1, shift2, scale2, gate2, norm2_g.reshape(1, D), w_in, w_in, w_out, final_g.reshape(1, D))


def _rope_tables(S):
    quarter = HEAD_DIM // 4
    pos = jnp.arange(S)
    rows = (pos // GRID_W).astype(F32)
    cols = (pos % GRID_W).astype(F32)
    inv_freq = ROPE_BASE ** (-jnp.arange(quarter, dtype=F32) / quarter)
    ar = rows[:, None] * inv_freq[None, :]
    ac = cols[:, None] * inv_freq[None, :]
    cos = jnp.concatenate([jnp.cos(ar), jnp.cos(ac), jnp.cos(ar), jnp.cos(ac)], axis=1)
    sin = jnp.concatenate([-jnp.sin(ar), -jnp.sin(ac), jnp.sin(ar), jnp.sin(ac)], axis=1)
    return cos, sin


def _rope_perm(n_heads):
    q = HEAD_DIM // 4
    one = np.concatenate([np.arange(0, q), np.arange(2 * q, 3 * q), np.arange(q, 2 * q), np.arange(3 * q, 4 * q)])
    return np.concatenate([h * HEAD_DIM + one for h in range(n_heads)])


def kernel(x, c, ctx, c_ctx, w_ada, b_ada, norm1_g, w_in, b_gates, attn_sink, mlstm_norm_g, w_attn_proj,
           w_mlstm_proj, w_out, norm2_g, w_ffn_in, w_ffn_out, final_norm_g):
    B, S, D = x.shape
    C = ctx.shape[1]
    assert w_ada.shape[0] == 1, "single-layer configuration"
    assert S % 512 == 0 and C % CHUNK == 0 and S % GRID_W == 0
    H = MLSTM_HEADS

    rows = -(-(B + 1) // 8) * 8
    cvecs = jnp.concatenate([c, c_ctx[None], jnp.zeros((rows - B - 1, D), F32)], axis=0)
    mod = _adaln(cvecs, w_ada[0], b_ada[0])
    shift1, scale1, gate1, shift2, scale2, gate2 = [mod[:B, k * D:(k + 1) * D].reshape(B, 1, D) for k in range(N_MOD)]
    shift_c = mod[B:B + 1, 0:D].reshape(1, 1, D)
    scale_c = mod[B:B + 1, D:2 * D].reshape(1, 1, D)

    wi = w_in[0]
    o = 0
    parts = {}
    for name, width in (("a_k", ATTN_KV_W), ("a_v", ATTN_KV_W), ("m_k", MLSTM_QK_W), ("m_v", MLSTM_V_W),
                        ("m_g", N_GATE), ("a_q", ATTN_Q_W), ("m_q", MLSTM_QK_W), ("m_o", MLSTM_V_W),
                        ("g_att", D), ("g_mem", D)):
        parts[name] = wi[:, o:o + width]
        o += width
    a_q = parts["a_q"][:, _rope_perm(ATTN_HEADS)]
    a_k = parts["a_k"][:, _rope_perm(ATTN_KV_HEADS)]
    w_lat = jnp.concatenate([a_q, a_k, parts["a_v"], parts["m_q"], parts["m_v"], parts["m_o"], parts["g_att"],
                             parts["g_mem"]], axis=1).astype(BF16)
    w_ctx = jnp.concatenate([parts["m_v"], a_k, parts["a_v"]], axis=1).astype(BF16)
    w_kg_t = jnp.concatenate([parts["m_k"], parts["m_g"]], axis=1).T.astype(BF16)
    kinds = (["rope scale"] * (ATTN_Q_W // LANE) + ["rope"] * (ATTN_KV_W // LANE) + [""] * (ATTN_KV_W // LANE)
             + ["scale"] * (MLSTM_QK_W // LANE) + [""] * ((2 * MLSTM_V_W + 2 * D) // LANE))
    q_blk0, k_blk0, v_blk0 = 0, ATTN_Q_W // HEAD_DIM, (ATTN_Q_W + ATTN_KV_W) // HEAD_DIM
    off = ATTN_Q_W + 2 * ATTN_KV_W
    mq_blk = off // MLSTM_QK_W
    mv_blk = (off + MLSTM_QK_W) // MLSTM_V_W
    mo_blk = mv_blk + 1
    ga_blk = (off + MLSTM_QK_W + 2 * MLSTM_V_W) // D
    gm_blk = ga_blk + 1
    kx_blk0, vx_blk0 = MLSTM_V_W // HEAD_DIM, (MLSTM_V_W + ATTN_KV_W) // HEAD_DIM

    p_lat, kt_lat, gt_lat = _inproj(x, shift1, scale1, norm1_g[0], w_lat, w_kg_t, kinds, _rope_tables(S),
                                    tm=512, tn=1024)
    p_ctx, kt_ctx, gt_ctx = _inproj(ctx, shift_c, scale_c, norm1_g[0], w_ctx, w_kg_t,
                                    [""] * (w_ctx.shape[1] // LANE), None, tm=min(C, 256), tn=1024)

    att = _attention(p_lat, p_ctx, attn_sink[0], q_blk0, k_blk0, v_blk0, kx_blk0, vx_blk0)

    rows_lat, cols_lat = _gate_prep(gt_lat, b_gates[0])
    rows_ctx, cols_ctx = _gate_prep(gt_ctx, b_gates[0])
    c_zero = jnp.zeros((B, H, MLSTM_QK_DIM, C_AUG), F32)
    m_zero = jnp.zeros((B, H, LANE), F32)
    _, _, cf_ctx, mf_ctx = _state_scan(kt_ctx, p_ctx, 0, rows_ctx, cols_ctx, c_zero, m_zero, reverse=False)
    _, _, cr_ctx, mr_ctx = _state_scan(kt_ctx, p_ctx, 0, rows_ctx, cols_ctx, c_zero, m_zero, reverse=True)
    c_rev, m_rev, _, _ = _state_scan(kt_lat, p_lat, mv_blk, rows_lat, cols_lat, cr_ctx, mr_ctx, reverse=True)
    mem = _mlstm_out(p_lat, kt_lat, rows_lat, cols_lat, c_rev, m_rev, cf_ctx, mf_ctx, mlstm_norm_g[0],
                     mq_blk, mv_blk, mo_blk)

    x1 = _merge(att, mem, p_lat, ga_blk, gm_blk, x, gate1, w_attn_proj[0].astype(BF16),
                w_mlstm_proj[0].astype(BF16), w_out[0].astype(BF16), tm=256)
    return _ffn(x1, shift2, scale2, gate2, norm2_g[0], w_ffn_in[0].astype(BF16), w_ffn_out[0].astype(BF16),
                final_norm_g, tm=512, tf=512)
```

```python
import functools

import numpy as np
import jax
import jax.numpy as jnp
from jax import lax
from jax.experimental import pallas as pl
from jax.experimental.pallas import tpu as pltpu

F32 = jnp.float32
BF16 = jnp.bfloat16

GRID_W = 64
ATTN_HEADS = 16
ATTN_KV_HEADS = 4
ATTN_GROUP = ATTN_HEADS // ATTN_KV_HEADS
HEAD_DIM = 128
WINDOW = 128
ROPE_BASE = 10000.0
MLSTM_HEADS = 8
MLSTM_QK_DIM = 128
MLSTM_V_DIM = 256
CHUNK = 128
N_DIRS = 2
N_GATE = N_DIRS * 2 * MLSTM_HEADS
NORM_EPS = 1e-6
N_MOD = 6
QK_SCALE = HEAD_DIM ** -0.5

LANE = 128
BF16_SUBLANES = 16
V7X_VMEM_BYTES = 64 * 1024 * 1024
VMEM_LIMIT = V7X_VMEM_BYTES - 8 * 1024 * 1024

ATTN_Q_W = ATTN_HEADS * HEAD_DIM
ATTN_KV_W = ATTN_KV_HEADS * HEAD_DIM
MLSTM_QK_W = MLSTM_HEADS * MLSTM_QK_DIM
MLSTM_V_W = MLSTM_HEADS * MLSTM_V_DIM
C_ROWS = MLSTM_V_DIM + BF16_SUBLANES

NEG = -1e30


def _params(sem):
    return pltpu.CompilerParams(dimension_semantics=sem, vmem_limit_bytes=VMEM_LIMIT)


def _sigmoid(x):
    return 1.0 / (1.0 + jnp.exp(-x))


def _adaln_kernel(c_ref, w_ref, b_ref, o_ref):
    cc = c_ref[...]
    s = (cc * _sigmoid(cc)).astype(BF16)
    o_ref[...] = jnp.dot(s, w_ref[...].astype(BF16), preferred_element_type=F32) + b_ref[...]


def _adaln(cvecs, w, b):
    R, D = cvecs.shape
    N = w.shape[1]
    tn = 1024
    return pl.pallas_call(
        _adaln_kernel,
        out_shape=jax.ShapeDtypeStruct((R, N), F32),
        grid=(N // tn,),
        in_specs=[pl.BlockSpec((R, D), lambda j: (0, 0)),
                  pl.BlockSpec((D, tn), lambda j: (0, j)),
                  pl.BlockSpec((1, tn), lambda j: (0, j))],
        out_specs=pl.BlockSpec((R, tn), lambda j: (0, j)),
        compiler_params=_params(("arbitrary",)),
        name="adaln",
    )(cvecs, w, b.reshape(1, N))


def _tile_groups(kinds, per):
    tiles = [tuple(kinds[t * per:(t + 1) * per]) for t in range(len(kinds) // per)]
    groups = []
    for t, tk in enumerate(tiles):
        if groups and groups[-1][2] == tk:
            groups[-1] = (groups[-1][0], t + 1, tk)
        else:
            groups.append((t, t + 1, tk))
    return tuple(groups)


def _inproj_kernel(*refs, n_groups, t_groups, nn, rope):
    if rope:
        (x_ref, shift_ref, scale_ref, g_ref, wn_ref, wt_ref, wg_ref, cos_ref, sin_ref,
         p_ref, pt_ref, gt_ref, h_scr) = refs
    else:
        x_ref, shift_ref, scale_ref, g_ref, wn_ref, wt_ref, wg_ref, p_ref, pt_ref, gt_ref, h_scr = refs
    j = pl.program_id(2)
    nt_dims = (((1,), (1,)), ((), ()))

    @pl.when(j == 0)
    def _():
        xf = x_ref[0]
        ms = jnp.mean(xf * xf, axis=-1, keepdims=True)
        y = xf * lax.rsqrt(ms + NORM_EPS) * g_ref[...]
        hb = (y * (1.0 + scale_ref[0]) + shift_ref[0]).astype(BF16)
        h_scr[...] = hb
        gt_ref[0] = lax.dot_general(wg_ref[...], hb, nt_dims, preferred_element_type=F32)

    for lo, hi, kinds in n_groups:
        @pl.when((j >= lo) & (j < hi))
        def _(kinds=kinds):
            acc = jnp.dot(h_scr[...], wn_ref[...], preferred_element_type=F32)
            for u, kind in enumerate(kinds):
                a = acc[:, u * LANE:(u + 1) * LANE]
                if "rope" in kind:
                    a = a * cos_ref[...] + pltpu.roll(a, HEAD_DIM // 2, 1) * sin_ref[...]
                if "scale" in kind:
                    a = a * QK_SCALE
                p_ref[0, :, u * LANE:(u + 1) * LANE] = a.astype(BF16)

    for lo, hi, kinds in t_groups:
        @pl.when((j >= nn + lo) & (j < nn + hi))
        def _(kinds=kinds):
            acc = lax.dot_general(wt_ref[...], h_scr[...], nt_dims, preferred_element_type=F32)
            for u, kind in enumerate(kinds):
                a = acc[u * LANE:(u + 1) * LANE]
                if "scale" in kind:
                    a = a * QK_SCALE
                pt_ref[0, u * LANE:(u + 1) * LANE, :] = a.astype(BF16)


def _inproj(x, shift, scale, gain, w_nat, w_t, w_g, kinds_nat, kinds_t, rope_tabs, tm, tn):
    B, T, D = x.shape
    n_nat = w_nat.shape[1]
    n_t = w_t.shape[0]
    nn, ntt = n_nat // tn, n_t // tn
    per = tn // LANE
    bm = shift.shape[0]
    mod_map = (lambda b, i, j: (b, 0, 0)) if bm == B else (lambda b, i, j: (0, 0, 0))
    rope = rope_tabs is not None
    in_specs = [pl.BlockSpec((1, tm, D), lambda b, i, j: (b, i, 0)),
                pl.BlockSpec((1, 1, D), mod_map),
                pl.BlockSpec((1, 1, D), mod_map),
                pl.BlockSpec((1, D), lambda b, i, j: (0, 0)),
                pl.BlockSpec((D, tn), lambda b, i, j: (0, jnp.minimum(j, nn - 1))),
                pl.BlockSpec((tn, D), lambda b, i, j: (jnp.maximum(j - nn, 0), 0)),
                pl.BlockSpec(w_g.shape, lambda b, i, j: (0, 0))]
    args = [x, shift, scale, gain.reshape(1, D), w_nat, w_t, w_g]
    if rope:
        in_specs += [pl.BlockSpec((tm, LANE), lambda b, i, j: (i, 0))] * 2
        args += list(rope_tabs)
    return pl.pallas_call(
        functools.partial(_inproj_kernel, n_groups=_tile_groups(kinds_nat, per),
                          t_groups=_tile_groups(kinds_t, per), nn=nn, rope=rope),
        out_shape=(jax.ShapeDtypeStruct((B, T, n_nat), BF16),
                   jax.ShapeDtypeStruct((B, n_t, T), BF16),
                   jax.ShapeDtypeStruct((B, w_g.shape[0], T), F32)),
        grid=(B, T // tm, nn + ntt),
        in_specs=in_specs,
        out_specs=(pl.BlockSpec((1, tm, tn), lambda b, i, j: (b, i, jnp.minimum(j, nn - 1))),
                   pl.BlockSpec((1, tn, tm), lambda b, i, j: (b, jnp.maximum(j - nn, 0), i)),
                   pl.BlockSpec((1, w_g.shape[0], tm), lambda b, i, j: (b, 0, i))),
        scratch_shapes=[pltpu.VMEM((tm, D), BF16)],
        compiler_params=_params(("parallel", "parallel", "arbitrary")),
        name="inproj_rope" if rope else "inproj_ctx",
    )(*args)


def _scan_lanes(x, op, reverse, fill):
    lane = lax.broadcasted_iota(jnp.int32, x.shape, 1)
    k = 1
    while k < CHUNK:
        if reverse:
            sh = jnp.where(lane < CHUNK - k, pltpu.roll(x, CHUNK - k, 1), fill)
        else:
            sh = jnp.where(lane >= k, pltpu.roll(x, k, 1), fill)
        x = op(x, sh)
        k *= 2
    return x


def _log_sigmoid(z):
    return jnp.minimum(z, 0.0) - jnp.log(1.0 + jnp.exp(-jnp.abs(z)))


def _lane_value(x, lane_idx):
    lane = lax.broadcasted_iota(jnp.int32, x.shape, 1)
    return jnp.broadcast_to(jnp.sum(jnp.where(lane == lane_idx, x, 0.0), axis=1, keepdims=True), x.shape)


def _gate_prep_kernel(gt_ref, bias_ref, row_ref, col_ref, *, nchunk):
    H = MLSTM_HEADS
    for c in range(nchunk):
        sl = slice(c * CHUNK, (c + 1) * CHUNK)
        z = gt_ref[0, :, sl] + bias_ref[...]
        li_f, lf_f = z[0:H], _log_sigmoid(z[H:2 * H])
        li_r, lf_r = z[2 * H:3 * H], _log_sigmoid(z[3 * H:4 * H])
        b_f = _scan_lanes(lf_f, jnp.add, False, 0.0)
        b_r = _scan_lanes(lf_r, jnp.add, True, 0.0)
        u_f = li_f - b_f
        u_r = li_r - b_r
        r_f = _scan_lanes(u_f, jnp.maximum, False, -jnp.inf)
        r_r = _scan_lanes(u_r, jnp.maximum, True, -jnp.inf)
        ends = [_lane_value(r_f, CHUNK - 1), _lane_value(r_r, 0), _lane_value(b_f, CHUNK - 1), _lane_value(b_r, 0)]
        for k, v in enumerate([r_f, r_r, b_f, b_r] + ends):
            row_ref[0, k * H:(k + 1) * H, sl] = v
        stack = jnp.concatenate([u_f, u_r, jnp.zeros((LANE - 2 * H, CHUNK), F32)], axis=0)
        col_ref[0, sl, :] = stack.T


def _gate_prep(g_t, bias):
    B, G, T = g_t.shape
    tg = min(T, 8 * CHUNK)
    H = MLSTM_HEADS
    return pl.pallas_call(
        functools.partial(_gate_prep_kernel, nchunk=tg // CHUNK),
        out_shape=(jax.ShapeDtypeStruct((B, 8 * H, T), F32),
                   jax.ShapeDtypeStruct((B, T, LANE), F32)),
        grid=(B, T // tg),
        in_specs=[pl.BlockSpec((1, G, tg), lambda b, i: (b, 0, i)),
                  pl.BlockSpec((G, CHUNK), lambda b, i: (0, 0))],
        out_specs=(pl.BlockSpec((1, 8 * H, tg), lambda b, i: (b, 0, i)),
                   pl.BlockSpec((1, tg, LANE), lambda b, i: (b, i, 0))),
        compiler_params=_params(("parallel", "parallel")),
        name="gate_prep",
    )(g_t, jnp.broadcast_to(bias.reshape(G, 1), (G, CHUNK)))


def _attn_kernel(sink_ref, q_ref, kp_ref, kc_ref, kn_ref, vp_ref, vc_ref, vn_ref, kx_ref, vx_ref, band_ref,
                 o_ref, *, nb):
    j = pl.program_id(1)
    rows = ATTN_GROUP * WINDOW
    prev_bias = jnp.where(j > 0, 0.0, NEG)
    next_bias = jnp.where(j < nb - 1, 0.0, NEG)
    col = lax.broadcasted_iota(jnp.int32, (1, 3 * WINDOW), 1)
    bias = band_ref[...] + jnp.where(col < WINDOW, prev_bias, jnp.where(col >= 2 * WINDOW, next_bias, 0.0))
    row = lax.broadcasted_iota(jnp.int32, (rows, 1), 0)
    for g in range(ATTN_KV_HEADS):
        hs = slice(g * HEAD_DIM, (g + 1) * HEAD_DIM)
        qs = jnp.concatenate([q_ref[0, :, (g * ATTN_GROUP + h) * HEAD_DIM:(g * ATTN_GROUP + h + 1) * HEAD_DIM]
                              for h in range(ATTN_GROUP)], axis=0)
        k = jnp.concatenate([kp_ref[0, :, hs], kc_ref[0, :, hs], kn_ref[0, :, hs], kx_ref[0, :, hs]], axis=0)
        v = jnp.concatenate([vp_ref[0, :, hs], vc_ref[0, :, hs], vn_ref[0, :, hs], vx_ref[0, :, hs]], axis=0)
        s = lax.dot_general(qs, k, (((1,), (1,)), ((), ())), preferred_element_type=F32)
        s_loc = s[:, :3 * WINDOW] + bias
        s_ctx = s[:, 3 * WINDOW:]
        sink = jnp.full((rows, 1), sink_ref[g * ATTN_GROUP], F32)
        for h in range(1, ATTN_GROUP):
            sink = jnp.where(row >= h * WINDOW, sink_ref[g * ATTN_GROUP + h], sink)
        m = jnp.maximum(jnp.maximum(jnp.max(s_loc, axis=1, keepdims=True),
                                    jnp.max(s_ctx, axis=1, keepdims=True)), sink)
        p_loc = jnp.exp(s_loc - m)
        p_ctx = jnp.exp(s_ctx - m)
        den = jnp.sum(p_loc, axis=1, keepdims=True) + jnp.sum(p_ctx, axis=1, keepdims=True) + jnp.exp(sink - m)
        p = jnp.concatenate([p_loc, p_ctx], axis=1).astype(BF16)
        o = jnp.dot(p, v, preferred_element_type=F32) * (1.0 / den)
        for h in range(ATTN_GROUP):
            c0 = (g * ATTN_GROUP + h) * HEAD_DIM
            o_ref[0, :, c0:c0 + HEAD_DIM] = o[h * WINDOW:(h + 1) * WINDOW].astype(BF16)


def _attention(p_lat, p_ctx, sink, q_blk, k_blk, v_blk, kx_blk, vx_blk):
    B, S, _ = p_lat.shape
    C = p_ctx.shape[1]
    nb = S // WINDOW
    rows = ATTN_GROUP * WINDOW
    t = np.arange(rows)[:, None] % WINDOW
    d = np.arange(3 * WINDOW)[None, :] - t
    band = jnp.asarray(np.where((d >= 0) & (d <= 2 * WINDOW), 0.0, NEG), F32)

    def kspec(off, blk):
        return pl.BlockSpec((1, WINDOW, ATTN_KV_W), lambda b, j: (b, jnp.clip(j + off, 0, nb - 1), blk))

    return pl.pallas_call(
        functools.partial(_attn_kernel, nb=nb),
        out_shape=jax.ShapeDtypeStruct((B, S, ATTN_Q_W), BF16),
        grid=(B, nb),
        in_specs=[pl.BlockSpec(memory_space=pltpu.SMEM),
                  pl.BlockSpec((1, WINDOW, ATTN_Q_W), lambda b, j: (b, j, q_blk)),
                  kspec(-1, k_blk), kspec(0, k_blk), kspec(1, k_blk),
                  kspec(-1, v_blk), kspec(0, v_blk), kspec(1, v_blk),
                  pl.BlockSpec((1, C, ATTN_KV_W), lambda b, j: (b, 0, kx_blk)),
                  pl.BlockSpec((1, C, ATTN_KV_W), lambda b, j: (b, 0, vx_blk)),
                  pl.BlockSpec((rows, 3 * WINDOW), lambda b, j: (0, 0))],
        out_specs=pl.BlockSpec((1, WINDOW, ATTN_Q_W), lambda b, j: (b, j, 0)),
        compiler_params=_params(("parallel", "arbitrary")),
        name="window_attn",
    )(sink, p_lat, p_lat, p_lat, p_lat, p_lat, p_lat, p_lat, p_ctx, p_ctx, band)


def _v_aug_t(vt):
    sub = lax.broadcasted_iota(jnp.int32, (BF16_SUBLANES, vt.shape[1]), 0)
    return jnp.concatenate([vt, jnp.where(sub == 0, 1.0, 0.0).astype(BF16)], axis=0)


def _state_update(ct_old, m_old, k, vaug_t, u_bc, r_end, b_end):
    m_end = jnp.maximum(m_old, r_end)
    ks = (k.astype(F32) * jnp.exp(u_bc - m_end)).astype(BF16)
    ct_new = jnp.exp(m_old - m_end) * ct_old + jnp.dot(vaug_t, ks, preferred_element_type=F32)
    return ct_new, b_end + m_end


def _row(row_ref, k, h):
    i = k * MLSTM_HEADS + h
    return row_ref[0, i:i + 1, :]


def _scan_kernel(k_ref, vt_ref, row_ref, col_ref, c0_ref, m0_ref, cs_ref, ms_ref, cf_ref, mf_ref,
                 c_scr, m_scr, *, reverse, nsteps):
    H = MLSTM_HEADS
    n = pl.program_id(1)
    d = 1 if reverse else 0

    @pl.when(n == 0)
    def _():
        c_scr[...] = c0_ref[0]
        m_scr[...] = m0_ref[0]

    cols = col_ref[0]
    for h in range(H):
        ct_old = c_scr[h]
        m_old = m_scr[h:h + 1, :]
        cs_ref[0, 0, h] = ct_old.astype(BF16)
        ms_ref[0, 0, h:h + 1, :] = m_old
        k = k_ref[0, :, h * MLSTM_QK_DIM:(h + 1) * MLSTM_QK_DIM]
        vaug_t = _v_aug_t(vt_ref[0, h * MLSTM_V_DIM:(h + 1) * MLSTM_V_DIM, :])
        u_bc = jnp.broadcast_to(cols[:, d * H + h:d * H + h + 1], (CHUNK, LANE))
        ct_new, m_new = _state_update(ct_old, m_old, k, vaug_t, u_bc, _row(row_ref, 4 + d, h), _row(row_ref, 6 + d, h))
        c_scr[h] = ct_new
        m_scr[h:h + 1, :] = m_new

    @pl.when(n == nsteps - 1)
    def _():
        cf_ref[0] = c_scr[...]
        mf_ref[0] = m_scr[...]


def _state_scan(p, k_blk, p_t, vt_blk, rows, cols, c0, m0, reverse):
    B, T, _ = p.shape
    nc = T // CHUNK
    H = MLSTM_HEADS
    cidx = (lambda n: nc - 1 - n) if reverse else (lambda n: n)
    st = (H, C_ROWS, MLSTM_QK_DIM)
    return pl.pallas_call(
        functools.partial(_scan_kernel, reverse=reverse, nsteps=nc),
        out_shape=(jax.ShapeDtypeStruct((B, nc) + st, BF16),
                   jax.ShapeDtypeStruct((B, nc, H, LANE), F32),
                   jax.ShapeDtypeStruct((B,) + st, F32),
                   jax.ShapeDtypeStruct((B, H, LANE), F32)),
        grid=(B, nc),
        in_specs=[pl.BlockSpec((1, CHUNK, MLSTM_QK_W), lambda b, n: (b, cidx(n), k_blk)),
                  pl.BlockSpec((1, MLSTM_V_W, CHUNK), lambda b, n: (b, vt_blk, cidx(n))),
                  pl.BlockSpec((1, 8 * H, CHUNK), lambda b, n: (b, 0, cidx(n))),
                  pl.BlockSpec((1, CHUNK, LANE), lambda b, n: (b, cidx(n), 0)),
                  pl.BlockSpec((1,) + st, lambda b, n: (b, 0, 0, 0)),
                  pl.BlockSpec((1, H, LANE), lambda b, n: (b, 0, 0))],
        out_specs=(pl.BlockSpec((1, 1) + st, lambda b, n: (b, cidx(n), 0, 0, 0)),
                   pl.BlockSpec((1, 1, H, LANE), lambda b, n: (b, cidx(n), 0, 0)),
                   pl.BlockSpec((1,) + st, lambda b, n: (b, 0, 0, 0)),
                   pl.BlockSpec((1, H, LANE), lambda b, n: (b, 0, 0))),
        scratch_shapes=[pltpu.VMEM(st, F32), pltpu.VMEM((H, LANE), F32)],
        compiler_params=_params(("parallel", "arbitrary")),
        name="mlstm_scan_rev" if reverse else "mlstm_scan_fwd",
    )(p, p_t, rows, cols, c0, m0)


def _mlstm_out_kernel(k_ref, qt_ref, vt_ref, mot_ref, row_ref, col_ref, cr_ref, mr_ref, c0_ref, m0_ref, g_ref,
                      o_ref, c_scr, m_scr):
    H = MLSTM_HEADS
    L = CHUNK
    n = pl.program_id(1)

    @pl.when(n == 0)
    def _():
        c_scr[...] = c0_ref[0]
        m_scr[...] = m0_ref[0]

    cols = col_ref[0]
    si = lax.broadcasted_iota(jnp.int32, (L, L), 0)
    ti = lax.broadcasted_iota(jnp.int32, (L, L), 1)
    for h in range(H):
        qt = qt_ref[0, h * MLSTM_QK_DIM:(h + 1) * MLSTM_QK_DIM, :]
        k = k_ref[0, :, h * MLSTM_QK_DIM:(h + 1) * MLSTM_QK_DIM]
        vs = slice(h * MLSTM_V_DIM, (h + 1) * MLSTM_V_DIM)
        vaug_t = _v_aug_t(vt_ref[0, vs, :])
        pt = jnp.dot(k, qt, preferred_element_type=F32)
        qt_f = qt.astype(F32)
        ct_f = c_scr[h]
        hs_t = None
        for d in range(N_DIRS):
            u_bc = jnp.broadcast_to(cols[:, d * H + h:d * H + h + 1], (L, LANE))
            m0 = m_scr[h:h + 1, :] if d == 0 else mr_ref[0, 0, h:h + 1, :]
            ct = ct_f.astype(BF16) if d == 0 else cr_ref[0, 0, h]
            mx = jnp.maximum(_row(row_ref, d, h), m0)
            valid = (si <= ti) if d == 0 else (si >= ti)
            s_t = (pt * jnp.exp(jnp.where(valid, u_bc - mx, -jnp.inf))).astype(BF16)
            q_in = (qt_f * jnp.exp(m0 - mx)).astype(BF16)
            tot = jnp.dot(jnp.concatenate([vaug_t, ct], axis=1), jnp.concatenate([s_t, q_in], axis=0),
                          preferred_element_type=F32)
            den = jnp.maximum(jnp.abs(tot[MLSTM_V_DIM:MLSTM_V_DIM + 1]), jnp.exp(-_row(row_ref, 2 + d, h) - mx))
            part = tot[:MLSTM_V_DIM] * (1.0 / den)
            hs_t = part if hs_t is None else hs_t + part
            if d == 0:
                u_bc_f = u_bc
        hn = hs_t * lax.rsqrt(jnp.mean(hs_t * hs_t, axis=0, keepdims=True) + NORM_EPS) * g_ref[vs, :]
        o_ref[0, vs, :] = (_sigmoid(mot_ref[0, vs, :].astype(F32)) * hn).astype(BF16)
        ct_new, m_new = _state_update(ct_f, m_scr[h:h + 1, :], k, vaug_t, u_bc_f, _row(row_ref, 4, h),
                                      _row(row_ref, 6, h))
        c_scr[h] = ct_new
        m_scr[h:h + 1, :] = m_new


def _mlstm_out(p, k_blk, p_t, qt_blk, vt_blk, mot_blk, rows, cols, c_rev, m_rev, c0, m0, gain):
    B, S, _ = p.shape
    nc = S // CHUNK
    H = MLSTM_HEADS
    st = (H, C_ROWS, MLSTM_QK_DIM)
    gain_bc = jnp.broadcast_to(gain.reshape(MLSTM_V_W, 1), (MLSTM_V_W, LANE))
    return pl.pallas_call(
        _mlstm_out_kernel,
        out_shape=jax.ShapeDtypeStruct((B, MLSTM_V_W, S), BF16),
        grid=(B, nc),
        in_specs=[pl.BlockSpec((1, CHUNK, MLSTM_QK_W), lambda b, n: (b, n, k_blk)),
                  pl.BlockSpec((1, MLSTM_QK_W, CHUNK), lambda b, n: (b, qt_blk, n)),
                  pl.BlockSpec((1, MLSTM_V_W, CHUNK), lambda b, n: (b, vt_blk, n)),
                  pl.BlockSpec((1, MLSTM_V_W, CHUNK), lambda b, n: (b, mot_blk, n)),
                  pl.BlockSpec((1, 8 * H, CHUNK), lambda b, n: (b, 0, n)),
                  pl.BlockSpec((1, CHUNK, LANE), lambda b, n: (b, n, 0)),
                  pl.BlockSpec((1, 1) + st, lambda b, n: (b, n, 0, 0, 0)),
                  pl.BlockSpec((1, 1, H, LANE), lambda b, n: (b, n, 0, 0)),
                  pl.BlockSpec((1,) + st, lambda b, n: (b, 0, 0, 0)),
                  pl.BlockSpec((1, H, LANE), lambda b, n: (b, 0, 0)),
                  pl.BlockSpec((MLSTM_V_W, LANE), lambda b, n: (0, 0))],
        out_specs=pl.BlockSpec((1, MLSTM_V_W, CHUNK), lambda b, n: (b, 0, n)),
        scratch_shapes=[pltpu.VMEM(st, F32), pltpu.VMEM((H, LANE), F32)],
        compiler_params=_params(("parallel", "arbitrary")),
        name="mlstm_out",
    )(p, p_t, p_t, p_t, rows, cols, c_rev, m_rev, c0, m0, gain_bc)


def _merge_kernel(att_ref, memt_ref, ga_ref, gm_ref, x_ref, gate_ref, wa_ref, wm_ref, wo_ref, o_ref):
    a = jnp.dot(att_ref[0], wa_ref[...], preferred_element_type=F32)
    m = lax.dot_general(memt_ref[0], wm_ref[...], (((0,), (0,)), ((), ())), preferred_element_type=F32)
    y = _sigmoid(ga_ref[0].astype(F32)) * a + _sigmoid(gm_ref[0].astype(F32)) * m
    z = jnp.dot(y.astype(BF16), wo_ref[...], preferred_element_type=F32)
    o_ref[0] = x_ref[0] + gate_ref[0] * z


def _merge(att, mem_t, p, ga_blk, gm_blk, x, gate1, w_ap, w_mp, w_out, tm):
    B, S, D = x.shape
    resident = functools.partial(pl.BlockSpec, pipeline_mode=pl.Buffered(1))
    return pl.pallas_call(
        _merge_kernel,
        out_shape=jax.ShapeDtypeStruct((B, S, D), F32),
        grid=(B, S // tm),
        in_specs=[pl.BlockSpec((1, tm, ATTN_Q_W), lambda b, i: (b, i, 0)),
                  pl.BlockSpec((1, MLSTM_V_W, tm), lambda b, i: (b, 0, i)),
                  pl.BlockSpec((1, tm, D), lambda b, i: (b, i, ga_blk)),
                  pl.BlockSpec((1, tm, D), lambda b, i: (b, i, gm_blk)),
                  pl.BlockSpec((1, tm, D), lambda b, i: (b, i, 0)),
                  pl.BlockSpec((1, 1, D), lambda b, i: (b, 0, 0)),
                  resident(w_ap.shape, lambda b, i: (0, 0)),
                  resident(w_mp.shape, lambda b, i: (0, 0)),
                  resident(w_out.shape, lambda b, i: (0, 0))],
        out_specs=pl.BlockSpec((1, tm, D), lambda b, i: (b, i, 0)),
        compiler_params=_params(("parallel", "parallel")),
        name="merge_outproj",
    )(att, mem_t, p, p, x, gate1, w_ap, w_mp, w_out)


def _ffn_kernel(x_ref, shift_ref, scale_ref, gate_ref, g2_ref, wg_ref, wu_ref, wo_ref, gf_ref, o_ref,
                h_scr, acc_scr, *, nf):
    f = pl.program_id(2)

    @pl.when(f == 0)
    def _():
        xf = x_ref[0]
        ms = jnp.mean(xf * xf, axis=-1, keepdims=True)
        y = xf * lax.rsqrt(ms + NORM_EPS) * g2_ref[...]
        h_scr[...] = (y * (1.0 + scale_ref[0]) + shift_ref[0]).astype(BF16)
        acc_scr[...] = jnp.zeros_like(acc_scr)

    h = h_scr[...]
    gt = jnp.dot(h, wg_ref[...], preferred_element_type=F32)
    up = jnp.dot(h, wu_ref[...], preferred_element_type=F32)
    act = (gt * _sigmoid(gt) * up).astype(BF16)
    acc_scr[...] += jnp.dot(act, wo_ref[...], preferred_element_type=F32)

    @pl.when(f == nf - 1)
    def _():
        x2 = x_ref[0] + gate_ref[0] * acc_scr[...]
        ms = jnp.mean(x2 * x2, axis=-1, keepdims=True)
        o_ref[0] = x2 * lax.rsqrt(ms + NORM_EPS) * gf_ref[...]


def _ffn(x1, shift2, scale2, gate2, norm2_g, w_in, w_out, final_g, tm, tf):
    B, S, D = x1.shape
    dff = w_out.shape[0]
    nf = dff // tf
    return pl.pallas_call(
        functools.partial(_ffn_kernel, nf=nf),
        out_shape=jax.ShapeDtypeStruct((B, S, D), F32),
        grid=(B, S // tm, nf),
        in_specs=[pl.BlockSpec((1, tm, D), lambda b, i, f: (b, i, 0)),
                  pl.BlockSpec((1, 1, D), lambda b, i, f: (b, 0, 0)),
                  pl.BlockSpec((1, 1, D), lambda b, i, f: (b, 0, 0)),
                  pl.BlockSpec((1, 1, D), lambda b, i, f: (b, 0, 0)),
                  pl.BlockSpec((1, D), lambda b, i, f: (0, 0)),
                  pl.BlockSpec((D, tf), lambda b, i, f: (0, f)),
                  pl.BlockSpec((D, tf), lambda b, i, f: (0, nf + f)),
                  pl.BlockSpec((tf, D), lambda b, i, f: (f, 0)),
                  pl.BlockSpec((1, D), lambda b, i, f: (0, 0))],
        out_specs=pl.BlockSpec((1, tm, D), lambda b, i, f: (b, i, 0)),
        scratch_shapes=[pltpu.VMEM((tm, D), BF16), pltpu.VMEM((tm, D), F32)],
        compiler_params=_params(("parallel", "parallel", "arbitrary")),
        name="ffn_final_norm",
    )(x1, shift2, scale2, gate2, norm2_g.reshape(1, D), w_in, w_in, w_out, final_g.reshape(1, D))


def _rope_tables(S):
    quarter = HEAD_DIM // 4
    pos = jnp.arange(S)
    rows = (pos // GRID_W).astype(F32)
    cols = (pos % GRID_W).astype(F32)
    inv_freq = ROPE_BASE ** (-jnp.arange(quarter, dtype=F32) / quarter)
    ar = rows[:, None] * inv_freq[None, :]
    ac = cols[:, None] * inv_freq[None, :]
    cos = jnp.concatenate([jnp.cos(ar), jnp.cos(ac), jnp.cos(ar), jnp.cos(ac)], axis=1)
    sin = jnp.concatenate([-jnp.sin(ar), -jnp.sin(ac), jnp.sin(ar), jnp.sin(ac)], axis=1)
    return cos, sin


def _rope_perm(n_heads):
    q = HEAD_DIM // 4
    one = np.concatenate([np.arange(0, q), np.arange(2 * q, 3 * q), np.arange(q, 2 * q), np.arange(3 * q, 4 * q)])
    return np.concatenate([h * HEAD_DIM + one for h in range(n_heads)])


def kernel(x, c, ctx, c_ctx, w_ada, b_ada, norm1_g, w_in, b_gates, attn_sink, mlstm_norm_g, w_attn_proj,
           w_mlstm_proj, w_out, norm2_g, w_ffn_in, w_ffn_out, final_norm_g):
    B, S, D = x.shape
    C = ctx.shape[1]
    assert w_ada.shape[0] == 1, "single-layer configuration"
    assert S % 512 == 0 and C % CHUNK == 0 and S % GRID_W == 0
    H = MLSTM_HEADS
    tn = 1024

    rows = -(-(B + 1) // 8) * 8
    cvecs = jnp.concatenate([c, c_ctx[None], jnp.zeros((rows - B - 1, D), F32)], axis=0)
    mod = _adaln(cvecs, w_ada[0], b_ada[0])
    shift1, scale1, gate1, shift2, scale2, gate2 = [mod[:B, k * D:(k + 1) * D].reshape(B, 1, D) for k in range(N_MOD)]
    shift_c = mod[B:B + 1, 0:D].reshape(1, 1, D)
    scale_c = mod[B:B + 1, D:2 * D].reshape(1, 1, D)

    wi = w_in[0]
    o = 0
    parts = {}
    for name, width in (("a_k", ATTN_KV_W), ("a_v", ATTN_KV_W), ("m_k", MLSTM_QK_W), ("m_v", MLSTM_V_W),
                        ("m_g", N_GATE), ("a_q", ATTN_Q_W), ("m_q", MLSTM_QK_W), ("m_o", MLSTM_V_W),
                        ("g_att", D), ("g_mem", D)):
        parts[name] = wi[:, o:o + width]
        o += width
    a_q = parts["a_q"][:, _rope_perm(ATTN_HEADS)]
    a_k = parts["a_k"][:, _rope_perm(ATTN_KV_HEADS)]
    w_nat = jnp.concatenate([a_q, a_k, parts["a_v"], parts["m_k"], parts["g_att"], parts["g_mem"]],
                            axis=1).astype(BF16)
    w_t = jnp.concatenate([parts["m_v"], parts["m_o"], parts["m_q"]], axis=1).T.astype(BF16)
    w_g = parts["m_g"].T.astype(BF16)
    kinds_nat = (["rope scale"] * (ATTN_Q_W // LANE) + ["rope"] * (ATTN_KV_W // LANE)
                 + [""] * ((ATTN_KV_W + MLSTM_QK_W + 2 * D) // LANE))
    kinds_t = [""] * (2 * MLSTM_V_W // LANE) + ["scale"] * (MLSTM_QK_W // LANE)
    off = ATTN_Q_W + 2 * ATTN_KV_W
    assert off % MLSTM_QK_W == 0 and (off + MLSTM_QK_W) % D == 0 and (2 * MLSTM_V_W) % MLSTM_QK_W == 0
    q_blk, k_blk, v_blk = 0, ATTN_Q_W // ATTN_KV_W, ATTN_Q_W // ATTN_KV_W + 1
    mk_blk = off // MLSTM_QK_W
    ga_blk = (off + MLSTM_QK_W) // D
    gm_blk = ga_blk + 1
    vt_blk, mot_blk, qt_blk = 0, 1, 2 * MLSTM_V_W // MLSTM_QK_W
    w_nat_c = jnp.concatenate([parts["m_k"], a_k, parts["a_v"]], axis=1).astype(BF16)
    w_t_c = parts["m_v"].T.astype(BF16)
    kx_blk, vx_blk = MLSTM_QK_W // ATTN_KV_W, MLSTM_QK_W // ATTN_KV_W + 1

    p_lat, pt_lat, gt_lat = _inproj(x, shift1, scale1, norm1_g[0], w_nat, w_t, w_g, kinds_nat, kinds_t,
                                    _rope_tables(S), tm=512, tn=tn)
    p_ctx, pt_ctx, gt_ctx = _inproj(ctx, shift_c, scale_c, norm1_g[0], w_nat_c, w_t_c, w_g,
                                    [""] * (w_nat_c.shape[1] // LANE), [""] * (MLSTM_V_W // LANE), None,
                                    tm=min(C, 256), tn=tn)

    att = _attention(p_lat, p_ctx, attn_sink[0], q_blk, k_blk, v_blk, kx_blk, vx_blk)

    rows_lat, cols_lat = _gate_prep(gt_lat, b_gates[0])
    rows_ctx, cols_ctx = _gate_prep(gt_ctx, b_gates[0])
    c_zero = jnp.zeros((B, H, C_ROWS, MLSTM_QK_DIM), F32)
    m_zero = jnp.zeros((B, H, LANE), F32)
    _, _, cf_ctx, mf_ctx = _state_scan(p_ctx, 0, pt_ctx, 0, rows_ctx, cols_ctx, c_zero, m_zero, reverse=False)
    _, _, cr_ctx, mr_ctx = _state_scan(p_ctx, 0, pt_ctx, 0, rows_ctx, cols_ctx, c_zero, m_zero, reverse=True)
    c_rev, m_rev, _, _ = _state_scan(p_lat, mk_blk, pt_lat, vt_blk, rows_lat, cols_lat, cr_ctx, mr_ctx,
                                     reverse=True)
    mem_t = _mlstm_out(p_lat, mk_blk, pt_lat, qt_blk, vt_blk, mot_blk, rows_lat, cols_lat, c_rev, m_rev,
                       cf_ctx, mf_ctx, mlstm_norm_g[0])

    x1 = _merge(att, mem_t, p_lat, ga_blk, gm_blk, x, gate1, w_attn_proj[0].astype(BF16),
                w_mlstm_proj[0].astype(BF16), w_out[0].astype(BF16), tm=256)
    return _ffn(x1, shift2, scale2, gate2, norm2_g[0], w_ffn_in[0].astype(BF16), w_ffn_out[0].astype(BF16),
                final_norm_g, tm=512, tf=512)
```

```python
import functools

import numpy as np
import jax
import jax.numpy as jnp
from jax import lax
from jax.experimental import pallas as pl
from jax.experimental.pallas import tpu as pltpu

F32 = jnp.float32
BF16 = jnp.bfloat16

GRID_W = 64
ATTN_HEADS = 16
ATTN_KV_HEADS = 4
ATTN_GROUP = ATTN_HEADS // ATTN_KV_HEADS
HEAD_DIM = 128
WINDOW = 128
ROPE_BASE = 10000.0
MLSTM_HEADS = 8
MLSTM_QK_DIM = 128
MLSTM_V_DIM = 256
CHUNK = 128
N_DIRS = 2
N_GATE = N_DIRS * 2 * MLSTM_HEADS
NORM_EPS = 1e-6
N_MOD = 6
QK_SCALE = HEAD_DIM ** -0.5

LANE = 128
BF16_SUBLANES = 16
V7X_VMEM_BYTES = 64 * 1024 * 1024
VMEM_LIMIT = V7X_VMEM_BYTES - 8 * 1024 * 1024

ATTN_Q_W = ATTN_HEADS * HEAD_DIM
ATTN_KV_W = ATTN_KV_HEADS * HEAD_DIM
MLSTM_QK_W = MLSTM_HEADS * MLSTM_QK_DIM
MLSTM_V_W = MLSTM_HEADS * MLSTM_V_DIM
C_ROWS = MLSTM_V_DIM + BF16_SUBLANES
NORM_ROWS = 256

NEG = -1e30


def _params(sem):
    return pltpu.CompilerParams(dimension_semantics=sem, vmem_limit_bytes=VMEM_LIMIT)


def _sigmoid(x):
    return 1.0 / (1.0 + jnp.exp(-x))


def _adaln_kernel(c_ref, w_ref, b_ref, o_ref):
    cc = c_ref[...]
    s = (cc * _sigmoid(cc)).astype(BF16)
    o_ref[...] = jnp.dot(s, w_ref[...].astype(BF16), preferred_element_type=F32) + b_ref[...]


def _adaln(cvecs, w, b):
    R, D = cvecs.shape
    N = w.shape[1]
    tn = 1024
    return pl.pallas_call(
        _adaln_kernel,
        out_shape=jax.ShapeDtypeStruct((R, N), F32),
        grid=(N // tn,),
        in_specs=[pl.BlockSpec((R, D), lambda j: (0, 0)),
                  pl.BlockSpec((D, tn), lambda j: (0, j)),
                  pl.BlockSpec((1, tn), lambda j: (0, j))],
        out_specs=pl.BlockSpec((R, tn), lambda j: (0, j)),
        compiler_params=_params(("arbitrary",)),
        name="adaln",
    )(cvecs, w, b.reshape(1, N))


def _tile_groups(kinds, per):
    tiles = [tuple(kinds[t * per:(t + 1) * per]) for t in range(len(kinds) // per)]
    groups = []
    for t, tk in enumerate(tiles):
        if groups and groups[-1][2] == tk:
            groups[-1] = (groups[-1][0], t + 1, tk)
        else:
            groups.append((t, t + 1, tk))
    return tuple(groups)


def _inproj_kernel(*refs, n_groups, t_groups, nn, rope):
    if rope:
        (x_ref, shift_ref, scale_ref, g_ref, wn_ref, wt_ref, wg_ref, cos_ref, sin_ref,
         p_ref, pt_ref, gt_ref, h_scr) = refs
    else:
        x_ref, shift_ref, scale_ref, g_ref, wn_ref, wt_ref, wg_ref, p_ref, pt_ref, gt_ref, h_scr = refs
    j = pl.program_id(2)
    nt_dims = (((1,), (1,)), ((), ()))

    @pl.when(j == 0)
    def _():
        tm = h_scr.shape[0]
        rc = min(tm, NORM_ROWS)
        for r0 in range(0, tm, rc):
            xf = x_ref[0, r0:r0 + rc, :]
            ms = jnp.mean(xf * xf, axis=-1, keepdims=True)
            y = xf * lax.rsqrt(ms + NORM_EPS) * g_ref[...]
            h_scr[r0:r0 + rc, :] = (y * (1.0 + scale_ref[0]) + shift_ref[0]).astype(BF16)
        gt_ref[0] = lax.dot_general(wg_ref[...], h_scr[...], nt_dims, preferred_element_type=F32)

    for lo, hi, kinds in n_groups:
        @pl.when((j >= lo) & (j < hi))
        def _(kinds=kinds):
            acc = jnp.dot(h_scr[...], wn_ref[...], preferred_element_type=F32)
            for u, kind in enumerate(kinds):
                a = acc[:, u * LANE:(u + 1) * LANE]
                if "rope" in kind:
                    a = a * cos_ref[...] + pltpu.roll(a, HEAD_DIM // 2, 1) * sin_ref[...]
                if "scale" in kind:
                    a = a * QK_SCALE
                p_ref[0, :, u * LANE:(u + 1) * LANE] = a.astype(BF16)

    for lo, hi, kinds in t_groups:
        @pl.when((j >= nn + lo) & (j < nn + hi))
        def _(kinds=kinds):
            acc = lax.dot_general(wt_ref[...], h_scr[...], nt_dims, preferred_element_type=F32)
            for u, kind in enumerate(kinds):
                a = acc[u * LANE:(u + 1) * LANE]
                if "scale" in kind:
                    a = a * QK_SCALE
                pt_ref[0, u * LANE:(u + 1) * LANE, :] = a.astype(BF16)


def _inproj(x, shift, scale, gain, w_nat, w_t, w_g, kinds_nat, kinds_t, rope_tabs, tm, tn):
    B, T, D = x.shape
    n_nat = w_nat.shape[1]
    n_t = w_t.shape[0]
    nn, ntt = n_nat // tn, n_t // tn
    per = tn // LANE
    bm = shift.shape[0]
    mod_map = (lambda b, i, j: (b, 0, 0)) if bm == B else (lambda b, i, j: (0, 0, 0))
    rope = rope_tabs is not None
    in_specs = [pl.BlockSpec((1, tm, D), lambda b, i, j: (b, i, 0)),
                pl.BlockSpec((1, 1, D), mod_map),
                pl.BlockSpec((1, 1, D), mod_map),
                pl.BlockSpec((1, D), lambda b, i, j: (0, 0)),
                pl.BlockSpec((D, tn), lambda b, i, j: (0, jnp.minimum(j, nn - 1))),
                pl.BlockSpec((tn, D), lambda b, i, j: (jnp.maximum(j - nn, 0), 0)),
                pl.BlockSpec(w_g.shape, lambda b, i, j: (0, 0))]
    args = [x, shift, scale, gain.reshape(1, D), w_nat, w_t, w_g]
    if rope:
        in_specs += [pl.BlockSpec((tm, LANE), lambda b, i, j: (i, 0))] * 2
        args += list(rope_tabs)
    return pl.pallas_call(
        functools.partial(_inproj_kernel, n_groups=_tile_groups(kinds_nat, per),
                          t_groups=_tile_groups(kinds_t, per), nn=nn, rope=rope),
        out_shape=(jax.ShapeDtypeStruct((B, T, n_nat), BF16),
                   jax.ShapeDtypeStruct((B, n_t, T), BF16),
                   jax.ShapeDtypeStruct((B, w_g.shape[0], T), F32)),
        grid=(B, T // tm, nn + ntt),
        in_specs=in_specs,
        out_specs=(pl.BlockSpec((1, tm, tn), lambda b, i, j: (b, i, jnp.minimum(j, nn - 1))),
                   pl.BlockSpec((1, tn, tm), lambda b, i, j: (b, jnp.maximum(j - nn, 0), i)),
                   pl.BlockSpec((1, w_g.shape[0], tm), lambda b, i, j: (b, 0, i))),
        scratch_shapes=[pltpu.VMEM((tm, D), BF16)],
        compiler_params=_params(("parallel", "parallel", "arbitrary")),
        name="inproj_rope" if rope else "inproj_ctx",
    )(*args)


def _scan_lanes(x, op, reverse, fill):
    lane = lax.broadcasted_iota(jnp.int32, x.shape, 1)
    k = 1
    while k < CHUNK:
        if reverse:
            sh = jnp.where(lane < CHUNK - k, pltpu.roll(x, CHUNK - k, 1), fill)
        else:
            sh = jnp.where(lane >= k, pltpu.roll(x, k, 1), fill)
        x = op(x, sh)
        k *= 2
    return x


def _log_sigmoid(z):
    return jnp.minimum(z, 0.0) - jnp.log(1.0 + jnp.exp(-jnp.abs(z)))


def _lane_value(x, lane_idx):
    lane = lax.broadcasted_iota(jnp.int32, x.shape, 1)
    return jnp.broadcast_to(jnp.sum(jnp.where(lane == lane_idx, x, 0.0), axis=1, keepdims=True), x.shape)


def _gate_prep_kernel(gt_ref, bias_ref, row_ref, col_ref, *, nchunk):
    H = MLSTM_HEADS
    for c in range(nchunk):
        sl = slice(c * CHUNK, (c + 1) * CHUNK)
        z = gt_ref[0, :, sl] + bias_ref[...]
        li_f, lf_f = z[0:H], _log_sigmoid(z[H:2 * H])
        li_r, lf_r = z[2 * H:3 * H], _log_sigmoid(z[3 * H:4 * H])
        b_f = _scan_lanes(lf_f, jnp.add, False, 0.0)
        b_r = _scan_lanes(lf_r, jnp.add, True, 0.0)
        u_f = li_f - b_f
        u_r = li_r - b_r
        r_f = _scan_lanes(u_f, jnp.maximum, False, -jnp.inf)
        r_r = _scan_lanes(u_r, jnp.maximum, True, -jnp.inf)
        ends = [_lane_value(r_f, CHUNK - 1), _lane_value(r_r, 0), _lane_value(b_f, CHUNK - 1), _lane_value(b_r, 0)]
        for k, v in enumerate([r_f, r_r, b_f, b_r] + ends):
            row_ref[0, k * H:(k + 1) * H, sl] = v
        stack = jnp.concatenate([u_f, u_r, jnp.zeros((LANE - 2 * H, CHUNK), F32)], axis=0)
        col_ref[0, sl, :] = stack.T


def _gate_prep(g_t, bias):
    B, G, T = g_t.shape
    tg = min(T, 8 * CHUNK)
    H = MLSTM_HEADS
    return pl.pallas_call(
        functools.partial(_gate_prep_kernel, nchunk=tg // CHUNK),
        out_shape=(jax.ShapeDtypeStruct((B, 8 * H, T), F32),
                   jax.ShapeDtypeStruct((B, T, LANE), F32)),
        grid=(B, T // tg),
        in_specs=[pl.BlockSpec((1, G, tg), lambda b, i: (b, 0, i)),
                  pl.BlockSpec((G, CHUNK), lambda b, i: (0, 0))],
        out_specs=(pl.BlockSpec((1, 8 * H, tg), lambda b, i: (b, 0, i)),
                   pl.BlockSpec((1, tg, LANE), lambda b, i: (b, i, 0))),
        compiler_params=_params(("parallel", "parallel")),
        name="gate_prep",
    )(g_t, jnp.broadcast_to(bias.reshape(G, 1), (G, CHUNK)))


def _attn_kernel(sink_ref, q_ref, kp_ref, kc_ref, kn_ref, vp_ref, vc_ref, vn_ref, kx_ref, vx_ref, band_ref,
                 o_ref, *, nb):
    j = pl.program_id(1)
    rows = ATTN_GROUP * WINDOW
    prev_bias = jnp.where(j > 0, 0.0, NEG)
    next_bias = jnp.where(j < nb - 1, 0.0, NEG)
    col = lax.broadcasted_iota(jnp.int32, (1, 3 * WINDOW), 1)
    bias = band_ref[...] + jnp.where(col < WINDOW, prev_bias, jnp.where(col >= 2 * WINDOW, next_bias, 0.0))
    row = lax.broadcasted_iota(jnp.int32, (rows, 1), 0)
    for g in range(ATTN_KV_HEADS):
        hs = slice(g * HEAD_DIM, (g + 1) * HEAD_DIM)
        qs = jnp.concatenate([q_ref[0, :, (g * ATTN_GROUP + h) * HEAD_DIM:(g * ATTN_GROUP + h + 1) * HEAD_DIM]
                              for h in range(ATTN_GROUP)], axis=0)
        k = jnp.concatenate([kp_ref[0, :, hs], kc_ref[0, :, hs], kn_ref[0, :, hs], kx_ref[0, :, hs]], axis=0)
        v = jnp.concatenate([vp_ref[0, :, hs], vc_ref[0, :, hs], vn_ref[0, :, hs], vx_ref[0, :, hs]], axis=0)
        s = lax.dot_general(qs, k, (((1,), (1,)), ((), ())), preferred_element_type=F32)
        s_loc = s[:, :3 * WINDOW] + bias
        s_ctx = s[:, 3 * WINDOW:]
        sink = jnp.full((rows, 1), sink_ref[g * ATTN_GROUP], F32)
        for h in range(1, ATTN_GROUP):
            sink = jnp.where(row >= h * WINDOW, sink_ref[g * ATTN_GROUP + h], sink)
        m = jnp.maximum(jnp.maximum(jnp.max(s_loc, axis=1, keepdims=True),
                                    jnp.max(s_ctx, axis=1, keepdims=True)), sink)
        p_loc = jnp.exp(s_loc - m)
        p_ctx = jnp.exp(s_ctx - m)
        den = jnp.sum(p_loc, axis=1, keepdims=True) + jnp.sum(p_ctx, axis=1, keepdims=True) + jnp.exp(sink - m)
        p = jnp.concatenate([p_loc, p_ctx], axis=1).astype(BF16)
        o = jnp.dot(p, v, preferred_element_type=F32) * (1.0 / den)
        for h in range(ATTN_GROUP):
            c0 = (g * ATTN_GROUP + h) * HEAD_DIM
            o_ref[0, :, c0:c0 + HEAD_DIM] = o[h * WINDOW:(h + 1) * WINDOW].astype(BF16)


def _attention(p_lat, p_ctx, sink, q_blk, k_blk, v_blk, kx_blk, vx_blk):
    B, S, _ = p_lat.shape
    C = p_ctx.shape[1]
    nb = S // WINDOW
    rows = ATTN_GROUP * WINDOW
    t = np.arange(rows)[:, None] % WINDOW
    d = np.arange(3 * WINDOW)[None, :] - t
    band = jnp.asarray(np.where((d >= 0) & (d <= 2 * WINDOW), 0.0, NEG), F32)

    def kspec(off, blk):
        return pl.BlockSpec((1, WINDOW, ATTN_KV_W), lambda b, j: (b, jnp.clip(j + off, 0, nb - 1), blk))

    return pl.pallas_call(
        functools.partial(_attn_kernel, nb=nb),
        out_shape=jax.ShapeDtypeStruct((B, S, ATTN_Q_W), BF16),
        grid=(B, nb),
        in_specs=[pl.BlockSpec(memory_space=pltpu.SMEM),
                  pl.BlockSpec((1, WINDOW, ATTN_Q_W), lambda b, j: (b, j, q_blk)),
                  kspec(-1, k_blk), kspec(0, k_blk), kspec(1, k_blk),
                  kspec(-1, v_blk), kspec(0, v_blk), kspec(1, v_blk),
                  pl.BlockSpec((1, C, ATTN_KV_W), lambda b, j: (b, 0, kx_blk)),
                  pl.BlockSpec((1, C, ATTN_KV_W), lambda b, j: (b, 0, vx_blk)),
                  pl.BlockSpec((rows, 3 * WINDOW), lambda b, j: (0, 0))],
        out_specs=pl.BlockSpec((1, WINDOW, ATTN_Q_W), lambda b, j: (b, j, 0)),
        compiler_params=_params(("parallel", "arbitrary")),
        name="window_attn",
    )(sink, p_lat, p_lat, p_lat, p_lat, p_lat, p_lat, p_lat, p_ctx, p_ctx, band)


def _v_aug_t(vt):
    sub = lax.broadcasted_iota(jnp.int32, (BF16_SUBLANES, vt.shape[1]), 0)
    return jnp.concatenate([vt, jnp.where(sub == 0, 1.0, 0.0).astype(BF16)], axis=0)


def _state_update(ct_old, m_old, k, vaug_t, u_bc, r_end, b_end):
    m_end = jnp.maximum(m_old, r_end)
    ks = (k.astype(F32) * jnp.exp(u_bc - m_end)).astype(BF16)
    ct_new = jnp.exp(m_old - m_end) * ct_old + jnp.dot(vaug_t, ks, preferred_element_type=F32)
    return ct_new, b_end + m_end


def _row(row_ref, k, h, ts):
    i = k * MLSTM_HEADS + h
    return row_ref[0, i:i + 1, ts]


def _scan_kernel(k_ref, vt_ref, row_ref, col_ref, c0_ref, m0_ref, cs_ref, ms_ref, cf_ref, mf_ref,
                 c_scr, m_scr, *, reverse, nsteps, cps):
    H = MLSTM_HEADS
    n = pl.program_id(1)
    d = 1 if reverse else 0

    @pl.when(n == 0)
    def _():
        c_scr[...] = c0_ref[0]
        m_scr[...] = m0_ref[0]

    for c in (range(cps - 1, -1, -1) if reverse else range(cps)):
        ts = slice(c * CHUNK, (c + 1) * CHUNK)
        cols = col_ref[0, ts, :]
        for h in range(H):
            ct_old = c_scr[h]
            m_old = m_scr[h:h + 1, :]
            cs_ref[0, c, h] = ct_old.astype(BF16)
            ms_ref[0, c, h:h + 1, :] = m_old
            k = k_ref[0, ts, h * MLSTM_QK_DIM:(h + 1) * MLSTM_QK_DIM]
            vaug_t = _v_aug_t(vt_ref[0, h * MLSTM_V_DIM:(h + 1) * MLSTM_V_DIM, ts])
            u_bc = jnp.broadcast_to(cols[:, d * H + h:d * H + h + 1], (CHUNK, LANE))
            ct_new, m_new = _state_update(ct_old, m_old, k, vaug_t, u_bc, _row(row_ref, 4 + d, h, ts),
                                          _row(row_ref, 6 + d, h, ts))
            c_scr[h] = ct_new
            m_scr[h:h + 1, :] = m_new

    @pl.when(n == nsteps - 1)
    def _():
        cf_ref[0] = c_scr[...]
        mf_ref[0] = m_scr[...]


def _state_scan(p, k_blk, p_t, vt_blk, rows, cols, c0, m0, reverse, cps):
    B, T, _ = p.shape
    nc = T // CHUNK
    cps = min(cps, nc)
    ns = nc // cps
    assert ns * cps == nc
    H = MLSTM_HEADS
    tb = cps * CHUNK
    cidx = (lambda n: ns - 1 - n) if reverse else (lambda n: n)
    st = (H, C_ROWS, MLSTM_QK_DIM)
    return pl.pallas_call(
        functools.partial(_scan_kernel, reverse=reverse, nsteps=ns, cps=cps),
        out_shape=(jax.ShapeDtypeStruct((B, nc) + st, BF16),
                   jax.ShapeDtypeStruct((B, nc, H, LANE), F32),
                   jax.ShapeDtypeStruct((B,) + st, F32),
                   jax.ShapeDtypeStruct((B, H, LANE), F32)),
        grid=(B, ns),
        in_specs=[pl.BlockSpec((1, tb, MLSTM_QK_W), lambda b, n: (b, cidx(n), k_blk)),
                  pl.BlockSpec((1, MLSTM_V_W, tb), lambda b, n: (b, vt_blk, cidx(n))),
                  pl.BlockSpec((1, 8 * H, tb), lambda b, n: (b, 0, cidx(n))),
                  pl.BlockSpec((1, tb, LANE), lambda b, n: (b, cidx(n), 0)),
                  pl.BlockSpec((1,) + st, lambda b, n: (b, 0, 0, 0)),
                  pl.BlockSpec((1, H, LANE), lambda b, n: (b, 0, 0))],
        out_specs=(pl.BlockSpec((1, cps) + st, lambda b, n: (b, cidx(n), 0, 0, 0)),
                   pl.BlockSpec((1, cps, H, LANE), lambda b, n: (b, cidx(n), 0, 0)),
                   pl.BlockSpec((1,) + st, lambda b, n: (b, 0, 0, 0)),
                   pl.BlockSpec((1, H, LANE), lambda b, n: (b, 0, 0))),
        scratch_shapes=[pltpu.VMEM(st, F32), pltpu.VMEM((H, LANE), F32)],
        compiler_params=_params(("parallel", "arbitrary")),
        name="mlstm_scan_rev" if reverse else "mlstm_scan_fwd",
    )(p, p_t, rows, cols, c0, m0)


def _mlstm_out_kernel(k_ref, qt_ref, vt_ref, mot_ref, row_ref, col_ref, cr_ref, mr_ref, c0_ref, m0_ref, g_ref,
                      o_ref, c_scr, m_scr, *, cps):
    H = MLSTM_HEADS
    L = CHUNK
    n = pl.program_id(1)

    @pl.when(n == 0)
    def _():
        c_scr[...] = c0_ref[0]
        m_scr[...] = m0_ref[0]

    si = lax.broadcasted_iota(jnp.int32, (L, L), 0)
    ti = lax.broadcasted_iota(jnp.int32, (L, L), 1)
    for c in range(cps):
        ts = slice(c * L, (c + 1) * L)
        cols = col_ref[0, ts, :]
        for h in range(H):
            qt = qt_ref[0, h * MLSTM_QK_DIM:(h + 1) * MLSTM_QK_DIM, ts]
            k = k_ref[0, ts, h * MLSTM_QK_DIM:(h + 1) * MLSTM_QK_DIM]
            vs = slice(h * MLSTM_V_DIM, (h + 1) * MLSTM_V_DIM)
            vaug_t = _v_aug_t(vt_ref[0, vs, ts])
            pt = jnp.dot(k, qt, preferred_element_type=F32)
            qt_f = qt.astype(F32)
            ct_f = c_scr[h]
            hs_t = None
            for d in range(N_DIRS):
                u_bc = jnp.broadcast_to(cols[:, d * H + h:d * H + h + 1], (L, LANE))
                m0 = m_scr[h:h + 1, :] if d == 0 else mr_ref[0, c, h:h + 1, :]
                ct = ct_f.astype(BF16) if d == 0 else cr_ref[0, c, h]
                mx = jnp.maximum(_row(row_ref, d, h, ts), m0)
                valid = (si <= ti) if d == 0 else (si >= ti)
                s_t = (pt * jnp.exp(jnp.where(valid, u_bc - mx, -jnp.inf))).astype(BF16)
                q_in = (qt_f * jnp.exp(m0 - mx)).astype(BF16)
                tot = jnp.dot(jnp.concatenate([vaug_t, ct], axis=1), jnp.concatenate([s_t, q_in], axis=0),
                              preferred_element_type=F32)
                den = jnp.maximum(jnp.abs(tot[MLSTM_V_DIM:MLSTM_V_DIM + 1]),
                                  jnp.exp(-_row(row_ref, 2 + d, h, ts) - mx))
                part = tot[:MLSTM_V_DIM] * (1.0 / den)
                hs_t = part if hs_t is None else hs_t + part
                if d == 0:
                    u_bc_f = u_bc
            hn = hs_t * lax.rsqrt(jnp.mean(hs_t * hs_t, axis=0, keepdims=True) + NORM_EPS) * g_ref[vs, :]
            o_ref[0, vs, ts] = (_sigmoid(mot_ref[0, vs, ts].astype(F32)) * hn).astype(BF16)
            ct_new, m_new = _state_update(ct_f, m_scr[h:h + 1, :], k, vaug_t, u_bc_f, _row(row_ref, 4, h, ts),
                                          _row(row_ref, 6, h, ts))
            c_scr[h] = ct_new
            m_scr[h:h + 1, :] = m_new


def _mlstm_out(p, k_blk, p_t, qt_blk, vt_blk, mot_blk, rows, cols, c_rev, m_rev, c0, m0, gain, cps):
    B, S, _ = p.shape
    nc = S // CHUNK
    ns = nc // cps
    assert ns * cps == nc
    H = MLSTM_HEADS
    tb = cps * CHUNK
    st = (H, C_ROWS, MLSTM_QK_DIM)
    gain_bc = jnp.broadcast_to(gain.reshape(MLSTM_V_W, 1), (MLSTM_V_W, LANE))
    return pl.pallas_call(
        functools.partial(_mlstm_out_kernel, cps=cps),
        out_shape=jax.ShapeDtypeStruct((B, MLSTM_V_W, S), BF16),
        grid=(B, ns),
        in_specs=[pl.BlockSpec((1, tb, MLSTM_QK_W), lambda b, n: (b, n, k_blk)),
                  pl.BlockSpec((1, MLSTM_QK_W, tb), lambda b, n: (b, qt_blk, n)),
                  pl.BlockSpec((1, MLSTM_V_W, tb), lambda b, n: (b, vt_blk, n)),
                  pl.BlockSpec((1, MLSTM_V_W, tb), lambda b, n: (b, mot_blk, n)),
                  pl.BlockSpec((1, 8 * H, tb), lambda b, n: (b, 0, n)),
                  pl.BlockSpec((1, tb, LANE), lambda b, n: (b, n, 0)),
                  pl.BlockSpec((1, cps) + st, lambda b, n: (b, n, 0, 0, 0)),
                  pl.BlockSpec((1, cps, H, LANE), lambda b, n: (b, n, 0, 0)),
                  pl.BlockSpec((1,) + st, lambda b, n: (b, 0, 0, 0)),
                  pl.BlockSpec((1, H, LANE), lambda b, n: (b, 0, 0)),
                  pl.BlockSpec((MLSTM_V_W, LANE), lambda b, n: (0, 0))],
        out_specs=pl.BlockSpec((1, MLSTM_V_W, tb), lambda b, n: (b, 0, n)),
        scratch_shapes=[pltpu.VMEM(st, F32), pltpu.VMEM((H, LANE), F32)],
        compiler_params=_params(("parallel", "arbitrary")),
        name="mlstm_out",
    )(p, p_t, p_t, p_t, rows, cols, c_rev, m_rev, c0, m0, gain_bc)


def _merge_kernel(att_ref, memt_ref, ga_ref, gm_ref, x_ref, gate_ref, wa_ref, wm_ref, wo_ref, o_ref):
    a = jnp.dot(att_ref[0], wa_ref[...], preferred_element_type=F32)
    m = lax.dot_general(memt_ref[0], wm_ref[...], (((0,), (0,)), ((), ())), preferred_element_type=F32)
    y = _sigmoid(ga_ref[0].astype(F32)) * a + _sigmoid(gm_ref[0].astype(F32)) * m
    z = jnp.dot(y.astype(BF16), wo_ref[...], preferred_element_type=F32)
    o_ref[0] = x_ref[0] + gate_ref[0] * z


def _merge(att, mem_t, p, ga_blk, gm_blk, x, gate1, w_ap, w_mp, w_out, tm):
    B, S, D = x.shape
    resident = functools.partial(pl.BlockSpec, pipeline_mode=pl.Buffered(1))
    return pl.pallas_call(
        _merge_kernel,
        out_shape=jax.ShapeDtypeStruct((B, S, D), F32),
        grid=(B, S // tm),
        in_specs=[pl.BlockSpec((1, tm, ATTN_Q_W), lambda b, i: (b, i, 0)),
                  pl.BlockSpec((1, MLSTM_V_W, tm), lambda b, i: (b, 0, i)),
                  pl.BlockSpec((1, tm, D), lambda b, i: (b, i, ga_blk)),
                  pl.BlockSpec((1, tm, D), lambda b, i: (b, i, gm_blk)),
                  pl.BlockSpec((1, tm, D), lambda b, i: (b, i, 0)),
                  pl.BlockSpec((1, 1, D), lambda b, i: (b, 0, 0)),
                  resident(w_ap.shape, lambda b, i: (0, 0)),
                  resident(w_mp.shape, lambda b, i: (0, 0)),
                  resident(w_out.shape, lambda b, i: (0, 0))],
        out_specs=pl.BlockSpec((1, tm, D), lambda b, i: (b, i, 0)),
        compiler_params=_params(("parallel", "parallel")),
        name="merge_outproj",
    )(att, mem_t, p, p, x, gate1, w_ap, w_mp, w_out)


def _ffn_kernel(x_ref, shift_ref, scale_ref, gate_ref, g2_ref, wg_ref, wu_ref, wo_ref, gf_ref, o_ref,
                h_scr, acc_scr, *, nf):
    f = pl.program_id(2)

    @pl.when(f == 0)
    def _():
        xf = x_ref[0]
        ms = jnp.mean(xf * xf, axis=-1, keepdims=True)
        y = xf * lax.rsqrt(ms + NORM_EPS) * g2_ref[...]
        h_scr[...] = (y * (1.0 + scale_ref[0]) + shift_ref[0]).astype(BF16)
        acc_scr[...] = jnp.zeros_like(acc_scr)

    h = h_scr[...]
    gt = jnp.dot(h, wg_ref[...], preferred_element_type=F32)
    up = jnp.dot(h, wu_ref[...], preferred_element_type=F32)
    act = (gt * _sigmoid(gt) * up).astype(BF16)
    acc_scr[...] += jnp.dot(act, wo_ref[...], preferred_element_type=F32)

    @pl.when(f == nf - 1)
    def _():
        x2 = x_ref[0] + gate_ref[0] * acc_scr[...]
        ms = jnp.mean(x2 * x2, axis=-1, keepdims=True)
        o_ref[0] = x2 * lax.rsqrt(ms + NORM_EPS) * gf_ref[...]


def _ffn(x1, shift2, scale2, gate2, norm2_g, w_in, w_out, final_g, tm, tf):
    B, S, D = x1.shape
    dff = w_out.shape[0]
    nf = dff // tf
    return pl.pallas_call(
        functools.partial(_ffn_kernel, nf=nf),
        out_shape=jax.ShapeDtypeStruct((B, S, D), F32),
        grid=(B, S // tm, nf),
        in_specs=[pl.BlockSpec((1, tm, D), lambda b, i, f: (b, i, 0)),
                  pl.BlockSpec((1, 1, D), lambda b, i, f: (b, 0, 0)),
                  pl.BlockSpec((1, 1, D), lambda b, i, f: (b, 0, 0)),
                  pl.BlockSpec((1, 1, D), lambda b, i, f: (b, 0, 0)),
                  pl.BlockSpec((1, D), lambda b, i, f: (0, 0)),
                  pl.BlockSpec((D, tf), lambda b, i, f: (0, f)),
                  pl.BlockSpec((D, tf), lambda b, i, f: (0, nf + f)),
                  pl.BlockSpec((tf, D), lambda b, i, f: (f, 0)),
                  pl.BlockSpec((1, D), lambda b, i, f: (0, 0))],
        out_specs=pl.BlockSpec((1, tm, D), lambda b, i, f: (b, i, 0)),
        scratch_shapes=[pltpu.VMEM((tm, D), BF16), pltpu.VMEM((tm, D), F32)],
        compiler_params=_params(("parallel", "parallel", "arbitrary")),
        name="ffn_final_norm",
    )(x1, shift2, scale2, gate2, norm2_g.reshape(1, D), w_in, w_in, w_out, final_g.reshape(1, D))


def _rope_tables(S):
    quarter = HEAD_DIM // 4
    pos = jnp.arange(S)
    rows = (pos // GRID_W).astype(F32)
    cols = (pos % GRID_W).astype(F32)
    inv_freq = ROPE_BASE ** (-jnp.arange(quarter, dtype=F32) / quarter)
    ar = rows[:, None] * inv_freq[None, :]
    ac = cols[:, None] * inv_freq[None, :]
    cos = jnp.concatenate([jnp.cos(ar), jnp.cos(ac), jnp.cos(ar), jnp.cos(ac)], axis=1)
    sin = jnp.concatenate([-jnp.sin(ar), -jnp.sin(ac), jnp.sin(ar), jnp.sin(ac)], axis=1)
    return cos, sin


def _rope_perm(n_heads):
    q = HEAD_DIM // 4
    one = np.concatenate([np.arange(0, q), np.arange(2 * q, 3 * q), np.arange(q, 2 * q), np.arange(3 * q, 4 * q)])
    return np.concatenate([h * HEAD_DIM + one for h in range(n_heads)])


def kernel(x, c, ctx, c_ctx, w_ada, b_ada, norm1_g, w_in, b_gates, attn_sink, mlstm_norm_g, w_attn_proj,
           w_mlstm_proj, w_out, norm2_g, w_ffn_in, w_ffn_out, final_norm_g):
    B, S, D = x.shape
    C = ctx.shape[1]
    assert w_ada.shape[0] == 1, "single-layer configuration"
    assert S % 512 == 0 and C % CHUNK == 0 and S % GRID_W == 0
    H = MLSTM_HEADS
    tn = 1024

    rows = -(-(B + 1) // 8) * 8
    cvecs = jnp.concatenate([c, c_ctx[None], jnp.zeros((rows - B - 1, D), F32)], axis=0)
    mod = _adaln(cvecs, w_ada[0], b_ada[0])
    shift1, scale1, gate1, shift2, scale2, gate2 = [mod[:B, k * D:(k + 1) * D].reshape(B, 1, D) for k in range(N_MOD)]
    shift_c = mod[B:B + 1, 0:D].reshape(1, 1, D)
    scale_c = mod[B:B + 1, D:2 * D].reshape(1, 1, D)

    wi = w_in[0]
    o = 0
    parts = {}
    for name, width in (("a_k", ATTN_KV_W), ("a_v", ATTN_KV_W), ("m_k", MLSTM_QK_W), ("m_v", MLSTM_V_W),
                        ("m_g", N_GATE), ("a_q", ATTN_Q_W), ("m_q", MLSTM_QK_W), ("m_o", MLSTM_V_W),
                        ("g_att", D), ("g_mem", D)):
        parts[name] = wi[:, o:o + width]
        o += width
    a_q = parts["a_q"][:, _rope_perm(ATTN_HEADS)]
    a_k = parts["a_k"][:, _rope_perm(ATTN_KV_HEADS)]
    w_nat = jnp.concatenate([a_q, a_k, parts["a_v"], parts["m_k"], parts["g_att"], parts["g_mem"]],
                            axis=1).astype(BF16)
    w_t = jnp.concatenate([parts["m_v"], parts["m_o"], parts["m_q"]], axis=1).T.astype(BF16)
    w_g = parts["m_g"].T.astype(BF16)
    kinds_nat = (["rope scale"] * (ATTN_Q_W // LANE) + ["rope"] * (ATTN_KV_W // LANE)
                 + [""] * ((ATTN_KV_W + MLSTM_QK_W + 2 * D) // LANE))
    kinds_t = [""] * (2 * MLSTM_V_W // LANE) + ["scale"] * (MLSTM_QK_W // LANE)
    off = ATTN_Q_W + 2 * ATTN_KV_W
    assert off % MLSTM_QK_W == 0 and (off + MLSTM_QK_W) % D == 0 and (2 * MLSTM_V_W) % MLSTM_QK_W == 0
    q_blk, k_blk, v_blk = 0, ATTN_Q_W // ATTN_KV_W, ATTN_Q_W // ATTN_KV_W + 1
    mk_blk = off // MLSTM_QK_W
    ga_blk = (off + MLSTM_QK_W) // D
    gm_blk = ga_blk + 1
    vt_blk, mot_blk, qt_blk = 0, 1, 2 * MLSTM_V_W // MLSTM_QK_W
    w_nat_c = jnp.concatenate([parts["m_k"], a_k, parts["a_v"]], axis=1).astype(BF16)
    w_t_c = parts["m_v"].T.astype(BF16)
    kx_blk, vx_blk = MLSTM_QK_W // ATTN_KV_W, MLSTM_QK_W // ATTN_KV_W + 1

    p_lat, pt_lat, gt_lat = _inproj(x, shift1, scale1, norm1_g[0], w_nat, w_t, w_g, kinds_nat, kinds_t,
                                    _rope_tables(S), tm=1024 if S % 1024 == 0 else 512, tn=tn)
    p_ctx, pt_ctx, gt_ctx = _inproj(ctx, shift_c, scale_c, norm1_g[0], w_nat_c, w_t_c, w_g,
                                    [""] * (w_nat_c.shape[1] // LANE), [""] * (MLSTM_V_W // LANE), None,
                                    tm=min(C, 256), tn=tn)

    att = _attention(p_lat, p_ctx, attn_sink[0], q_blk, k_blk, v_blk, kx_blk, vx_blk)

    rows_lat, cols_lat = _gate_prep(gt_lat, b_gates[0])
    rows_ctx, cols_ctx = _gate_prep(gt_ctx, b_gates[0])
    c_zero = jnp.zeros((B, H, C_ROWS, MLSTM_QK_DIM), F32)
    m_zero = jnp.zeros((B, H, LANE), F32)
    _, _, cf_ctx, mf_ctx = _state_scan(p_ctx, 0, pt_ctx, 0, rows_ctx, cols_ctx, c_zero, m_zero, reverse=False,
                                       cps=4)
    _, _, cr_ctx, mr_ctx = _state_scan(p_ctx, 0, pt_ctx, 0, rows_ctx, cols_ctx, c_zero, m_zero, reverse=True,
                                       cps=4)
    c_rev, m_rev, _, _ = _state_scan(p_lat, mk_blk, pt_lat, vt_blk, rows_lat, cols_lat, cr_ctx, mr_ctx,
                                     reverse=True, cps=4)
    mem_t = _mlstm_out(p_lat, mk_blk, pt_lat, qt_blk, vt_blk, mot_blk, rows_lat, cols_lat, c_rev, m_rev,
                       cf_ctx, mf_ctx, mlstm_norm_g[0], cps=2)

    x1 = _merge(att, mem_t, p_lat, ga_blk, gm_blk, x, gate1, w_attn_proj[0].astype(BF16),
                w_mlstm_proj[0].astype(BF16), w_out[0].astype(BF16), tm=256)
    return _ffn(x1, shift2, scale2, gate2, norm2_g[0], w_ffn_in[0].astype(BF16), w_ffn_out[0].astype(BF16),
                final_norm_g, tm=512, tf=512)
```

```python
import functools

import numpy as np
import jax
import jax.numpy as jnp
from jax import lax
from jax.experimental import pallas as pl
from jax.experimental.pallas import tpu as pltpu

F32 = jnp.float32
BF16 = jnp.bfloat16

GRID_W = 64
ATTN_HEADS = 16
ATTN_KV_HEADS = 4
ATTN_GROUP = ATTN_HEADS // ATTN_KV_HEADS
HEAD_DIM = 128
WINDOW = 128
ROPE_BASE = 10000.0
MLSTM_HEADS = 8
MLSTM_QK_DIM = 128
MLSTM_V_DIM = 256
CHUNK = 128
N_DIRS = 2
N_GATE = N_DIRS * 2 * MLSTM_HEADS
NORM_EPS = 1e-6
N_MOD = 6
QK_SCALE = HEAD_DIM ** -0.5
ROPE_PAIR = HEAD_DIM // 4

LANE = 128
BF16_SUBLANES = 16
V7X_VMEM_BYTES = 64 * 1024 * 1024
VMEM_LIMIT = V7X_VMEM_BYTES - 3 * 1024 * 1024

ATTN_Q_W = ATTN_HEADS * HEAD_DIM
ATTN_KV_W = ATTN_KV_HEADS * HEAD_DIM
MLSTM_QK_W = MLSTM_HEADS * MLSTM_QK_DIM
MLSTM_V_W = MLSTM_HEADS * MLSTM_V_DIM
C_ROWS = MLSTM_V_DIM + BF16_SUBLANES
NORM_ROWS = 256

NEG = -1e30


def _params(sem):
    return pltpu.CompilerParams(dimension_semantics=sem, vmem_limit_bytes=VMEM_LIMIT)


def _sigmoid(x):
    return 1.0 / (1.0 + jnp.exp(-x))


def _adaln_kernel(c_ref, w_ref, b_ref, o_ref):
    cc = c_ref[...]
    s = (cc * _sigmoid(cc)).astype(BF16)
    o_ref[...] = jnp.dot(s, w_ref[...].astype(BF16), preferred_element_type=F32) + b_ref[...]


def _adaln(cvecs, w, b):
    R, D = cvecs.shape
    N = w.shape[1]
    tn = 1024
    return pl.pallas_call(
        _adaln_kernel,
        out_shape=jax.ShapeDtypeStruct((R, N), F32),
        grid=(N // tn,),
        in_specs=[pl.BlockSpec((R, D), lambda j: (0, 0)),
                  pl.BlockSpec((D, tn), lambda j: (0, j)),
                  pl.BlockSpec((1, tn), lambda j: (0, j))],
        out_specs=pl.BlockSpec((R, tn), lambda j: (0, j)),
        compiler_params=_params(("arbitrary",)),
        name="adaln",
    )(cvecs, w, b.reshape(1, N))


def _tile_groups(kinds, per):
    tiles = [tuple(kinds[t * per:(t + 1) * per]) for t in range(len(kinds) // per)]
    groups = []
    for t, tk in enumerate(tiles):
        if groups and groups[-1][2] == tk:
            groups[-1] = (groups[-1][0], t + 1, tk)
        else:
            groups.append((t, t + 1, tk))
    return tuple(groups)


def _inproj_kernel(*refs, t_groups, nn, rope):
    if rope:
        (x_ref, shift_ref, scale_ref, g_ref, wn_ref, wt_ref, wg_ref, cos_ref, sin_lo_ref, sin_hi_ref,
         p_ref, pt_ref, gt_ref, h_scr) = refs
    else:
        x_ref, shift_ref, scale_ref, g_ref, wn_ref, wt_ref, wg_ref, p_ref, pt_ref, gt_ref, h_scr = refs
    j = pl.program_id(2)
    nt_dims = (((1,), (1,)), ((), ()))

    @pl.when(j == 0)
    def _():
        tm = h_scr.shape[0]
        rc = min(tm, NORM_ROWS)
        for r0 in range(0, tm, rc):
            xf = x_ref[0, r0:r0 + rc, :]
            ms = jnp.mean(xf * xf, axis=-1, keepdims=True)
            y = xf * lax.rsqrt(ms + NORM_EPS) * g_ref[...]
            h_scr[r0:r0 + rc, :] = (y * (1.0 + scale_ref[0]) + shift_ref[0]).astype(BF16)
        gt_ref[0] = lax.dot_general(wg_ref[...], h_scr[...], nt_dims, preferred_element_type=F32)

    @pl.when(j < nn)
    def _():
        p_ref[0] = jnp.dot(h_scr[...], wn_ref[...], preferred_element_type=F32).astype(BF16)

    for lo, hi, kinds in t_groups:
        @pl.when((j >= nn + lo) & (j < nn + hi))
        def _(kinds=kinds):
            acc = lax.dot_general(wt_ref[...], h_scr[...], nt_dims, preferred_element_type=F32)
            for u, kind in enumerate(kinds):
                a = acc[u * LANE:(u + 1) * LANE]
                if "rope" in kind:
                    a = (a * cos_ref[...] + pltpu.roll(a, HEAD_DIM - ROPE_PAIR, 0) * sin_lo_ref[...]
                         + pltpu.roll(a, ROPE_PAIR, 0) * sin_hi_ref[...])
                if "scale" in kind:
                    a = a * QK_SCALE
                pt_ref[0, u * LANE:(u + 1) * LANE, :] = a.astype(BF16)


def _inproj(x, shift, scale, gain, w_nat, w_t, w_g, kinds_t, rope_tabs, tm, tn):
    B, T, D = x.shape
    n_nat = w_nat.shape[1]
    n_t = w_t.shape[0]
    nn, ntt = n_nat // tn, n_t // tn
    per = tn // LANE
    bm = shift.shape[0]
    mod_map = (lambda b, i, j: (b, 0, 0)) if bm == B else (lambda b, i, j: (0, 0, 0))
    rope = rope_tabs is not None
    in_specs = [pl.BlockSpec((1, tm, D), lambda b, i, j: (b, i, 0)),
                pl.BlockSpec((1, 1, D), mod_map),
                pl.BlockSpec((1, 1, D), mod_map),
                pl.BlockSpec((1, D), lambda b, i, j: (0, 0)),
                pl.BlockSpec((D, tn), lambda b, i, j: (0, jnp.minimum(j, nn - 1))),
                pl.BlockSpec((tn, D), lambda b, i, j: (jnp.maximum(j - nn, 0), 0)),
                pl.BlockSpec(w_g.shape, lambda b, i, j: (0, 0))]
    args = [x, shift, scale, gain.reshape(1, D), w_nat, w_t, w_g]
    if rope:
        in_specs += [pl.BlockSpec((HEAD_DIM, tm), lambda b, i, j: (0, i))] * len(rope_tabs)
        args += list(rope_tabs)
    return pl.pallas_call(
        functools.partial(_inproj_kernel, t_groups=_tile_groups(kinds_t, per), nn=nn, rope=rope),
        out_shape=(jax.ShapeDtypeStruct((B, T, n_nat), BF16),
                   jax.ShapeDtypeStruct((B, n_t, T), BF16),
                   jax.ShapeDtypeStruct((B, w_g.shape[0], T), F32)),
        grid=(B, T // tm, nn + ntt),
        in_specs=in_specs,
        out_specs=(pl.BlockSpec((1, tm, tn), lambda b, i, j: (b, i, jnp.minimum(j, nn - 1))),
                   pl.BlockSpec((1, tn, tm), lambda b, i, j: (b, jnp.maximum(j - nn, 0), i)),
                   pl.BlockSpec((1, w_g.shape[0], tm), lambda b, i, j: (b, 0, i))),
        scratch_shapes=[pltpu.VMEM((tm, D), BF16)],
        compiler_params=_params(("parallel", "parallel", "arbitrary")),
        name="inproj_rope" if rope else "inproj_ctx",
    )(*args)


def _scan_lanes(x, op, reverse, fill):
    lane = lax.broadcasted_iota(jnp.int32, x.shape, 1)
    k = 1
    while k < CHUNK:
        if reverse:
            sh = jnp.where(lane < CHUNK - k, pltpu.roll(x, CHUNK - k, 1), fill)
        else:
            sh = jnp.where(lane >= k, pltpu.roll(x, k, 1), fill)
        x = op(x, sh)
        k *= 2
    return x


def _log_sigmoid(z):
    return jnp.minimum(z, 0.0) - jnp.log(1.0 + jnp.exp(-jnp.abs(z)))


def _lane_value(x, lane_idx):
    lane = lax.broadcasted_iota(jnp.int32, x.shape, 1)
    return jnp.broadcast_to(jnp.sum(jnp.where(lane == lane_idx, x, 0.0), axis=1, keepdims=True), x.shape)


def _gate_prep_kernel(gt_ref, bias_ref, row_ref, col_ref, *, nchunk):
    H = MLSTM_HEADS
    for c in range(nchunk):
        sl = slice(c * CHUNK, (c + 1) * CHUNK)
        z = gt_ref[0, :, sl] + bias_ref[...]
        li_f, lf_f = z[0:H], _log_sigmoid(z[H:2 * H])
        li_r, lf_r = z[2 * H:3 * H], _log_sigmoid(z[3 * H:4 * H])
        b_f = _scan_lanes(lf_f, jnp.add, False, 0.0)
        b_r = _scan_lanes(lf_r, jnp.add, True, 0.0)
        u_f = li_f - b_f
        u_r = li_r - b_r
        r_f = _scan_lanes(u_f, jnp.maximum, False, -jnp.inf)
        r_r = _scan_lanes(u_r, jnp.maximum, True, -jnp.inf)
        ends = [_lane_value(r_f, CHUNK - 1), _lane_value(r_r, 0), _lane_value(b_f, CHUNK - 1), _lane_value(b_r, 0)]
        for k, v in enumerate([r_f, r_r, b_f, b_r] + ends):
            row_ref[0, k * H:(k + 1) * H, sl] = v
        stack = jnp.concatenate([u_f, u_r, jnp.zeros((LANE - 2 * H, CHUNK), F32)], axis=0)
        col_ref[0, sl, :] = stack.T


def _gate_prep(g_t, bias):
    B, G, T = g_t.shape
    tg = min(T, 8 * CHUNK)
    H = MLSTM_HEADS
    return pl.pallas_call(
        functools.partial(_gate_prep_kernel, nchunk=tg // CHUNK),
        out_shape=(jax.ShapeDtypeStruct((B, 8 * H, T), F32),
                   jax.ShapeDtypeStruct((B, T, LANE), F32)),
        grid=(B, T // tg),
        in_specs=[pl.BlockSpec((1, G, tg), lambda b, i: (b, 0, i)),
                  pl.BlockSpec((G, CHUNK), lambda b, i: (0, 0))],
        out_specs=(pl.BlockSpec((1, 8 * H, tg), lambda b, i: (b, 0, i)),
                   pl.BlockSpec((1, tg, LANE), lambda b, i: (b, i, 0))),
        compiler_params=_params(("parallel", "parallel")),
        name="gate_prep",
    )(g_t, jnp.broadcast_to(bias.reshape(G, 1), (G, CHUNK)))


def _attn_kernel(sink_ref, qt_ref, kp_ref, kc_ref, kn_ref, vp_ref, vc_ref, vn_ref, kx_ref, vx_ref, band_ref,
                 o_ref, *, nb):
    j = pl.program_id(1)
    cols = ATTN_GROUP * WINDOW
    prev_bias = jnp.where(j > 0, 0.0, NEG)
    next_bias = jnp.where(j < nb - 1, 0.0, NEG)
    row = lax.broadcasted_iota(jnp.int32, (3 * WINDOW, WINDOW), 0)
    edge = jnp.where(row < WINDOW, prev_bias, jnp.where(row >= 2 * WINDOW, next_bias, 0.0))
    bias = band_ref[...] + jnp.concatenate([edge] * ATTN_GROUP, axis=1)
    lane = lax.broadcasted_iota(jnp.int32, (1, cols), 1)
    tn_dims = (((0,), (0,)), ((), ()))
    for g in range(ATTN_KV_HEADS):
        hs = slice(g * HEAD_DIM, (g + 1) * HEAD_DIM)
        qt = jnp.concatenate([qt_ref[0, (g * ATTN_GROUP + h) * HEAD_DIM:(g * ATTN_GROUP + h + 1) * HEAD_DIM, :]
                              for h in range(ATTN_GROUP)], axis=1)
        kt = jnp.concatenate([kp_ref[0, hs, :], kc_ref[0, hs, :], kn_ref[0, hs, :], kx_ref[0, hs, :]], axis=1)
        vt = jnp.concatenate([vp_ref[0, hs, :], vc_ref[0, hs, :], vn_ref[0, hs, :], vx_ref[0, hs, :]], axis=1)
        st = lax.dot_general(kt, qt, tn_dims, preferred_element_type=F32)
        s_loc = st[:3 * WINDOW] + bias
        s_ctx = st[3 * WINDOW:]
        sink = jnp.full((1, cols), sink_ref[g * ATTN_GROUP], F32)
        for h in range(1, ATTN_GROUP):
            sink = jnp.where(lane >= h * WINDOW, sink_ref[g * ATTN_GROUP + h], sink)
        m = jnp.maximum(jnp.maximum(jnp.max(s_loc, axis=0, keepdims=True),
                                    jnp.max(s_ctx, axis=0, keepdims=True)), sink)
        p_loc = jnp.exp(s_loc - m)
        p_ctx = jnp.exp(s_ctx - m)
        den = jnp.sum(p_loc, axis=0, keepdims=True) + jnp.sum(p_ctx, axis=0, keepdims=True) + jnp.exp(sink - m)
        pt = jnp.concatenate([p_loc, p_ctx], axis=0).astype(BF16)
        ot = jnp.dot(vt, pt, preferred_element_type=F32) * (1.0 / den)
        for h in range(ATTN_GROUP):
            r0 = (g * ATTN_GROUP + h) * HEAD_DIM
            o_ref[0, r0:r0 + HEAD_DIM, :] = ot[:, h * WINDOW:(h + 1) * WINDOW].astype(BF16)


def _attention(pt_lat, pt_ctx, sink, q_blk, k_blk, v_blk, kx_blk, vx_blk):
    B, _, S = pt_lat.shape
    C = pt_ctx.shape[2]
    nb = S // WINDOW
    cols = ATTN_GROUP * WINDOW
    t = np.arange(cols)[None, :] % WINDOW
    d = np.arange(3 * WINDOW)[:, None] - t
    band = jnp.asarray(np.where((d >= 0) & (d <= 2 * WINDOW), 0.0, NEG), F32)

    def kspec(off, blk):
        return pl.BlockSpec((1, ATTN_KV_W, WINDOW), lambda b, j: (b, blk, jnp.clip(j + off, 0, nb - 1)))

    return pl.pallas_call(
        functools.partial(_attn_kernel, nb=nb),
        out_shape=jax.ShapeDtypeStruct((B, ATTN_Q_W, S), BF16),
        grid=(B, nb),
        in_specs=[pl.BlockSpec(memory_space=pltpu.SMEM),
                  pl.BlockSpec((1, ATTN_Q_W, WINDOW), lambda b, j: (b, q_blk, j)),
                  kspec(-1, k_blk), kspec(0, k_blk), kspec(1, k_blk),
                  kspec(-1, v_blk), kspec(0, v_blk), kspec(1, v_blk),
                  pl.BlockSpec((1, ATTN_KV_W, C), lambda b, j: (b, kx_blk, 0)),
                  pl.BlockSpec((1, ATTN_KV_W, C), lambda b, j: (b, vx_blk, 0)),
                  pl.BlockSpec((3 * WINDOW, cols), lambda b, j: (0, 0))],
        out_specs=pl.BlockSpec((1, ATTN_Q_W, WINDOW), lambda b, j: (b, 0, j)),
        compiler_params=_params(("parallel", "arbitrary")),
        name="window_attn",
    )(sink, pt_lat, pt_lat, pt_lat, pt_lat, pt_lat, pt_lat, pt_lat, pt_ctx, pt_ctx, band)


def _v_aug_t(vt):
    sub = lax.broadcasted_iota(jnp.int32, (BF16_SUBLANES, vt.shape[1]), 0)
    return jnp.concatenate([vt, jnp.where(sub == 0, 1.0, 0.0).astype(BF16)], axis=0)


def _state_update(ct_old, m_old, k, vaug_t, u_bc, r_end, b_end):
    m_end = jnp.maximum(m_old, r_end)
    ks = (k.astype(F32) * jnp.exp(u_bc - m_end)).astype(BF16)
    ct_new = jnp.exp(m_old - m_end) * ct_old + jnp.dot(vaug_t, ks, preferred_element_type=F32)
    return ct_new, b_end + m_end


def _row(row_ref, k, h, ts):
    i = k * MLSTM_HEADS + h
    return row_ref[0, i:i + 1, ts]


def _scan_kernel(k_ref, vt_ref, row_ref, col_ref, c0_ref, m0_ref, cs_ref, ms_ref, cf_ref, mf_ref,
                 c_scr, m_scr, *, reverse, nsteps, cps):
    H = MLSTM_HEADS
    n = pl.program_id(1)
    d = 1 if reverse else 0

    @pl.when(n == 0)
    def _():
        c_scr[...] = c0_ref[0]
        m_scr[...] = m0_ref[0]

    for c in (range(cps - 1, -1, -1) if reverse else range(cps)):
        ts = slice(c * CHUNK, (c + 1) * CHUNK)
        cols = col_ref[0, ts, :]
        for h in range(H):
            ct_old = c_scr[h]
            m_old = m_scr[h:h + 1, :]
            cs_ref[0, c, h] = ct_old.astype(BF16)
            ms_ref[0, c, h:h + 1, :] = m_old
            k = k_ref[0, ts, h * MLSTM_QK_DIM:(h + 1) * MLSTM_QK_DIM]
            vaug_t = _v_aug_t(vt_ref[0, h * MLSTM_V_DIM:(h + 1) * MLSTM_V_DIM, ts])
            u_bc = jnp.broadcast_to(cols[:, d * H + h:d * H + h + 1], (CHUNK, LANE))
            ct_new, m_new = _state_update(ct_old, m_old, k, vaug_t, u_bc, _row(row_ref, 4 + d, h, ts),
                                          _row(row_ref, 6 + d, h, ts))
            c_scr[h] = ct_new
            m_scr[h:h + 1, :] = m_new

    @pl.when(n == nsteps - 1)
    def _():
        cf_ref[0] = c_scr[...]
        mf_ref[0] = m_scr[...]


def _state_scan(p, k_blk, p_t, vt_blk, rows, cols, c0, m0, reverse, cps):
    B, T, _ = p.shape
    nc = T // CHUNK
    cps = min(cps, nc)
    ns = nc // cps
    assert ns * cps == nc
    H = MLSTM_HEADS
    tb = cps * CHUNK
    cidx = (lambda n: ns - 1 - n) if reverse else (lambda n: n)
    st = (H, C_ROWS, MLSTM_QK_DIM)
    return pl.pallas_call(
        functools.partial(_scan_kernel, reverse=reverse, nsteps=ns, cps=cps),
        out_shape=(jax.ShapeDtypeStruct((B, nc) + st, BF16),
                   jax.ShapeDtypeStruct((B, nc, H, LANE), F32),
                   jax.ShapeDtypeStruct((B,) + st, F32),
                   jax.ShapeDtypeStruct((B, H, LANE), F32)),
        grid=(B, ns),
        in_specs=[pl.BlockSpec((1, tb, MLSTM_QK_W), lambda b, n: (b, cidx(n), k_blk)),
                  pl.BlockSpec((1, MLSTM_V_W, tb), lambda b, n: (b, vt_blk, cidx(n))),
                  pl.BlockSpec((1, 8 * H, tb), lambda b, n: (b, 0, cidx(n))),
                  pl.BlockSpec((1, tb, LANE), lambda b, n: (b, cidx(n), 0)),
                  pl.BlockSpec((1,) + st, lambda b, n: (b, 0, 0, 0)),
                  pl.BlockSpec((1, H, LANE), lambda b, n: (b, 0, 0))],
        out_specs=(pl.BlockSpec((1, cps) + st, lambda b, n: (b, cidx(n), 0, 0, 0)),
                   pl.BlockSpec((1, cps, H, LANE), lambda b, n: (b, cidx(n), 0, 0)),
                   pl.BlockSpec((1,) + st, lambda b, n: (b, 0, 0, 0)),
                   pl.BlockSpec((1, H, LANE), lambda b, n: (b, 0, 0))),
        scratch_shapes=[pltpu.VMEM(st, F32), pltpu.VMEM((H, LANE), F32)],
        compiler_params=_params(("parallel", "arbitrary")),
        name="mlstm_scan_rev" if reverse else "mlstm_scan_fwd",
    )(p, p_t, rows, cols, c0, m0)


def _mlstm_out_kernel(k_ref, qt_ref, vt_ref, mot_ref, row_ref, col_ref, cr_ref, mr_ref, c0_ref, m0_ref, g_ref,
                      o_ref, c_scr, m_scr, *, cps):
    H = MLSTM_HEADS
    L = CHUNK
    n = pl.program_id(1)

    @pl.when(n == 0)
    def _():
        c_scr[...] = c0_ref[0]
        m_scr[...] = m0_ref[0]

    si = lax.broadcasted_iota(jnp.int32, (L, L), 0)
    ti = lax.broadcasted_iota(jnp.int32, (L, L), 1)
    for c in range(cps):
        ts = slice(c * L, (c + 1) * L)
        cols = col_ref[0, ts, :]
        for h in range(H):
            qt = qt_ref[0, h * MLSTM_QK_DIM:(h + 1) * MLSTM_QK_DIM, ts]
            k = k_ref[0, ts, h * MLSTM_QK_DIM:(h + 1) * MLSTM_QK_DIM]
            vs = slice(h * MLSTM_V_DIM, (h + 1) * MLSTM_V_DIM)
            vaug_t = _v_aug_t(vt_ref[0, vs, ts])
            pt = jnp.dot(k, qt, preferred_element_type=F32)
            qt_f = qt.astype(F32)
            ct_f = c_scr[h]
            hs_t = None
            for d in range(N_DIRS):
                u_bc = jnp.broadcast_to(cols[:, d * H + h:d * H + h + 1], (L, LANE))
                m0 = m_scr[h:h + 1, :] if d == 0 else mr_ref[0, c, h:h + 1, :]
                ct = ct_f.astype(BF16) if d == 0 else cr_ref[0, c, h]
                mx = jnp.maximum(_row(row_ref, d, h, ts), m0)
                valid = (si <= ti) if d == 0 else (si >= ti)
                s_t = (pt * jnp.exp(jnp.where(valid, u_bc - mx, -jnp.inf))).astype(BF16)
                q_in = (qt_f * jnp.exp(m0 - mx)).astype(BF16)
                tot = jnp.dot(jnp.concatenate([vaug_t, ct], axis=1), jnp.concatenate([s_t, q_in], axis=0),
                              preferred_element_type=F32)
                den = jnp.maximum(jnp.abs(tot[MLSTM_V_DIM:MLSTM_V_DIM + 1]),
                                  jnp.exp(-_row(row_ref, 2 + d, h, ts) - mx))
                part = tot[:MLSTM_V_DIM] * (1.0 / den)
                hs_t = part if hs_t is None else hs_t + part
                if d == 0:
                    u_bc_f = u_bc
            hn = hs_t * lax.rsqrt(jnp.mean(hs_t * hs_t, axis=0, keepdims=True) + NORM_EPS) * g_ref[vs, :]
            o_ref[0, vs, ts] = (_sigmoid(mot_ref[0, vs, ts].astype(F32)) * hn).astype(BF16)
            ct_new, m_new = _state_update(ct_f, m_scr[h:h + 1, :], k, vaug_t, u_bc_f, _row(row_ref, 4, h, ts),
                                          _row(row_ref, 6, h, ts))
            c_scr[h] = ct_new
            m_scr[h:h + 1, :] = m_new


def _mlstm_out(p, k_blk, p_t, qt_blk, vt_blk, mot_blk, rows, cols, c_rev, m_rev, c0, m0, gain, cps):
    B, S, _ = p.shape
    nc = S // CHUNK
    ns = nc // cps
    assert ns * cps == nc
    H = MLSTM_HEADS
    tb = cps * CHUNK
    st = (H, C_ROWS, MLSTM_QK_DIM)
    gain_bc = jnp.broadcast_to(gain.reshape(MLSTM_V_W, 1), (MLSTM_V_W, LANE))
    return pl.pallas_call(
        functools.partial(_mlstm_out_kernel, cps=cps),
        out_shape=jax.ShapeDtypeStruct((B, MLSTM_V_W, S), BF16),
        grid=(B, ns),
        in_specs=[pl.BlockSpec((1, tb, MLSTM_QK_W), lambda b, n: (b, n, k_blk)),
                  pl.BlockSpec((1, MLSTM_QK_W, tb), lambda b, n: (b, qt_blk, n)),
                  pl.BlockSpec((1, MLSTM_V_W, tb), lambda b, n: (b, vt_blk, n)),
                  pl.BlockSpec((1, MLSTM_V_W, tb), lambda b, n: (b, mot_blk, n)),
                  pl.BlockSpec((1, 8 * H, tb), lambda b, n: (b, 0, n)),
                  pl.BlockSpec((1, tb, LANE), lambda b, n: (b, n, 0)),
                  pl.BlockSpec((1, cps) + st, lambda b, n: (b, n, 0, 0, 0)),
                  pl.BlockSpec((1, cps, H, LANE), lambda b, n: (b, n, 0, 0)),
                  pl.BlockSpec((1,) + st, lambda b, n: (b, 0, 0, 0)),
                  pl.BlockSpec((1, H, LANE), lambda b, n: (b, 0, 0)),
                  pl.BlockSpec((MLSTM_V_W, LANE), lambda b, n: (0, 0))],
        out_specs=pl.BlockSpec((1, MLSTM_V_W, tb), lambda b, n: (b, 0, n)),
        scratch_shapes=[pltpu.VMEM(st, F32), pltpu.VMEM((H, LANE), F32)],
        compiler_params=_params(("parallel", "arbitrary")),
        name="mlstm_out",
    )(p, p_t, p_t, p_t, rows, cols, c_rev, m_rev, c0, m0, gain_bc)


def _merge_kernel(attt_ref, memt_ref, ga_ref, gm_ref, x_ref, gate_ref, wa_ref, wm_ref, wo_ref, o_ref):
    tn_dims = (((0,), (0,)), ((), ()))
    a = lax.dot_general(attt_ref[0], wa_ref[...], tn_dims, preferred_element_type=F32)
    m = lax.dot_general(memt_ref[0], wm_ref[...], tn_dims, preferred_element_type=F32)
    y = _sigmoid(ga_ref[0].astype(F32)) * a + _sigmoid(gm_ref[0].astype(F32)) * m
    z = jnp.dot(y.astype(BF16), wo_ref[...], preferred_element_type=F32)
    o_ref[0] = x_ref[0] + gate_ref[0] * z


def _merge(att_t, mem_t, p, ga_blk, gm_blk, x, gate1, w_ap, w_mp, w_out, tm):
    B, S, D = x.shape
    resident = functools.partial(pl.BlockSpec, pipeline_mode=pl.Buffered(1))
    return pl.pallas_call(
        _merge_kernel,
        out_shape=jax.ShapeDtypeStruct((B, S, D), F32),
        grid=(B, S // tm),
        in_specs=[pl.BlockSpec((1, ATTN_Q_W, tm), lambda b, i: (b, 0, i)),
                  pl.BlockSpec((1, MLSTM_V_W, tm), lambda b, i: (b, 0, i)),
                  pl.BlockSpec((1, tm, D), lambda b, i: (b, i, ga_blk)),
                  pl.BlockSpec((1, tm, D), lambda b, i: (b, i, gm_blk)),
                  pl.BlockSpec((1, tm, D), lambda b, i: (b, i, 0)),
                  pl.BlockSpec((1, 1, D), lambda b, i: (b, 0, 0)),
                  resident(w_ap.shape, lambda b, i: (0, 0)),
                  resident(w_mp.shape, lambda b, i: (0, 0)),
                  resident(w_out.shape, lambda b, i: (0, 0))],
        out_specs=pl.BlockSpec((1, tm, D), lambda b, i: (b, i, 0)),
        compiler_params=_params(("parallel", "parallel")),
        name="merge_outproj",
    )(att_t, mem_t, p, p, x, gate1, w_ap, w_mp, w_out)


def _ffn_kernel(x_ref, shift_ref, scale_ref, gate_ref, g2_ref, wg_ref, wu_ref, wo_ref, gf_ref, o_ref,
                h_scr, *, nf):
    f = pl.program_id(2)
    tm = h_scr.shape[0]
    rc = min(tm, NORM_ROWS)

    @pl.when(f == 0)
    def _():
        for r0 in range(0, tm, rc):
            xf = x_ref[0, r0:r0 + rc, :]
            ms = jnp.mean(xf * xf, axis=-1, keepdims=True)
            y = xf * lax.rsqrt(ms + NORM_EPS) * g2_ref[...]
            h_scr[r0:r0 + rc, :] = (y * (1.0 + scale_ref[0]) + shift_ref[0]).astype(BF16)
        o_ref[...] = jnp.zeros_like(o_ref)

    h = h_scr[...]
    gt = jnp.dot(h, wg_ref[...], preferred_element_type=F32)
    up = jnp.dot(h, wu_ref[...], preferred_element_type=F32)
    act = (gt * _sigmoid(gt) * up).astype(BF16)
    o_ref[0] += jnp.dot(act, wo_ref[...], preferred_element_type=F32)

    @pl.when(f == nf - 1)
    def _():
        for r0 in range(0, tm, rc):
            x2 = x_ref[0, r0:r0 + rc, :] + gate_ref[0] * o_ref[0, r0:r0 + rc, :]
            ms = jnp.mean(x2 * x2, axis=-1, keepdims=True)
            o_ref[0, r0:r0 + rc, :] = x2 * lax.rsqrt(ms + NORM_EPS) * gf_ref[...]


def _ffn(x1, shift2, scale2, gate2, norm2_g, w_in, w_out, final_g, tm, tf):
    B, S, D = x1.shape
    dff = w_out.shape[0]
    nf = dff // tf
    return pl.pallas_call(
        functools.partial(_ffn_kernel, nf=nf),
        out_shape=jax.ShapeDtypeStruct((B, S, D), F32),
        grid=(B, S // tm, nf),
        in_specs=[pl.BlockSpec((1, tm, D), lambda b, i, f: (b, i, 0)),
                  pl.BlockSpec((1, 1, D), lambda b, i, f: (b, 0, 0)),
                  pl.BlockSpec((1, 1, D), lambda b, i, f: (b, 0, 0)),
                  pl.BlockSpec((1, 1, D), lambda b, i, f: (b, 0, 0)),
                  pl.BlockSpec((1, D), lambda b, i, f: (0, 0)),
                  pl.BlockSpec((D, tf), lambda b, i, f: (0, f)),
                  pl.BlockSpec((D, tf), lambda b, i, f: (0, nf + f)),
                  pl.BlockSpec((tf, D), lambda b, i, f: (f, 0)),
                  pl.BlockSpec((1, D), lambda b, i, f: (0, 0))],
        out_specs=pl.BlockSpec((1, tm, D), lambda b, i, f: (b, i, 0)),
        scratch_shapes=[pltpu.VMEM((tm, D), BF16)],
        compiler_params=_params(("parallel", "parallel", "arbitrary")),
        name="ffn_final_norm",
    )(x1, shift2, scale2, gate2, norm2_g.reshape(1, D), w_in, w_in, w_out, final_g.reshape(1, D))


def _rope_tables(S):
    pos = jnp.arange(S)
    rows = (pos // GRID_W).astype(F32)
    cols = (pos % GRID_W).astype(F32)
    inv_freq = ROPE_BASE ** (-jnp.arange(ROPE_PAIR, dtype=F32) / ROPE_PAIR)
    ar = rows[:, None] * inv_freq[None, :]
    ac = cols[:, None] * inv_freq[None, :]
    zero = jnp.zeros_like(ar)
    cos = jnp.concatenate([jnp.cos(ar), jnp.cos(ar), jnp.cos(ac), jnp.cos(ac)], axis=1)
    sin_lo = jnp.concatenate([-jnp.sin(ar), zero, -jnp.sin(ac), zero], axis=1)
    sin_hi = jnp.concatenate([zero, jnp.sin(ar), zero, jnp.sin(ac)], axis=1)
    return cos.T, sin_lo.T, sin_hi.T


def kernel(x, c, ctx, c_ctx, w_ada, b_ada, norm1_g, w_in, b_gates, attn_sink, mlstm_norm_g, w_attn_proj,
           w_mlstm_proj, w_out, norm2_g, w_ffn_in, w_ffn_out, final_norm_g):
    B, S, D = x.shape
    C = ctx.shape[1]
    assert w_ada.shape[0] == 1, "single-layer configuration"
    assert S % 512 == 0 and C % CHUNK == 0 and S % GRID_W == 0
    H = MLSTM_HEADS
    tn = 1024

    rows = -(-(B + 1) // 8) * 8
    cvecs = jnp.concatenate([c, c_ctx[None], jnp.zeros((rows - B - 1, D), F32)], axis=0)
    mod = _adaln(cvecs, w_ada[0], b_ada[0])
    shift1, scale1, gate1, shift2, scale2, gate2 = [mod[:B, k * D:(k + 1) * D].reshape(B, 1, D) for k in range(N_MOD)]
    shift_c = mod[B:B + 1, 0:D].reshape(1, 1, D)
    scale_c = mod[B:B + 1, D:2 * D].reshape(1, 1, D)

    wi = w_in[0]
    o = 0
    parts = {}
    for name, width in (("a_k", ATTN_KV_W), ("a_v", ATTN_KV_W), ("m_k", MLSTM_QK_W), ("m_v", MLSTM_V_W),
                        ("m_g", N_GATE), ("a_q", ATTN_Q_W), ("m_q", MLSTM_QK_W), ("m_o", MLSTM_V_W),
                        ("g_att", D), ("g_mem", D)):
        parts[name] = wi[:, o:o + width]
        o += width
    w_nat = jnp.concatenate([parts["g_att"], parts["g_mem"], parts["m_k"]], axis=1).astype(BF16)
    w_t = jnp.concatenate([parts["m_v"], parts["m_o"], parts["a_q"], parts["m_q"], parts["a_k"], parts["a_v"]],
                          axis=1).T.astype(BF16)
    w_g = parts["m_g"].T.astype(BF16)
    kinds_t = ([""] * (2 * MLSTM_V_W // LANE) + ["rope scale"] * (ATTN_Q_W // LANE)
               + ["scale"] * (MLSTM_QK_W // LANE) + ["rope"] * (ATTN_KV_W // LANE) + [""] * (ATTN_KV_W // LANE))
    assert (2 * D) % MLSTM_QK_W == 0 and (2 * MLSTM_V_W) % ATTN_Q_W == 0
    ga_blk, gm_blk, mk_blk = 0, 1, 2 * D // MLSTM_QK_W
    vt_blk, mot_blk = 0, 1
    aq_blk = 2 * MLSTM_V_W // ATTN_Q_W
    qt_blk = (2 * MLSTM_V_W + ATTN_Q_W) // MLSTM_QK_W
    kt_blk = (2 * MLSTM_V_W + ATTN_Q_W + MLSTM_QK_W) // ATTN_KV_W
    vtt_blk = kt_blk + 1
    w_nat_c = parts["m_k"].astype(BF16)
    w_t_c = jnp.concatenate([parts["m_v"], parts["a_k"], parts["a_v"]], axis=1).T.astype(BF16)
    kx_blk, vx_blk = MLSTM_V_W // ATTN_KV_W, MLSTM_V_W // ATTN_KV_W + 1

    p_lat, pt_lat, gt_lat = _inproj(x, shift1, scale1, norm1_g[0], w_nat, w_t, w_g, kinds_t,
                                    _rope_tables(S), tm=1024 if S % 1024 == 0 else 512, tn=tn)
    p_ctx, pt_ctx, gt_ctx = _inproj(ctx, shift_c, scale_c, norm1_g[0], w_nat_c, w_t_c, w_g,
                                    [""] * (w_t_c.shape[0] // LANE), None, tm=min(C, 256), tn=tn)

    att_t = _attention(pt_lat, pt_ctx, attn_sink[0], aq_blk, kt_blk, vtt_blk, kx_blk, vx_blk)

    rows_lat, cols_lat = _gate_prep(gt_lat, b_gates[0])
    rows_ctx, cols_ctx = _gate_prep(gt_ctx, b_gates[0])
    c_zero = jnp.zeros((B, H, C_ROWS, MLSTM_QK_DIM), F32)
    m_zero = jnp.zeros((B, H, LANE), F32)
    _, _, cf_ctx, mf_ctx = _state_scan(p_ctx, 0, pt_ctx, 0, rows_ctx, cols_ctx, c_zero, m_zero, reverse=False,
                                       cps=4)
    _, _, cr_ctx, mr_ctx = _state_scan(p_ctx, 0, pt_ctx, 0, rows_ctx, cols_ctx, c_zero, m_zero, reverse=True,
                                       cps=4)
    c_rev, m_rev, _, _ = _state_scan(p_lat, mk_blk, pt_lat, vt_blk, rows_lat, cols_lat, cr_ctx, mr_ctx,
                                     reverse=True, cps=4)
    mem_t = _mlstm_out(p_lat, mk_blk, pt_lat, qt_blk, vt_blk, mot_blk, rows_lat, cols_lat, c_rev, m_rev,
                       cf_ctx, mf_ctx, mlstm_norm_g[0], cps=2)

    x1 = _merge(att_t, mem_t, p_lat, ga_blk, gm_blk, x, gate1, w_attn_proj[0].astype(BF16),
                w_mlstm_proj[0].astype(BF16), w_out[0].astype(BF16), tm=256)
    return _ffn(x1, shift2, scale2, gate2, norm2_g[0], w_ffn_in[0].astype(BF16), w_ffn_out[0].astype(BF16),
                final_norm_g, tm=512, tf=512)
```

```python
import functools

import numpy as np
import jax
import jax.numpy as jnp
from jax import lax
from jax.experimental import pallas as pl
from jax.experimental.pallas import tpu as pltpu

F32 = jnp.float32
BF16 = jnp.bfloat16

GRID_W = 64
ATTN_HEADS = 16
ATTN_KV_HEADS = 4
ATTN_GROUP = ATTN_HEADS // ATTN_KV_HEADS
HEAD_DIM = 128
WINDOW = 128
ROPE_BASE = 10000.0
MLSTM_HEADS = 8
MLSTM_QK_DIM = 128
MLSTM_V_DIM = 256
CHUNK = 128
N_DIRS = 2
N_GATE = N_DIRS * 2 * MLSTM_HEADS
NORM_EPS = 1e-6
N_MOD = 6
QK_SCALE = HEAD_DIM ** -0.5
LOG2E = 1.4426950408889634
ROPE_PAIR = HEAD_DIM // 4

LANE = 128
BF16_SUBLANES = 16
V7X_VMEM_BYTES = 64 * 1024 * 1024
VMEM_LIMIT = V7X_VMEM_BYTES - 1 * 1024 * 1024

ATTN_Q_W = ATTN_HEADS * HEAD_DIM
ATTN_KV_W = ATTN_KV_HEADS * HEAD_DIM
MLSTM_QK_W = MLSTM_HEADS * MLSTM_QK_DIM
MLSTM_V_W = MLSTM_HEADS * MLSTM_V_DIM
C_ROWS = MLSTM_V_DIM + BF16_SUBLANES
NORM_ROWS = 128
X1_LOOKAHEAD = 3

NEG = -1e30


def _params(sem):
    return pltpu.CompilerParams(dimension_semantics=sem, vmem_limit_bytes=VMEM_LIMIT)


def _sigmoid(x):
    return 1.0 / (1.0 + jnp.exp(-x))


def _adaln_kernel(c_ref, w_ref, b_ref, o_ref):
    cc = c_ref[...]
    s = (cc * _sigmoid(cc)).astype(BF16)
    o_ref[...] = jnp.dot(s, w_ref[...].astype(BF16), preferred_element_type=F32) + b_ref[...]


def _adaln(cvecs, w, b):
    R, D = cvecs.shape
    N = w.shape[1]
    tn = 1024
    return pl.pallas_call(
        _adaln_kernel,
        out_shape=jax.ShapeDtypeStruct((R, N), F32),
        grid=(N // tn,),
        in_specs=[pl.BlockSpec((R, D), lambda j: (0, 0)),
                  pl.BlockSpec((D, tn), lambda j: (0, j)),
                  pl.BlockSpec((1, tn), lambda j: (0, j))],
        out_specs=pl.BlockSpec((R, tn), lambda j: (0, j)),
        compiler_params=_params(("arbitrary",)),
        name="adaln",
    )(cvecs, w, b.reshape(1, N))


def _tile_groups(kinds, per):
    tiles = [tuple(kinds[t * per:(t + 1) * per]) for t in range(len(kinds) // per)]
    groups = []
    for t, tk in enumerate(tiles):
        if groups and groups[-1][2] == tk:
            groups[-1] = (groups[-1][0], t + 1, tk)
        else:
            groups.append((t, t + 1, tk))
    return tuple(groups)


def _inproj_kernel(*refs, t_groups, nn, rope):
    if rope:
        (x_ref, shift_ref, scale_ref, g_ref, wn_ref, wt_ref, wg_ref, cos_ref, sin_lo_ref, sin_hi_ref,
         p_ref, pt_ref, gt_ref, h_scr) = refs
    else:
        x_ref, shift_ref, scale_ref, g_ref, wn_ref, wt_ref, wg_ref, p_ref, pt_ref, gt_ref, h_scr = refs
    j = pl.program_id(2)
    nt_dims = (((1,), (1,)), ((), ()))

    @pl.when(j == 0)
    def _():
        tm = h_scr.shape[0]
        rc = min(tm, NORM_ROWS)
        for r0 in range(0, tm, rc):
            xf = x_ref[0, r0:r0 + rc, :]
            ms = jnp.mean(xf * xf, axis=-1, keepdims=True)
            y = xf * lax.rsqrt(ms + NORM_EPS) * g_ref[...]
            h_scr[r0:r0 + rc, :] = (y * (1.0 + scale_ref[0]) + shift_ref[0]).astype(BF16)
        gt_ref[0] = lax.dot_general(wg_ref[...], h_scr[...], nt_dims, preferred_element_type=F32)

    @pl.when(j < nn)
    def _():
        p_ref[0] = jnp.dot(h_scr[...], wn_ref[...], preferred_element_type=F32).astype(BF16)

    for lo, hi, kinds in t_groups:
        @pl.when((j >= nn + lo) & (j < nn + hi))
        def _(kinds=kinds):
            acc = lax.dot_general(wt_ref[...], h_scr[...], nt_dims, preferred_element_type=F32)
            for u, kind in enumerate(kinds):
                a = acc[u * LANE:(u + 1) * LANE]
                if "rope" in kind:
                    a = (a * cos_ref[...] + pltpu.roll(a, HEAD_DIM - ROPE_PAIR, 0) * sin_lo_ref[...]
                         + pltpu.roll(a, ROPE_PAIR, 0) * sin_hi_ref[...])
                if "scale" in kind:
                    a = a * (QK_SCALE * LOG2E if "log2" in kind else QK_SCALE)
                pt_ref[0, u * LANE:(u + 1) * LANE, :] = a.astype(BF16)


def _inproj(x, shift, scale, gain, w_nat, w_t, w_g, kinds_t, rope_tabs, tm, tn):
    B, T, D = x.shape
    n_nat = w_nat.shape[1]
    n_t = w_t.shape[0]
    nn, ntt = n_nat // tn, n_t // tn
    per = tn // LANE
    bm = shift.shape[0]
    mod_map = (lambda b, i, j: (b, 0, 0)) if bm == B else (lambda b, i, j: (0, 0, 0))
    rope = rope_tabs is not None
    in_specs = [pl.BlockSpec((1, tm, D), lambda b, i, j: (b, i, 0)),
                pl.BlockSpec((1, 1, D), mod_map),
                pl.BlockSpec((1, 1, D), mod_map),
                pl.BlockSpec((1, D), lambda b, i, j: (0, 0)),
                pl.BlockSpec((D, tn), lambda b, i, j: (0, jnp.minimum(j, nn - 1))),
                pl.BlockSpec((tn, D), lambda b, i, j: (jnp.maximum(j - nn, 0), 0)),
                pl.BlockSpec(w_g.shape, lambda b, i, j: (0, 0))]
    args = [x, shift, scale, gain.reshape(1, D), w_nat, w_t, w_g]
    if rope:
        in_specs += [pl.BlockSpec((HEAD_DIM, tm), lambda b, i, j: (0, i))] * len(rope_tabs)
        args += list(rope_tabs)
    return pl.pallas_call(
        functools.partial(_inproj_kernel, t_groups=_tile_groups(kinds_t, per), nn=nn, rope=rope),
        out_shape=(jax.ShapeDtypeStruct((B, T, n_nat), BF16),
                   jax.ShapeDtypeStruct((B, n_t, T), BF16),
                   jax.ShapeDtypeStruct((B, w_g.shape[0], T), F32)),
        grid=(B, T // tm, nn + ntt),
        in_specs=in_specs,
        out_specs=(pl.BlockSpec((1, tm, tn), lambda b, i, j: (b, i, jnp.minimum(j, nn - 1))),
                   pl.BlockSpec((1, tn, tm), lambda b, i, j: (b, jnp.maximum(j - nn, 0), i)),
                   pl.BlockSpec((1, w_g.shape[0], tm), lambda b, i, j: (b, 0, i))),
        scratch_shapes=[pltpu.VMEM((tm, D), BF16)],
        compiler_params=_params(("parallel", "parallel", "arbitrary")),
        name="inproj_rope" if rope else "inproj_ctx",
    )(*args)


def _scan_lanes(x, op, reverse, fill):
    lane = lax.broadcasted_iota(jnp.int32, x.shape, 1)
    k = 1
    while k < CHUNK:
        if reverse:
            sh = jnp.where(lane < CHUNK - k, pltpu.roll(x, CHUNK - k, 1), fill)
        else:
            sh = jnp.where(lane >= k, pltpu.roll(x, k, 1), fill)
        x = op(x, sh)
        k *= 2
    return x


def _log_sigmoid(z):
    return jnp.minimum(z, 0.0) - jnp.log(1.0 + jnp.exp(-jnp.abs(z)))


def _lane_value(x, lane_idx):
    lane = lax.broadcasted_iota(jnp.int32, x.shape, 1)
    return jnp.broadcast_to(jnp.sum(jnp.where(lane == lane_idx, x, 0.0), axis=1, keepdims=True), x.shape)


def _gate_prep_kernel(gt_ref, bias_ref, row_ref, col_ref, *, nchunk):
    H = MLSTM_HEADS
    for c in range(nchunk):
        sl = slice(c * CHUNK, (c + 1) * CHUNK)
        z = gt_ref[0, :, sl] + bias_ref[...]
        li_f, lf_f = z[0:H] * LOG2E, _log_sigmoid(z[H:2 * H]) * LOG2E
        li_r, lf_r = z[2 * H:3 * H] * LOG2E, _log_sigmoid(z[3 * H:4 * H]) * LOG2E
        b_f = _scan_lanes(lf_f, jnp.add, False, 0.0)
        b_r = _scan_lanes(lf_r, jnp.add, True, 0.0)
        u_f = li_f - b_f
        u_r = li_r - b_r
        r_f = _scan_lanes(u_f, jnp.maximum, False, -jnp.inf)
        r_r = _scan_lanes(u_r, jnp.maximum, True, -jnp.inf)
        ends = [_lane_value(r_f, CHUNK - 1), _lane_value(r_r, 0), _lane_value(b_f, CHUNK - 1), _lane_value(b_r, 0)]
        for k, v in enumerate([r_f, r_r, b_f, b_r] + ends):
            row_ref[0, k * H:(k + 1) * H, sl] = v
        stack = jnp.concatenate([u_f, u_r, jnp.zeros((LANE - 2 * H, CHUNK), F32)], axis=0)
        col_ref[0, sl, :] = stack.T


def _gate_prep(g_t, bias):
    B, G, T = g_t.shape
    tg = min(T, 8 * CHUNK)
    H = MLSTM_HEADS
    return pl.pallas_call(
        functools.partial(_gate_prep_kernel, nchunk=tg // CHUNK),
        out_shape=(jax.ShapeDtypeStruct((B, 8 * H, T), F32),
                   jax.ShapeDtypeStruct((B, T, LANE), F32)),
        grid=(B, T // tg),
        in_specs=[pl.BlockSpec((1, G, tg), lambda b, i: (b, 0, i)),
                  pl.BlockSpec((G, CHUNK), lambda b, i: (0, 0))],
        out_specs=(pl.BlockSpec((1, 8 * H, tg), lambda b, i: (b, 0, i)),
                   pl.BlockSpec((1, tg, LANE), lambda b, i: (b, i, 0))),
        compiler_params=_params(("parallel", "parallel")),
        name="gate_prep",
    )(g_t, jnp.broadcast_to(bias.reshape(G, 1), (G, CHUNK)))


def _attn_kernel(sink_ref, qt_ref, kp_ref, kc_ref, kn_ref, vp_ref, vc_ref, vn_ref, kx_ref, vx_ref, band_ref,
                 o_ref, *, nb):
    j = pl.program_id(1)
    cols = ATTN_GROUP * WINDOW
    prev_bias = jnp.where(j > 0, 0.0, NEG)
    next_bias = jnp.where(j < nb - 1, 0.0, NEG)
    row = lax.broadcasted_iota(jnp.int32, (3 * WINDOW, WINDOW), 0)
    edge = jnp.where(row < WINDOW, prev_bias, jnp.where(row >= 2 * WINDOW, next_bias, 0.0))
    bias = band_ref[...] + jnp.concatenate([edge] * ATTN_GROUP, axis=1)
    lane = lax.broadcasted_iota(jnp.int32, (1, cols), 1)
    tn_dims = (((0,), (0,)), ((), ()))
    for g in range(ATTN_KV_HEADS):
        hs = slice(g * HEAD_DIM, (g + 1) * HEAD_DIM)
        qt = jnp.concatenate([qt_ref[0, (g * ATTN_GROUP + h) * HEAD_DIM:(g * ATTN_GROUP + h + 1) * HEAD_DIM, :]
                              for h in range(ATTN_GROUP)], axis=1)
        kt = jnp.concatenate([kp_ref[0, hs, :], kc_ref[0, hs, :], kn_ref[0, hs, :], kx_ref[0, hs, :]], axis=1)
        vt = jnp.concatenate([vp_ref[0, hs, :], vc_ref[0, hs, :], vn_ref[0, hs, :], vx_ref[0, hs, :]], axis=1)
        st = lax.dot_general(kt, qt, tn_dims, preferred_element_type=F32)
        s_loc = st[:3 * WINDOW] + bias
        s_ctx = st[3 * WINDOW:]
        sink = jnp.full((1, cols), sink_ref[g * ATTN_GROUP] * LOG2E, F32)
        for h in range(1, ATTN_GROUP):
            sink = jnp.where(lane >= h * WINDOW, sink_ref[g * ATTN_GROUP + h] * LOG2E, sink)
        m = jnp.maximum(jnp.maximum(jnp.max(s_loc, axis=0, keepdims=True),
                                    jnp.max(s_ctx, axis=0, keepdims=True)), sink)
        p_loc = jnp.exp2(s_loc - m)
        p_ctx = jnp.exp2(s_ctx - m)
        den = jnp.sum(p_loc, axis=0, keepdims=True) + jnp.sum(p_ctx, axis=0, keepdims=True) + jnp.exp2(sink - m)
        pt = jnp.concatenate([p_loc, p_ctx], axis=0).astype(BF16)
        ot = jnp.dot(vt, pt, preferred_element_type=F32) * (1.0 / den)
        for h in range(ATTN_GROUP):
            r0 = (g * ATTN_GROUP + h) * HEAD_DIM
            o_ref[0, r0:r0 + HEAD_DIM, :] = ot[:, h * WINDOW:(h + 1) * WINDOW].astype(BF16)


def _attention(pt_lat, pt_ctx, sink, q_blk, k_blk, v_blk, kx_blk, vx_blk):
    B, _, S = pt_lat.shape
    C = pt_ctx.shape[2]
    nb = S // WINDOW
    cols = ATTN_GROUP * WINDOW
    t = np.arange(cols)[None, :] % WINDOW
    d = np.arange(3 * WINDOW)[:, None] - t
    band = jnp.asarray(np.where((d >= 0) & (d <= 2 * WINDOW), 0.0, NEG), F32)

    def kspec(off, blk):
        return pl.BlockSpec((1, ATTN_KV_W, WINDOW), lambda b, j: (b, blk, jnp.clip(j + off, 0, nb - 1)))

    return pl.pallas_call(
        functools.partial(_attn_kernel, nb=nb),
        out_shape=jax.ShapeDtypeStruct((B, ATTN_Q_W, S), BF16),
        grid=(B, nb),
        in_specs=[pl.BlockSpec(memory_space=pltpu.SMEM),
                  pl.BlockSpec((1, ATTN_Q_W, WINDOW), lambda b, j: (b, q_blk, j)),
                  kspec(-1, k_blk), kspec(0, k_blk), kspec(1, k_blk),
                  kspec(-1, v_blk), kspec(0, v_blk), kspec(1, v_blk),
                  pl.BlockSpec((1, ATTN_KV_W, C), lambda b, j: (b, kx_blk, 0)),
                  pl.BlockSpec((1, ATTN_KV_W, C), lambda b, j: (b, vx_blk, 0)),
                  pl.BlockSpec((3 * WINDOW, cols), lambda b, j: (0, 0))],
        out_specs=pl.BlockSpec((1, ATTN_Q_W, WINDOW), lambda b, j: (b, 0, j)),
        compiler_params=_params(("parallel", "arbitrary")),
        name="window_attn",
    )(sink, pt_lat, pt_lat, pt_lat, pt_lat, pt_lat, pt_lat, pt_lat, pt_ctx, pt_ctx, band)


def _v_aug_t(vt):
    sub = lax.broadcasted_iota(jnp.int32, (BF16_SUBLANES, vt.shape[1]), 0)
    return jnp.concatenate([vt, jnp.where(sub == 0, 1.0, 0.0).astype(BF16)], axis=0)


def _state_update(ct_old, m_old, k, vaug_t, u_bc, r_end, b_end):
    m_end = jnp.maximum(m_old, r_end)
    ks = (k.astype(F32) * jnp.exp2(u_bc - m_end)).astype(BF16)
    ct_new = jnp.exp2(m_old - m_end) * ct_old + jnp.dot(vaug_t, ks, preferred_element_type=F32)
    return ct_new, b_end + m_end


def _row(row_ref, k, h, ts):
    i = k * MLSTM_HEADS + h
    return row_ref[0, i:i + 1, ts]


def _scan_kernel(k_ref, vt_ref, row_ref, col_ref, c0_ref, m0_ref, cs_ref, ms_ref, cf_ref, mf_ref,
                 c_scr, m_scr, *, reverse, nsteps, cps):
    H = MLSTM_HEADS
    n = pl.program_id(1)
    d = 1 if reverse else 0

    @pl.when(n == 0)
    def _():
        c_scr[...] = c0_ref[0]
        m_scr[...] = m0_ref[0]

    for c in (range(cps - 1, -1, -1) if reverse else range(cps)):
        ts = slice(c * CHUNK, (c + 1) * CHUNK)
        cols = col_ref[0, ts, :]
        for h in range(H):
            ct_old = c_scr[h]
            m_old = m_scr[h:h + 1, :]
            cs_ref[0, c, h] = ct_old.astype(BF16)
            ms_ref[0, c, h:h + 1, :] = m_old
            k = k_ref[0, ts, h * MLSTM_QK_DIM:(h + 1) * MLSTM_QK_DIM]
            vaug_t = _v_aug_t(vt_ref[0, h * MLSTM_V_DIM:(h + 1) * MLSTM_V_DIM, ts])
            u_bc = jnp.broadcast_to(cols[:, d * H + h:d * H + h + 1], (CHUNK, LANE))
            ct_new, m_new = _state_update(ct_old, m_old, k, vaug_t, u_bc, _row(row_ref, 4 + d, h, ts),
                                          _row(row_ref, 6 + d, h, ts))
            c_scr[h] = ct_new
            m_scr[h:h + 1, :] = m_new

    @pl.when(n == nsteps - 1)
    def _():
        cf_ref[0] = c_scr[...]
        mf_ref[0] = m_scr[...]


def _state_scan(p, k_blk, p_t, vt_blk, rows, cols, c0, m0, reverse, cps):
    B, T, _ = p.shape
    nc = T // CHUNK
    cps = min(cps, nc)
    ns = nc // cps
    assert ns * cps == nc
    H = MLSTM_HEADS
    tb = cps * CHUNK
    cidx = (lambda n: ns - 1 - n) if reverse else (lambda n: n)
    st = (H, C_ROWS, MLSTM_QK_DIM)
    return pl.pallas_call(
        functools.partial(_scan_kernel, reverse=reverse, nsteps=ns, cps=cps),
        out_shape=(jax.ShapeDtypeStruct((B, nc) + st, BF16),
                   jax.ShapeDtypeStruct((B, nc, H, LANE), F32),
                   jax.ShapeDtypeStruct((B,) + st, F32),
                   jax.ShapeDtypeStruct((B, H, LANE), F32)),
        grid=(B, ns),
        in_specs=[pl.BlockSpec((1, tb, MLSTM_QK_W), lambda b, n: (b, cidx(n), k_blk)),
                  pl.BlockSpec((1, MLSTM_V_W, tb), lambda b, n: (b, vt_blk, cidx(n))),
                  pl.BlockSpec((1, 8 * H, tb), lambda b, n: (b, 0, cidx(n))),
                  pl.BlockSpec((1, tb, LANE), lambda b, n: (b, cidx(n), 0)),
                  pl.BlockSpec((1,) + st, lambda b, n: (b, 0, 0, 0)),
                  pl.BlockSpec((1, H, LANE), lambda b, n: (b, 0, 0))],
        out_specs=(pl.BlockSpec((1, cps) + st, lambda b, n: (b, cidx(n), 0, 0, 0)),
                   pl.BlockSpec((1, cps, H, LANE), lambda b, n: (b, cidx(n), 0, 0)),
                   pl.BlockSpec((1,) + st, lambda b, n: (b, 0, 0, 0)),
                   pl.BlockSpec((1, H, LANE), lambda b, n: (b, 0, 0))),
        scratch_shapes=[pltpu.VMEM(st, F32), pltpu.VMEM((H, LANE), F32)],
        compiler_params=_params(("parallel", "arbitrary")),
        name="mlstm_scan_rev" if reverse else "mlstm_scan_fwd",
    )(p, p_t, rows, cols, c0, m0)


def _mlstm_out_kernel(k_ref, qt_ref, vt_ref, mot_ref, row_ref, col_ref, cr_ref, mr_ref, c0_ref, m0_ref, g_ref,
                      o_ref, c_scr, m_scr, *, cps):
    H = MLSTM_HEADS
    L = CHUNK
    n = pl.program_id(1)

    @pl.when(n == 0)
    def _():
        c_scr[...] = c0_ref[0]
        m_scr[...] = m0_ref[0]

    si = lax.broadcasted_iota(jnp.int32, (L, L), 0)
    ti = lax.broadcasted_iota(jnp.int32, (L, L), 1)
    for c in range(cps):
        ts = slice(c * L, (c + 1) * L)
        cols = col_ref[0, ts, :]
        for h in range(H):
            qt = qt_ref[0, h * MLSTM_QK_DIM:(h + 1) * MLSTM_QK_DIM, ts]
            k = k_ref[0, ts, h * MLSTM_QK_DIM:(h + 1) * MLSTM_QK_DIM]
            vs = slice(h * MLSTM_V_DIM, (h + 1) * MLSTM_V_DIM)
            vaug_t = _v_aug_t(vt_ref[0, vs, ts])
            pt = jnp.dot(k, qt, preferred_element_type=F32)
            qt_f = qt.astype(F32)
            ct_f = c_scr[h]
            hs_t = None
            for d in range(N_DIRS):
                u_bc = jnp.broadcast_to(cols[:, d * H + h:d * H + h + 1], (L, LANE))
                m0 = m_scr[h:h + 1, :] if d == 0 else mr_ref[0, c, h:h + 1, :]
                ct = ct_f.astype(BF16) if d == 0 else cr_ref[0, c, h]
                mx = jnp.maximum(_row(row_ref, d, h, ts), m0)
                valid = (si <= ti) if d == 0 else (si >= ti)
                s_t = (pt * jnp.exp2(jnp.where(valid, u_bc - mx, -jnp.inf))).astype(BF16)
                q_in = (qt_f * jnp.exp2(m0 - mx)).astype(BF16)
                tot = jnp.dot(jnp.concatenate([vaug_t, ct], axis=1), jnp.concatenate([s_t, q_in], axis=0),
                              preferred_element_type=F32)
                den = jnp.maximum(jnp.abs(tot[MLSTM_V_DIM:MLSTM_V_DIM + 1]),
                                  jnp.exp2(-_row(row_ref, 2 + d, h, ts) - mx))
                part = tot[:MLSTM_V_DIM] * (1.0 / den)
                hs_t = part if hs_t is None else hs_t + part
                if d == 0:
                    u_bc_f = u_bc
            hn = hs_t * lax.rsqrt(jnp.mean(hs_t * hs_t, axis=0, keepdims=True) + NORM_EPS) * g_ref[vs, :]
            o_ref[0, vs, ts] = (_sigmoid(mot_ref[0, vs, ts].astype(F32)) * hn).astype(BF16)
            ct_new, m_new = _state_update(ct_f, m_scr[h:h + 1, :], k, vaug_t, u_bc_f, _row(row_ref, 4, h, ts),
                                          _row(row_ref, 6, h, ts))
            c_scr[h] = ct_new
            m_scr[h:h + 1, :] = m_new


def _mlstm_out(p, k_blk, p_t, qt_blk, vt_blk, mot_blk, rows, cols, c_rev, m_rev, c0, m0, gain, cps):
    B, S, _ = p.shape
    nc = S // CHUNK
    ns = nc // cps
    assert ns * cps == nc
    H = MLSTM_HEADS
    tb = cps * CHUNK
    st = (H, C_ROWS, MLSTM_QK_DIM)
    gain_bc = jnp.broadcast_to(gain.reshape(MLSTM_V_W, 1), (MLSTM_V_W, LANE))
    return pl.pallas_call(
        functools.partial(_mlstm_out_kernel, cps=cps),
        out_shape=jax.ShapeDtypeStruct((B, MLSTM_V_W, S), BF16),
        grid=(B, ns),
        in_specs=[pl.BlockSpec((1, tb, MLSTM_QK_W), lambda b, n: (b, n, k_blk)),
                  pl.BlockSpec((1, MLSTM_QK_W, tb), lambda b, n: (b, qt_blk, n)),
                  pl.BlockSpec((1, MLSTM_V_W, tb), lambda b, n: (b, vt_blk, n)),
                  pl.BlockSpec((1, MLSTM_V_W, tb), lambda b, n: (b, mot_blk, n)),
                  pl.BlockSpec((1, 8 * H, tb), lambda b, n: (b, 0, n)),
                  pl.BlockSpec((1, tb, LANE), lambda b, n: (b, n, 0)),
                  pl.BlockSpec((1, cps) + st, lambda b, n: (b, n, 0, 0, 0)),
                  pl.BlockSpec((1, cps, H, LANE), lambda b, n: (b, n, 0, 0)),
                  pl.BlockSpec((1,) + st, lambda b, n: (b, 0, 0, 0)),
                  pl.BlockSpec((1, H, LANE), lambda b, n: (b, 0, 0)),
                  pl.BlockSpec((MLSTM_V_W, LANE), lambda b, n: (0, 0))],
        out_specs=pl.BlockSpec((1, MLSTM_V_W, tb), lambda b, n: (b, 0, n)),
        scratch_shapes=[pltpu.VMEM(st, F32), pltpu.VMEM((H, LANE), F32)],
        compiler_params=_params(("parallel", "arbitrary")),
        name="mlstm_out",
    )(p, p_t, p_t, p_t, rows, cols, c_rev, m_rev, c0, m0, gain_bc)


def _merge_kernel(attt_ref, memt_ref, ga_ref, gm_ref, x_ref, gate_ref, shift_ref, scale_ref, g2_ref,
                  wa_ref, wm_ref, wo_ref, o_ref, h_ref):
    tn_dims = (((0,), (0,)), ((), ()))
    a = lax.dot_general(attt_ref[0], wa_ref[...], tn_dims, preferred_element_type=F32)
    m = lax.dot_general(memt_ref[0], wm_ref[...], tn_dims, preferred_element_type=F32)
    y = _sigmoid(ga_ref[0].astype(F32)) * a + _sigmoid(gm_ref[0].astype(F32)) * m
    z = jnp.dot(y.astype(BF16), wo_ref[...], preferred_element_type=F32)
    x1 = x_ref[0] + gate_ref[0] * z
    o_ref[0] = x1
    ms = jnp.mean(x1 * x1, axis=-1, keepdims=True)
    h_ref[0] = (x1 * lax.rsqrt(ms + NORM_EPS) * g2_ref[...] * (1.0 + scale_ref[0]) + shift_ref[0]).astype(BF16)


def _merge(att_t, mem_t, p, ga_blk, gm_blk, x, gate1, shift2, scale2, norm2_g, w_ap, w_mp, w_out, tm):
    B, S, D = x.shape
    resident = functools.partial(pl.BlockSpec, pipeline_mode=pl.Buffered(1))
    mod = pl.BlockSpec((1, 1, D), lambda b, i: (b, 0, 0))
    return pl.pallas_call(
        _merge_kernel,
        out_shape=(jax.ShapeDtypeStruct((B, S, D), F32), jax.ShapeDtypeStruct((B, S, D), BF16)),
        grid=(B, S // tm),
        in_specs=[pl.BlockSpec((1, ATTN_Q_W, tm), lambda b, i: (b, 0, i)),
                  pl.BlockSpec((1, MLSTM_V_W, tm), lambda b, i: (b, 0, i)),
                  pl.BlockSpec((1, tm, D), lambda b, i: (b, i, ga_blk)),
                  pl.BlockSpec((1, tm, D), lambda b, i: (b, i, gm_blk)),
                  pl.BlockSpec((1, tm, D), lambda b, i: (b, i, 0)),
                  mod, mod, mod,
                  pl.BlockSpec((1, D), lambda b, i: (0, 0)),
                  resident(w_ap.shape, lambda b, i: (0, 0)),
                  resident(w_mp.shape, lambda b, i: (0, 0)),
                  resident(w_out.shape, lambda b, i: (0, 0))],
        out_specs=(pl.BlockSpec((1, tm, D), lambda b, i: (b, i, 0)),
                   pl.BlockSpec((1, tm, D), lambda b, i: (b, i, 0))),
        compiler_params=_params(("parallel", "parallel")),
        name="merge_outproj",
    )(att_t, mem_t, p, p, x, gate1, shift2, scale2, norm2_g.reshape(1, D), w_ap, w_mp, w_out)


def _ffn_kernel(x1_hbm, h_ref, gate_ref, wg_ref, wu_ref, wo_ref, gf_ref, o_ref, x1_buf, x1_sem, *, nf, tm):
    b, i, f = pl.program_id(0), pl.program_id(1), pl.program_id(2)
    rc = min(tm, NORM_ROWS)
    x1_copy = pltpu.make_async_copy(x1_hbm.at[b, pl.ds(pl.multiple_of(i * tm, tm), tm), :], x1_buf, x1_sem)

    @pl.when(f == 0)
    def _():
        o_ref[...] = jnp.zeros_like(o_ref)

    @pl.when(f == max(nf - 1 - X1_LOOKAHEAD, 0))
    def _():
        x1_copy.start()

    h = h_ref[0]
    gt = jnp.dot(h, wg_ref[...], preferred_element_type=F32)
    up = jnp.dot(h, wu_ref[...], preferred_element_type=F32)
    act = (gt * _sigmoid(gt) * up).astype(BF16)
    nc = wo_ref.shape[0]
    for c0 in range(0, o_ref.shape[2], nc):
        o_ref[0, :, c0:c0 + nc] += jnp.dot(act, wo_ref[:, c0:c0 + nc], preferred_element_type=F32)

    @pl.when(f == nf - 1)
    def _():
        x1_copy.wait()
        for r0 in range(0, tm, rc):
            x2 = x1_buf[r0:r0 + rc, :] + gate_ref[0] * o_ref[0, r0:r0 + rc, :]
            ms = jnp.mean(x2 * x2, axis=-1, keepdims=True)
            o_ref[0, r0:r0 + rc, :] = x2 * lax.rsqrt(ms + NORM_EPS) * gf_ref[...]


def _ffn(x1, h2, gate2, w_in, w_out, final_g, tm, tf):
    B, S, D = x1.shape
    dff = w_out.shape[0]
    nf = dff // tf
    return pl.pallas_call(
        functools.partial(_ffn_kernel, nf=nf, tm=tm),
        out_shape=jax.ShapeDtypeStruct((B, S, D), F32),
        grid=(B, S // tm, nf),
        in_specs=[pl.BlockSpec(memory_space=pl.ANY),
                  pl.BlockSpec((1, tm, D), lambda b, i, f: (b, i, 0)),
                  pl.BlockSpec((1, 1, D), lambda b, i, f: (b, 0, 0)),
                  pl.BlockSpec((D, tf), lambda b, i, f: (0, f)),
                  pl.BlockSpec((D, tf), lambda b, i, f: (0, nf + f)),
                  pl.BlockSpec((tf, D), lambda b, i, f: (f, 0)),
                  pl.BlockSpec((1, D), lambda b, i, f: (0, 0))],
        out_specs=pl.BlockSpec((1, tm, D), lambda b, i, f: (b, i, 0)),
        scratch_shapes=[pltpu.VMEM((tm, D), F32), pltpu.SemaphoreType.DMA(())],
        compiler_params=_params(("parallel", "parallel", "arbitrary")),
        name="ffn_final_norm",
    )(x1, h2, gate2, w_in, w_in, w_out, final_g.reshape(1, D))


def _rope_tables(S):
    pos = jnp.arange(S)
    rows = (pos // GRID_W).astype(F32)
    cols = (pos % GRID_W).astype(F32)
    inv_freq = ROPE_BASE ** (-jnp.arange(ROPE_PAIR, dtype=F32) / ROPE_PAIR)
    ar = rows[:, None] * inv_freq[None, :]
    ac = cols[:, None] * inv_freq[None, :]
    zero = jnp.zeros_like(ar)
    cos = jnp.concatenate([jnp.cos(ar), jnp.cos(ar), jnp.cos(ac), jnp.cos(ac)], axis=1)
    sin_lo = jnp.concatenate([-jnp.sin(ar), zero, -jnp.sin(ac), zero], axis=1)
    sin_hi = jnp.concatenate([zero, jnp.sin(ar), zero, jnp.sin(ac)], axis=1)
    return cos.T, sin_lo.T, sin_hi.T


def kernel(x, c, ctx, c_ctx, w_ada, b_ada, norm1_g, w_in, b_gates, attn_sink, mlstm_norm_g, w_attn_proj,
           w_mlstm_proj, w_out, norm2_g, w_ffn_in, w_ffn_out, final_norm_g):
    B, S, D = x.shape
    C = ctx.shape[1]
    assert w_ada.shape[0] == 1, "single-layer configuration"
    assert S % 512 == 0 and C % CHUNK == 0 and S % GRID_W == 0
    H = MLSTM_HEADS
    tn = 1024

    rows = -(-(B + 1) // 8) * 8
    cvecs = jnp.concatenate([c, c_ctx[None], jnp.zeros((rows - B - 1, D), F32)], axis=0)
    mod = _adaln(cvecs, w_ada[0], b_ada[0])
    shift1, scale1, gate1, shift2, scale2, gate2 = [mod[:B, k * D:(k + 1) * D].reshape(B, 1, D) for k in range(N_MOD)]
    shift_c = mod[B:B + 1, 0:D].reshape(1, 1, D)
    scale_c = mod[B:B + 1, D:2 * D].reshape(1, 1, D)

    wi = w_in[0]
    o = 0
    parts = {}
    for name, width in (("a_k", ATTN_KV_W), ("a_v", ATTN_KV_W), ("m_k", MLSTM_QK_W), ("m_v", MLSTM_V_W),
                        ("m_g", N_GATE), ("a_q", ATTN_Q_W), ("m_q", MLSTM_QK_W), ("m_o", MLSTM_V_W),
                        ("g_att", D), ("g_mem", D)):
        parts[name] = wi[:, o:o + width]
        o += width
    w_nat = jnp.concatenate([parts["g_att"], parts["g_mem"], parts["m_k"]], axis=1).astype(BF16)
    w_t = jnp.concatenate([parts["m_v"], parts["m_o"], parts["a_q"], parts["m_q"], parts["a_k"], parts["a_v"]],
                          axis=1).T.astype(BF16)
    w_g = parts["m_g"].T.astype(BF16)
    kinds_t = ([""] * (2 * MLSTM_V_W // LANE) + ["rope scale log2"] * (ATTN_Q_W // LANE)
               + ["scale"] * (MLSTM_QK_W // LANE) + ["rope"] * (ATTN_KV_W // LANE) + [""] * (ATTN_KV_W // LANE))
    assert (2 * D) % MLSTM_QK_W == 0 and (2 * MLSTM_V_W) % ATTN_Q_W == 0
    ga_blk, gm_blk, mk_blk = 0, 1, 2 * D // MLSTM_QK_W
    vt_blk, mot_blk = 0, 1
    aq_blk = 2 * MLSTM_V_W // ATTN_Q_W
    qt_blk = (2 * MLSTM_V_W + ATTN_Q_W) // MLSTM_QK_W
    kt_blk = (2 * MLSTM_V_W + ATTN_Q_W + MLSTM_QK_W) // ATTN_KV_W
    vtt_blk = kt_blk + 1
    w_nat_c = parts["m_k"].astype(BF16)
    w_t_c = jnp.concatenate([parts["m_v"], parts["a_k"], parts["a_v"]], axis=1).T.astype(BF16)
    kx_blk, vx_blk = MLSTM_V_W // ATTN_KV_W, MLSTM_V_W // ATTN_KV_W + 1

    p_lat, pt_lat, gt_lat = _inproj(x, shift1, scale1, norm1_g[0], w_nat, w_t, w_g, kinds_t,
                                    _rope_tables(S), tm=1024 if S % 1024 == 0 else 512, tn=tn)
    p_ctx, pt_ctx, gt_ctx = _inproj(ctx, shift_c, scale_c, norm1_g[0], w_nat_c, w_t_c, w_g,
                                    [""] * (w_t_c.shape[0] // LANE), None, tm=min(C, 256), tn=tn)

    att_t = _attention(pt_lat, pt_ctx, attn_sink[0], aq_blk, kt_blk, vtt_blk, kx_blk, vx_blk)

    rows_lat, cols_lat = _gate_prep(gt_lat, b_gates[0])
    rows_ctx, cols_ctx = _gate_prep(gt_ctx, b_gates[0])
    c_zero = jnp.zeros((B, H, C_ROWS, MLSTM_QK_DIM), F32)
    m_zero = jnp.zeros((B, H, LANE), F32)
    _, _, cf_ctx, mf_ctx = _state_scan(p_ctx, 0, pt_ctx, 0, rows_ctx, cols_ctx, c_zero, m_zero, reverse=False,
                                       cps=4)
    _, _, cr_ctx, mr_ctx = _state_scan(p_ctx, 0, pt_ctx, 0, rows_ctx, cols_ctx, c_zero, m_zero, reverse=True,
                                       cps=4)
    c_rev, m_rev, _, _ = _state_scan(p_lat, mk_blk, pt_lat, vt_blk, rows_lat, cols_lat, cr_ctx, mr_ctx,
                                     reverse=True, cps=4)
    mem_t = _mlstm_out(p_lat, mk_blk, pt_lat, qt_blk, vt_blk, mot_blk, rows_lat, cols_lat, c_rev, m_rev,
                       cf_ctx, mf_ctx, mlstm_norm_g[0], cps=2)

    x1, h2 = _merge(att_t, mem_t, p_lat, ga_blk, gm_blk, x, gate1, shift2, scale2, norm2_g[0],
                    w_attn_proj[0].astype(BF16), w_mlstm_proj[0].astype(BF16), w_out[0].astype(BF16), tm=256)
    return _ffn(x1, h2, gate2, w_ffn_in[0].astype(BF16), w_ffn_out[0].astype(BF16), final_norm_g,
                tm=1024 if S % 1024 == 0 else 512, tf=512)
```

```python
import functools

import numpy as np
import jax
import jax.numpy as jnp
from jax import lax
from jax.experimental import pallas as pl
from jax.experimental.pallas import tpu as pltpu

F32 = jnp.float32
BF16 = jnp.bfloat16

GRID_W = 64
ATTN_HEADS = 16
ATTN_KV_HEADS = 4
ATTN_GROUP = ATTN_HEADS // ATTN_KV_HEADS
HEAD_DIM = 128
WINDOW = 128
ROPE_BASE = 10000.0
MLSTM_HEADS = 8
MLSTM_QK_DIM = 128
MLSTM_V_DIM = 256
CHUNK = 128
N_DIRS = 2
N_GATE = N_DIRS * 2 * MLSTM_HEADS
NORM_EPS = 1e-6
N_MOD = 6
QK_SCALE = HEAD_DIM ** -0.5
LOG2E = 1.4426950408889634
ROPE_PAIR = HEAD_DIM // 4

LANE = 128
BF16_SUBLANES = 16
V7X_VMEM_BYTES = 64 * 1024 * 1024
VMEM_LIMIT = V7X_VMEM_BYTES - 1 * 1024 * 1024

ATTN_Q_W = ATTN_HEADS * HEAD_DIM
ATTN_KV_W = ATTN_KV_HEADS * HEAD_DIM
MLSTM_QK_W = MLSTM_HEADS * MLSTM_QK_DIM
MLSTM_V_W = MLSTM_HEADS * MLSTM_V_DIM
C_ROWS = MLSTM_V_DIM + BF16_SUBLANES
NORM_ROWS = 128
X1_LOOKAHEAD = 3

NEG = -1e30


def _params(sem):
    return pltpu.CompilerParams(dimension_semantics=sem, vmem_limit_bytes=VMEM_LIMIT)


def _sigmoid(x):
    return 1.0 / (1.0 + jnp.exp(-x))


def _adaln_kernel(c_ref, w_ref, b_ref, o_ref):
    cc = c_ref[...]
    s = (cc * _sigmoid(cc)).astype(BF16)
    o_ref[...] = jnp.dot(s, w_ref[...].astype(BF16), preferred_element_type=F32) + b_ref[...]


def _adaln(cvecs, w, b):
    R, D = cvecs.shape
    N = w.shape[1]
    tn = 1024
    return pl.pallas_call(
        _adaln_kernel,
        out_shape=jax.ShapeDtypeStruct((R, N), F32),
        grid=(N // tn,),
        in_specs=[pl.BlockSpec((R, D), lambda j: (0, 0)),
                  pl.BlockSpec((D, tn), lambda j: (0, j)),
                  pl.BlockSpec((1, tn), lambda j: (0, j))],
        out_specs=pl.BlockSpec((R, tn), lambda j: (0, j)),
        compiler_params=_params(("arbitrary",)),
        name="adaln",
    )(cvecs, w, b.reshape(1, N))


def _tile_groups(kinds, per):
    tiles = [tuple(kinds[t * per:(t + 1) * per]) for t in range(len(kinds) // per)]
    groups = []
    for t, tk in enumerate(tiles):
        if groups and groups[-1][2] == tk:
            groups[-1] = (groups[-1][0], t + 1, tk)
        else:
            groups.append((t, t + 1, tk))
    return tuple(groups)


def _inproj_kernel(*refs, t_groups, nn, rope):
    if rope:
        (x_ref, shift_ref, scale_ref, g_ref, wn_ref, wt_ref, wg_ref, cos_ref, sin_lo_ref, sin_hi_ref,
         p_ref, pt_ref, gt_ref, h_scr) = refs
    else:
        x_ref, shift_ref, scale_ref, g_ref, wn_ref, wt_ref, wg_ref, p_ref, pt_ref, gt_ref, h_scr = refs
    j = pl.program_id(2)
    nt_dims = (((1,), (1,)), ((), ()))

    @pl.when(j == 0)
    def _():
        tm = h_scr.shape[0]
        rc = min(tm, NORM_ROWS)
        for r0 in range(0, tm, rc):
            xf = x_ref[0, r0:r0 + rc, :]
            ms = jnp.mean(xf * xf, axis=-1, keepdims=True)
            y = xf * lax.rsqrt(ms + NORM_EPS) * g_ref[...]
            h_scr[r0:r0 + rc, :] = (y * (1.0 + scale_ref[0]) + shift_ref[0]).astype(BF16)
        gt_ref[0] = lax.dot_general(wg_ref[...], h_scr[...], nt_dims, preferred_element_type=F32)

    @pl.when(j < nn)
    def _():
        p_ref[0] = jnp.dot(h_scr[...], wn_ref[...], preferred_element_type=F32).astype(BF16)

    for lo, hi, kinds in t_groups:
        @pl.when((j >= nn + lo) & (j < nn + hi))
        def _(kinds=kinds):
            acc = lax.dot_general(wt_ref[...], h_scr[...], nt_dims, preferred_element_type=F32)
            for u, kind in enumerate(kinds):
                a = acc[u * LANE:(u + 1) * LANE]
                if "rope" in kind:
                    a = (a * cos_ref[...] + pltpu.roll(a, HEAD_DIM - ROPE_PAIR, 0) * sin_lo_ref[...]
                         + pltpu.roll(a, ROPE_PAIR, 0) * sin_hi_ref[...])
                if "scale" in kind:
                    a = a * (QK_SCALE * LOG2E if "log2" in kind else QK_SCALE)
                pt_ref[0, u * LANE:(u + 1) * LANE, :] = a.astype(BF16)


def _inproj(x, shift, scale, gain, w_nat, w_t, w_g, kinds_t, rope_tabs, tm, tn):
    B, T, D = x.shape
    n_nat = w_nat.shape[1]
    n_t = w_t.shape[0]
    nn, ntt = n_nat // tn, n_t // tn
    per = tn // LANE
    n_i = T // tm
    bm = shift.shape[0]
    rope = rope_tabs is not None

    def next_tile(b, i, j):
        t = jnp.minimum(b * n_i + i + (j >= 1), B * n_i - 1)
        return t // n_i, t % n_i

    def mod_map(b, i, j):
        return (next_tile(b, i, j)[0] if bm == B else 0, 0, 0)

    in_specs = [pl.BlockSpec((1, tm, D), lambda b, i, j: next_tile(b, i, j) + (0,)),
                pl.BlockSpec((1, 1, D), mod_map),
                pl.BlockSpec((1, 1, D), mod_map),
                pl.BlockSpec((1, D), lambda b, i, j: (0, 0)),
                pl.BlockSpec((D, tn), lambda b, i, j: (0, jnp.minimum(j, nn - 1))),
                pl.BlockSpec((tn, D), lambda b, i, j: (jnp.where(j == 0, ntt - 1, jnp.maximum(j - nn, 0)), 0)),
                pl.BlockSpec(w_g.shape, lambda b, i, j: (0, 0))]
    args = [x, shift, scale, gain.reshape(1, D), w_nat, w_t, w_g]
    if rope:
        in_specs += [pl.BlockSpec((HEAD_DIM, tm), lambda b, i, j: (0, i))] * len(rope_tabs)
        args += list(rope_tabs)
    return pl.pallas_call(
        functools.partial(_inproj_kernel, t_groups=_tile_groups(kinds_t, per), nn=nn, rope=rope),
        out_shape=(jax.ShapeDtypeStruct((B, T, n_nat), BF16),
                   jax.ShapeDtypeStruct((B, n_t, T), BF16),
                   jax.ShapeDtypeStruct((B, w_g.shape[0], T), F32)),
        grid=(B, T // tm, nn + ntt),
        in_specs=in_specs,
        out_specs=(pl.BlockSpec((1, tm, tn), lambda b, i, j: (b, i, jnp.minimum(j, nn - 1))),
                   pl.BlockSpec((1, tn, tm), lambda b, i, j: (b, jnp.maximum(j - nn, 0), i)),
                   pl.BlockSpec((1, w_g.shape[0], tm), lambda b, i, j: (b, 0, i))),
        scratch_shapes=[pltpu.VMEM((tm, D), BF16)],
        compiler_params=_params(("arbitrary", "arbitrary", "arbitrary")),
        name="inproj_rope" if rope else "inproj_ctx",
    )(*args)


def _scan_lanes(x, op, reverse, fill):
    lane = lax.broadcasted_iota(jnp.int32, x.shape, 1)
    k = 1
    while k < CHUNK:
        if reverse:
            sh = jnp.where(lane < CHUNK - k, pltpu.roll(x, CHUNK - k, 1), fill)
        else:
            sh = jnp.where(lane >= k, pltpu.roll(x, k, 1), fill)
        x = op(x, sh)
        k *= 2
    return x


def _log_sigmoid(z):
    return jnp.minimum(z, 0.0) - jnp.log(1.0 + jnp.exp(-jnp.abs(z)))


def _lane_value(x, lane_idx):
    lane = lax.broadcasted_iota(jnp.int32, x.shape, 1)
    return jnp.broadcast_to(jnp.sum(jnp.where(lane == lane_idx, x, 0.0), axis=1, keepdims=True), x.shape)


def _gate_prep_kernel(gt_ref, bias_ref, row_ref, col_ref, *, nchunk):
    H = MLSTM_HEADS
    for c in range(nchunk):
        sl = slice(c * CHUNK, (c + 1) * CHUNK)
        z = gt_ref[0, :, sl] + bias_ref[...]
        li_f, lf_f = z[0:H] * LOG2E, _log_sigmoid(z[H:2 * H]) * LOG2E
        li_r, lf_r = z[2 * H:3 * H] * LOG2E, _log_sigmoid(z[3 * H:4 * H]) * LOG2E
        b_f = _scan_lanes(lf_f, jnp.add, False, 0.0)
        b_r = _scan_lanes(lf_r, jnp.add, True, 0.0)
        u_f = li_f - b_f
        u_r = li_r - b_r
        r_f = _scan_lanes(u_f, jnp.maximum, False, -jnp.inf)
        r_r = _scan_lanes(u_r, jnp.maximum, True, -jnp.inf)
        ends = [_lane_value(r_f, CHUNK - 1), _lane_value(r_r, 0), _lane_value(b_f, CHUNK - 1), _lane_value(b_r, 0)]
        for k, v in enumerate([r_f, r_r, b_f, b_r] + ends):
            row_ref[0, k * H:(k + 1) * H, sl] = v
        stack = jnp.concatenate([u_f, u_r, jnp.zeros((LANE - 2 * H, CHUNK), F32)], axis=0)
        col_ref[0, sl, :] = stack.T


def _gate_prep(g_t, bias):
    B, G, T = g_t.shape
    tg = min(T, 8 * CHUNK)
    H = MLSTM_HEADS
    return pl.pallas_call(
        functools.partial(_gate_prep_kernel, nchunk=tg // CHUNK),
        out_shape=(jax.ShapeDtypeStruct((B, 8 * H, T), F32),
                   jax.ShapeDtypeStruct((B, T, LANE), F32)),
        grid=(B, T // tg),
        in_specs=[pl.BlockSpec((1, G, tg), lambda b, i: (b, 0, i)),
                  pl.BlockSpec((G, CHUNK), lambda b, i: (0, 0))],
        out_specs=(pl.BlockSpec((1, 8 * H, tg), lambda b, i: (b, 0, i)),
                   pl.BlockSpec((1, tg, LANE), lambda b, i: (b, i, 0))),
        compiler_params=_params(("parallel", "parallel")),
        name="gate_prep",
    )(g_t, jnp.broadcast_to(bias.reshape(G, 1), (G, CHUNK)))


def _attn_kernel(*refs, nb, qb):
    sink_ref, qt_ref = refs[:2]
    k_refs = refs[2:qb + 4]
    v_refs = refs[qb + 4:2 * qb + 6]
    kx_ref, vx_ref, band_ref, o_ref = refs[2 * qb + 6:]
    cols = ATTN_GROUP * WINDOW
    row = lax.broadcasted_iota(jnp.int32, (3 * WINDOW, WINDOW), 0)
    lane = lax.broadcasted_iota(jnp.int32, (1, cols), 1)
    tn_dims = (((0,), (0,)), ((), ()))
    for q in range(qb):
        j = pl.program_id(1) * qb + q
        qs = slice(q * WINDOW, (q + 1) * WINDOW)
        prev_bias = jnp.where(j > 0, 0.0, NEG)
        next_bias = jnp.where(j < nb - 1, 0.0, NEG)
        edge = jnp.where(row < WINDOW, prev_bias, jnp.where(row >= 2 * WINDOW, next_bias, 0.0))
        bias = band_ref[...] + jnp.concatenate([edge] * ATTN_GROUP, axis=1)
        for g in range(ATTN_KV_HEADS):
            hs = slice(g * HEAD_DIM, (g + 1) * HEAD_DIM)
            qt = jnp.concatenate([qt_ref[0, (g * ATTN_GROUP + h) * HEAD_DIM:(g * ATTN_GROUP + h + 1) * HEAD_DIM, qs]
                                  for h in range(ATTN_GROUP)], axis=1)
            kt = jnp.concatenate([r[0, hs, :] for r in k_refs[q:q + 3]] + [kx_ref[0, hs, :]], axis=1)
            vt = jnp.concatenate([r[0, hs, :] for r in v_refs[q:q + 3]] + [vx_ref[0, hs, :]], axis=1)
            st = lax.dot_general(kt, qt, tn_dims, preferred_element_type=F32)
            s_loc = st[:3 * WINDOW] + bias
            s_ctx = st[3 * WINDOW:]
            sink = jnp.full((1, cols), sink_ref[g * ATTN_GROUP] * LOG2E, F32)
            for h in range(1, ATTN_GROUP):
                sink = jnp.where(lane >= h * WINDOW, sink_ref[g * ATTN_GROUP + h] * LOG2E, sink)
            m = jnp.maximum(jnp.maximum(jnp.max(s_loc, axis=0, keepdims=True),
                                        jnp.max(s_ctx, axis=0, keepdims=True)), sink)
            p_loc = jnp.exp2(s_loc - m)
            p_ctx = jnp.exp2(s_ctx - m)
            den = (jnp.sum(p_loc, axis=0, keepdims=True) + jnp.sum(p_ctx, axis=0, keepdims=True)
                   + jnp.exp2(sink - m))
            pt = jnp.concatenate([p_loc, p_ctx], axis=0).astype(BF16)
            ot = jnp.dot(vt, pt, preferred_element_type=F32) * (1.0 / den)
            for h in range(ATTN_GROUP):
                r0 = (g * ATTN_GROUP + h) * HEAD_DIM
                o_ref[0, r0:r0 + HEAD_DIM, qs] = ot[:, h * WINDOW:(h + 1) * WINDOW].astype(BF16)


def _attention(pt_lat, pt_ctx, sink, q_blk, k_blk, v_blk, kx_blk, vx_blk, qb):
    B, _, S = pt_lat.shape
    C = pt_ctx.shape[2]
    nb = S // WINDOW
    assert nb % qb == 0
    cols = ATTN_GROUP * WINDOW
    t = np.arange(cols)[None, :] % WINDOW
    d = np.arange(3 * WINDOW)[:, None] - t
    band = jnp.asarray(np.where((d >= 0) & (d <= 2 * WINDOW), 0.0, NEG), F32)

    def kspec(off, blk):
        return pl.BlockSpec((1, ATTN_KV_W, WINDOW), lambda b, j: (b, blk, jnp.clip(j * qb + off, 0, nb - 1)))

    offs = range(-1, qb + 1)
    return pl.pallas_call(
        functools.partial(_attn_kernel, nb=nb, qb=qb),
        out_shape=jax.ShapeDtypeStruct((B, ATTN_Q_W, S), BF16),
        grid=(B, nb // qb),
        in_specs=[pl.BlockSpec(memory_space=pltpu.SMEM),
                  pl.BlockSpec((1, ATTN_Q_W, qb * WINDOW), lambda b, j: (b, q_blk, j))]
                 + [kspec(o, k_blk) for o in offs] + [kspec(o, v_blk) for o in offs]
                 + [pl.BlockSpec((1, ATTN_KV_W, C), lambda b, j: (b, kx_blk, 0)),
                    pl.BlockSpec((1, ATTN_KV_W, C), lambda b, j: (b, vx_blk, 0)),
                    pl.BlockSpec((3 * WINDOW, cols), lambda b, j: (0, 0))],
        out_specs=pl.BlockSpec((1, ATTN_Q_W, qb * WINDOW), lambda b, j: (b, 0, j)),
        compiler_params=_params(("parallel", "arbitrary")),
        name="window_attn",
    )(sink, pt_lat, *([pt_lat] * (2 * len(offs))), pt_ctx, pt_ctx, band)


def _v_aug_t(vt):
    sub = lax.broadcasted_iota(jnp.int32, (BF16_SUBLANES, vt.shape[1]), 0)
    return jnp.concatenate([vt, jnp.where(sub == 0, 1.0, 0.0).astype(BF16)], axis=0)


def _state_update(ct_old, m_old, k, vaug_t, u_bc, r_end, b_end):
    m_end = jnp.maximum(m_old, r_end)
    ks = (k.astype(F32) * jnp.exp2(u_bc - m_end)).astype(BF16)
    ct_new = jnp.exp2(m_old - m_end) * ct_old + jnp.dot(vaug_t, ks, preferred_element_type=F32)
    return ct_new, b_end + m_end


def _row(row_ref, k, h, ts):
    i = k * MLSTM_HEADS + h
    return row_ref[0, i:i + 1, ts]


def _scan_kernel(k_ref, vt_ref, row_ref, col_ref, c0_ref, m0_ref, cs_ref, ms_ref, cf_ref, mf_ref,
                 c_scr, m_scr, *, reverse, nsteps, cps):
    H = MLSTM_HEADS
    n = pl.program_id(1)
    d = 1 if reverse else 0

    @pl.when(n == 0)
    def _():
        c_scr[...] = c0_ref[0]
        m_scr[...] = m0_ref[0]

    for c in (range(cps - 1, -1, -1) if reverse else range(cps)):
        ts = slice(c * CHUNK, (c + 1) * CHUNK)
        cols = col_ref[0, ts, :]
        for h in range(H):
            ct_old = c_scr[h]
            m_old = m_scr[h:h + 1, :]
            cs_ref[0, c, h] = ct_old.astype(BF16)
            ms_ref[0, c, h:h + 1, :] = m_old
            k = k_ref[0, ts, h * MLSTM_QK_DIM:(h + 1) * MLSTM_QK_DIM]
            vaug_t = _v_aug_t(vt_ref[0, h * MLSTM_V_DIM:(h + 1) * MLSTM_V_DIM, ts])
            u_bc = jnp.broadcast_to(cols[:, d * H + h:d * H + h + 1], (CHUNK, LANE))
            ct_new, m_new = _state_update(ct_old, m_old, k, vaug_t, u_bc, _row(row_ref, 4 + d, h, ts),
                                          _row(row_ref, 6 + d, h, ts))
            c_scr[h] = ct_new
            m_scr[h:h + 1, :] = m_new

    @pl.when(n == nsteps - 1)
    def _():
        cf_ref[0] = c_scr[...]
        mf_ref[0] = m_scr[...]


def _state_scan(p, k_blk, p_t, vt_blk, rows, cols, c0, m0, reverse, cps):
    B, T, _ = p.shape
    nc = T // CHUNK
    cps = min(cps, nc)
    ns = nc // cps
    assert ns * cps == nc
    H = MLSTM_HEADS
    tb = cps * CHUNK
    cidx = (lambda n: ns - 1 - n) if reverse else (lambda n: n)
    st = (H, C_ROWS, MLSTM_QK_DIM)
    return pl.pallas_call(
        functools.partial(_scan_kernel, reverse=reverse, nsteps=ns, cps=cps),
        out_shape=(jax.ShapeDtypeStruct((B, nc) + st, BF16),
                   jax.ShapeDtypeStruct((B, nc, H, LANE), F32),
                   jax.ShapeDtypeStruct((B,) + st, F32),
                   jax.ShapeDtypeStruct((B, H, LANE), F32)),
        grid=(B, ns),
        in_specs=[pl.BlockSpec((1, tb, MLSTM_QK_W), lambda b, n: (b, cidx(n), k_blk)),
                  pl.BlockSpec((1, MLSTM_V_W, tb), lambda b, n: (b, vt_blk, cidx(n))),
                  pl.BlockSpec((1, 8 * H, tb), lambda b, n: (b, 0, cidx(n))),
                  pl.BlockSpec((1, tb, LANE), lambda b, n: (b, cidx(n), 0)),
                  pl.BlockSpec((1,) + st, lambda b, n: (b, 0, 0, 0)),
                  pl.BlockSpec((1, H, LANE), lambda b, n: (b, 0, 0))],
        out_specs=(pl.BlockSpec((1, cps) + st, lambda b, n: (b, cidx(n), 0, 0, 0)),
                   pl.BlockSpec((1, cps, H, LANE), lambda b, n: (b, cidx(n), 0, 0)),
                   pl.BlockSpec((1,) + st, lambda b, n: (b, 0, 0, 0)),
                   pl.BlockSpec((1, H, LANE), lambda b, n: (b, 0, 0))),
        scratch_shapes=[pltpu.VMEM(st, F32), pltpu.VMEM((H, LANE), F32)],
        compiler_params=_params(("parallel", "arbitrary")),
        name="mlstm_scan_rev" if reverse else "mlstm_scan_fwd",
    )(p, p_t, rows, cols, c0, m0)


def _mlstm_out_kernel(k_ref, qt_ref, vt_ref, mot_ref, row_ref, col_ref, cr_ref, mr_ref, c0_ref, m0_ref, g_ref,
                      o_ref, c_scr, m_scr, *, cps):
    H = MLSTM_HEADS
    L = CHUNK
    n = pl.program_id(1)

    @pl.when(n == 0)
    def _():
        c_scr[...] = c0_ref[0]
        m_scr[...] = m0_ref[0]

    si = lax.broadcasted_iota(jnp.int32, (L, L), 0)
    ti = lax.broadcasted_iota(jnp.int32, (L, L), 1)
    for c in range(cps):
        ts = slice(c * L, (c + 1) * L)
        cols = col_ref[0, ts, :]
        for h in range(H):
            qt = qt_ref[0, h * MLSTM_QK_DIM:(h + 1) * MLSTM_QK_DIM, ts]
            k = k_ref[0, ts, h * MLSTM_QK_DIM:(h + 1) * MLSTM_QK_DIM]
            vs = slice(h * MLSTM_V_DIM, (h + 1) * MLSTM_V_DIM)
            vaug_t = _v_aug_t(vt_ref[0, vs, ts])
            pt = jnp.dot(k, qt, preferred_element_type=F32)
            qt_f = qt.astype(F32)
            ct_f = c_scr[h]
            hs_t = None
            for d in range(N_DIRS):
                u_bc = jnp.broadcast_to(cols[:, d * H + h:d * H + h + 1], (L, LANE))
                m0 = m_scr[h:h + 1, :] if d == 0 else mr_ref[0, c, h:h + 1, :]
                ct = ct_f.astype(BF16) if d == 0 else cr_ref[0, c, h]
                mx = jnp.maximum(_row(row_ref, d, h, ts), m0)
                valid = (si <= ti) if d == 0 else (si >= ti)
                s_t = (pt * jnp.exp2(jnp.where(valid, u_bc - mx, -jnp.inf))).astype(BF16)
                q_in = (qt_f * jnp.exp2(m0 - mx)).astype(BF16)
                tot = jnp.dot(jnp.concatenate([vaug_t, ct], axis=1), jnp.concatenate([s_t, q_in], axis=0),
                              preferred_element_type=F32)
                den = jnp.maximum(jnp.abs(tot[MLSTM_V_DIM:MLSTM_V_DIM + 1]),
                                  jnp.exp2(-_row(row_ref, 2 + d, h, ts) - mx))
                part = tot[:MLSTM_V_DIM] * (1.0 / den)
                hs_t = part if hs_t is None else hs_t + part
                if d == 0:
                    u_bc_f = u_bc
            hn = hs_t * lax.rsqrt(jnp.mean(hs_t * hs_t, axis=0, keepdims=True) + NORM_EPS) * g_ref[vs, :]
            o_ref[0, vs, ts] = (_sigmoid(mot_ref[0, vs, ts].astype(F32)) * hn).astype(BF16)
            ct_new, m_new = _state_update(ct_f, m_scr[h:h + 1, :], k, vaug_t, u_bc_f, _row(row_ref, 4, h, ts),
                                          _row(row_ref, 6, h, ts))
            c_scr[h] = ct_new
            m_scr[h:h + 1, :] = m_new


def _mlstm_out(p, k_blk, p_t, qt_blk, vt_blk, mot_blk, rows, cols, c_rev, m_rev, c0, m0, gain, cps):
    B, S, _ = p.shape
    nc = S // CHUNK
    ns = nc // cps
    assert ns * cps == nc
    H = MLSTM_HEADS
    tb = cps * CHUNK
    st = (H, C_ROWS, MLSTM_QK_DIM)
    gain_bc = jnp.broadcast_to(gain.reshape(MLSTM_V_W, 1), (MLSTM_V_W, LANE))
    return pl.pallas_call(
        functools.partial(_mlstm_out_kernel, cps=cps),
        out_shape=jax.ShapeDtypeStruct((B, MLSTM_V_W, S), BF16),
        grid=(B, ns),
        in_specs=[pl.BlockSpec((1, tb, MLSTM_QK_W), lambda b, n: (b, n, k_blk)),
                  pl.BlockSpec((1, MLSTM_QK_W, tb), lambda b, n: (b, qt_blk, n)),
                  pl.BlockSpec((1, MLSTM_V_W, tb), lambda b, n: (b, vt_blk, n)),
                  pl.BlockSpec((1, MLSTM_V_W, tb), lambda b, n: (b, mot_blk, n)),
                  pl.BlockSpec((1, 8 * H, tb), lambda b, n: (b, 0, n)),
                  pl.BlockSpec((1, tb, LANE), lambda b, n: (b, n, 0)),
                  pl.BlockSpec((1, cps) + st, lambda b, n: (b, n, 0, 0, 0)),
                  pl.BlockSpec((1, cps, H, LANE), lambda b, n: (b, n, 0, 0)),
                  pl.BlockSpec((1,) + st, lambda b, n: (b, 0, 0, 0)),
                  pl.BlockSpec((1, H, LANE), lambda b, n: (b, 0, 0)),
                  pl.BlockSpec((MLSTM_V_W, LANE), lambda b, n: (0, 0))],
        out_specs=pl.BlockSpec((1, MLSTM_V_W, tb), lambda b, n: (b, 0, n)),
        scratch_shapes=[pltpu.VMEM(st, F32), pltpu.VMEM((H, LANE), F32)],
        compiler_params=_params(("parallel", "arbitrary")),
        name="mlstm_out",
    )(p, p_t, p_t, p_t, rows, cols, c_rev, m_rev, c0, m0, gain_bc)


def _merge_kernel(attt_ref, memt_ref, ga_ref, gm_ref, x_ref, gate_ref, shift_ref, scale_ref, g2_ref,
                  wa_ref, wm_ref, wo_ref, o_ref, h_ref):
    tn_dims = (((0,), (0,)), ((), ()))
    a = lax.dot_general(attt_ref[0], wa_ref[...], tn_dims, preferred_element_type=F32)
    m = lax.dot_general(memt_ref[0], wm_ref[...], tn_dims, preferred_element_type=F32)
    y = _sigmoid(ga_ref[0].astype(F32)) * a + _sigmoid(gm_ref[0].astype(F32)) * m
    z = jnp.dot(y.astype(BF16), wo_ref[...], preferred_element_type=F32)
    x1 = x_ref[0] + gate_ref[0] * z
    o_ref[0] = x1
    ms = jnp.mean(x1 * x1, axis=-1, keepdims=True)
    h_ref[0] = (x1 * lax.rsqrt(ms + NORM_EPS) * g2_ref[...] * (1.0 + scale_ref[0]) + shift_ref[0]).astype(BF16)


def _merge(att_t, mem_t, p, ga_blk, gm_blk, x, gate1, shift2, scale2, norm2_g, w_ap, w_mp, w_out, tm):
    B, S, D = x.shape
    resident = functools.partial(pl.BlockSpec, pipeline_mode=pl.Buffered(1))
    mod = pl.BlockSpec((1, 1, D), lambda b, i: (b, 0, 0))
    return pl.pallas_call(
        _merge_kernel,
        out_shape=(jax.ShapeDtypeStruct((B, S, D), F32), jax.ShapeDtypeStruct((B, S, D), BF16)),
        grid=(B, S // tm),
        in_specs=[pl.BlockSpec((1, ATTN_Q_W, tm), lambda b, i: (b, 0, i)),
                  pl.BlockSpec((1, MLSTM_V_W, tm), lambda b, i: (b, 0, i)),
                  pl.BlockSpec((1, tm, D), lambda b, i: (b, i, ga_blk)),
                  pl.BlockSpec((1, tm, D), lambda b, i: (b, i, gm_blk)),
                  pl.BlockSpec((1, tm, D), lambda b, i: (b, i, 0)),
                  mod, mod, mod,
                  pl.BlockSpec((1, D), lambda b, i: (0, 0)),
                  resident(w_ap.shape, lambda b, i: (0, 0)),
                  resident(w_mp.shape, lambda b, i: (0, 0)),
                  resident(w_out.shape, lambda b, i: (0, 0))],
        out_specs=(pl.BlockSpec((1, tm, D), lambda b, i: (b, i, 0)),
                   pl.BlockSpec((1, tm, D), lambda b, i: (b, i, 0))),
        compiler_params=_params(("parallel", "parallel")),
        name="merge_outproj",
    )(att_t, mem_t, p, p, x, gate1, shift2, scale2, norm2_g.reshape(1, D), w_ap, w_mp, w_out)


def _ffn_kernel(x1_hbm, h_ref, gate_ref, wg_ref, wu_ref, wo_ref, gf_ref, o_ref, x1_buf, x1_sem, *, nf, tm):
    b, i, f = pl.program_id(0), pl.program_id(1), pl.program_id(2)
    rc = min(tm, NORM_ROWS)
    x1_copy = pltpu.make_async_copy(x1_hbm.at[b, pl.ds(pl.multiple_of(i * tm, tm), tm), :], x1_buf, x1_sem)

    @pl.when(f == 0)
    def _():
        o_ref[...] = jnp.zeros_like(o_ref)

    @pl.when(f == max(nf - 1 - X1_LOOKAHEAD, 0))
    def _():
        x1_copy.start()

    h = h_ref[0]
    gt = jnp.dot(h, wg_ref[...], preferred_element_type=F32)
    up = jnp.dot(h, wu_ref[...], preferred_element_type=F32)
    act = (gt * _sigmoid(gt) * up).astype(BF16)
    nc = wo_ref.shape[0]
    for c0 in range(0, o_ref.shape[2], nc):
        o_ref[0, :, c0:c0 + nc] += jnp.dot(act, wo_ref[:, c0:c0 + nc], preferred_element_type=F32)

    @pl.when(f == nf - 1)
    def _():
        x1_copy.wait()
        for r0 in range(0, tm, rc):
            x2 = x1_buf[r0:r0 + rc, :] + gate_ref[0] * o_ref[0, r0:r0 + rc, :]
            ms = jnp.mean(x2 * x2, axis=-1, keepdims=True)
            o_ref[0, r0:r0 + rc, :] = x2 * lax.rsqrt(ms + NORM_EPS) * gf_ref[...]


def _ffn(x1, h2, gate2, w_in, w_out, final_g, tm, tf):
    B, S, D = x1.shape
    dff = w_out.shape[0]
    nf = dff // tf
    return pl.pallas_call(
        functools.partial(_ffn_kernel, nf=nf, tm=tm),
        out_shape=jax.ShapeDtypeStruct((B, S, D), F32),
        grid=(B, S // tm, nf),
        in_specs=[pl.BlockSpec(memory_space=pl.ANY),
                  pl.BlockSpec((1, tm, D), lambda b, i, f: (b, i, 0)),
                  pl.BlockSpec((1, 1, D), lambda b, i, f: (b, 0, 0)),
                  pl.BlockSpec((D, tf), lambda b, i, f: (0, f)),
                  pl.BlockSpec((D, tf), lambda b, i, f: (0, nf + f)),
                  pl.BlockSpec((tf, D), lambda b, i, f: (f, 0)),
                  pl.BlockSpec((1, D), lambda b, i, f: (0, 0))],
        out_specs=pl.BlockSpec((1, tm, D), lambda b, i, f: (b, i, 0)),
        scratch_shapes=[pltpu.VMEM((tm, D), F32), pltpu.SemaphoreType.DMA(())],
        compiler_params=_params(("parallel", "parallel", "arbitrary")),
        name="ffn_final_norm",
    )(x1, h2, gate2, w_in, w_in, w_out, final_g.reshape(1, D))


def _rope_tables(S):
    pos = jnp.arange(S)
    rows = (pos // GRID_W).astype(F32)
    cols = (pos % GRID_W).astype(F32)
    inv_freq = ROPE_BASE ** (-jnp.arange(ROPE_PAIR, dtype=F32) / ROPE_PAIR)
    ar = rows[:, None] * inv_freq[None, :]
    ac = cols[:, None] * inv_freq[None, :]
    zero = jnp.zeros_like(ar)
    cos = jnp.concatenate([jnp.cos(ar), jnp.cos(ar), jnp.cos(ac), jnp.cos(ac)], axis=1)
    sin_lo = jnp.concatenate([-jnp.sin(ar), zero, -jnp.sin(ac), zero], axis=1)
    sin_hi = jnp.concatenate([zero, jnp.sin(ar), zero, jnp.sin(ac)], axis=1)
    return cos.T, sin_lo.T, sin_hi.T


def kernel(x, c, ctx, c_ctx, w_ada, b_ada, norm1_g, w_in, b_gates, attn_sink, mlstm_norm_g, w_attn_proj,
           w_mlstm_proj, w_out, norm2_g, w_ffn_in, w_ffn_out, final_norm_g):
    B, S, D = x.shape
    C = ctx.shape[1]
    assert w_ada.shape[0] == 1, "single-layer configuration"
    assert S % 512 == 0 and C % CHUNK == 0 and S % GRID_W == 0
    H = MLSTM_HEADS
    tn = 1024

    rows = -(-(B + 1) // 8) * 8
    cvecs = jnp.concatenate([c, c_ctx[None], jnp.zeros((rows - B - 1, D), F32)], axis=0)
    mod = _adaln(cvecs, w_ada[0], b_ada[0])
    shift1, scale1, gate1, shift2, scale2, gate2 = [mod[:B, k * D:(k + 1) * D].reshape(B, 1, D) for k in range(N_MOD)]
    shift_c = mod[B:B + 1, 0:D].reshape(1, 1, D)
    scale_c = mod[B:B + 1, D:2 * D].reshape(1, 1, D)

    wi = w_in[0]
    o = 0
    parts = {}
    for name, width in (("a_k", ATTN_KV_W), ("a_v", ATTN_KV_W), ("m_k", MLSTM_QK_W), ("m_v", MLSTM_V_W),
                        ("m_g", N_GATE), ("a_q", ATTN_Q_W), ("m_q", MLSTM_QK_W), ("m_o", MLSTM_V_W),
                        ("g_att", D), ("g_mem", D)):
        parts[name] = wi[:, o:o + width]
        o += width
    w_nat = jnp.concatenate([parts["g_att"], parts["g_mem"], parts["m_k"]], axis=1).astype(BF16)
    w_t = jnp.concatenate([parts["m_v"], parts["m_o"], parts["a_q"], parts["m_q"], parts["a_k"], parts["a_v"]],
                          axis=1).T.astype(BF16)
    w_g = parts["m_g"].T.astype(BF16)
    kinds_t = ([""] * (2 * MLSTM_V_W // LANE) + ["rope scale log2"] * (ATTN_Q_W // LANE)
               + ["scale"] * (MLSTM_QK_W // LANE) + ["rope"] * (ATTN_KV_W // LANE) + [""] * (ATTN_KV_W // LANE))
    assert (2 * D) % MLSTM_QK_W == 0 and (2 * MLSTM_V_W) % ATTN_Q_W == 0
    ga_blk, gm_blk, mk_blk = 0, 1, 2 * D // MLSTM_QK_W
    vt_blk, mot_blk = 0, 1
    aq_blk = 2 * MLSTM_V_W // ATTN_Q_W
    qt_blk = (2 * MLSTM_V_W + ATTN_Q_W) // MLSTM_QK_W
    kt_blk = (2 * MLSTM_V_W + ATTN_Q_W + MLSTM_QK_W) // ATTN_KV_W
    vtt_blk = kt_blk + 1
    w_nat_c = parts["m_k"].astype(BF16)
    w_t_c = jnp.concatenate([parts["m_v"], parts["a_k"], parts["a_v"]], axis=1).T.astype(BF16)
    kx_blk, vx_blk = MLSTM_V_W // ATTN_KV_W, MLSTM_V_W // ATTN_KV_W + 1

    p_lat, pt_lat, gt_lat = _inproj(x, shift1, scale1, norm1_g[0], w_nat, w_t, w_g, kinds_t,
                                    _rope_tables(S), tm=1024 if S % 1024 == 0 else 512, tn=tn)
    p_ctx, pt_ctx, gt_ctx = _inproj(ctx, shift_c, scale_c, norm1_g[0], w_nat_c, w_t_c, w_g,
                                    [""] * (w_t_c.shape[0] // LANE), None, tm=min(C, 256), tn=tn)

    att_t = _attention(pt_lat, pt_ctx, attn_sink[0], aq_blk, kt_blk, vtt_blk, kx_blk, vx_blk, qb=2)

    rows_lat, cols_lat = _gate_prep(gt_lat, b_gates[0])
    rows_ctx, cols_ctx = _gate_prep(gt_ctx, b_gates[0])
    c_zero = jnp.zeros((B, H, C_ROWS, MLSTM_QK_DIM), F32)
    m_zero = jnp.zeros((B, H, LANE), F32)
    _, _, cf_ctx, mf_ctx = _state_scan(p_ctx, 0, pt_ctx, 0, rows_ctx, cols_ctx, c_zero, m_zero, reverse=False,
                                       cps=4)
    _, _, cr_ctx, mr_ctx = _state_scan(p_ctx, 0, pt_ctx, 0, rows_ctx, cols_ctx, c_zero, m_zero, reverse=True,
                                       cps=4)
    c_rev, m_rev, _, _ = _state_scan(p_lat, mk_blk, pt_lat, vt_blk, rows_lat, cols_lat, cr_ctx, mr_ctx,
                                     reverse=True, cps=8)
    mem_t = _mlstm_out(p_lat, mk_blk, pt_lat, qt_blk, vt_blk, mot_blk, rows_lat, cols_lat, c_rev, m_rev,
                       cf_ctx, mf_ctx, mlstm_norm_g[0], cps=4)

    x1, h2 = _merge(att_t, mem_t, p_lat, ga_blk, gm_blk, x, gate1, shift2, scale2, norm2_g[0],
                    w_attn_proj[0].astype(BF16), w_mlstm_proj[0].astype(BF16), w_out[0].astype(BF16), tm=256)
    return _ffn(x1, h2, gate2, w_ffn_in[0].astype(BF16), w_ffn_out[0].astype(BF16), final_norm_g,
                tm=1024 if S % 1024 == 0 else 512, tf=512)
```

```python
import functools

import numpy as np
import jax
import jax.numpy as jnp
from jax import lax
from jax.experimental import pallas as pl
from jax.experimental.pallas import tpu as pltpu

F32 = jnp.float32
BF16 = jnp.bfloat16

GRID_W = 64
ATTN_HEADS = 16
ATTN_KV_HEADS = 4
ATTN_GROUP = ATTN_HEADS // ATTN_KV_HEADS
HEAD_DIM = 128
WINDOW = 128
ROPE_BASE = 10000.0
MLSTM_HEADS = 8
MLSTM_QK_DIM = 128
MLSTM_V_DIM = 256
CHUNK = 128
N_DIRS = 2
N_GATE = N_DIRS * 2 * MLSTM_HEADS
NORM_EPS = 1e-6
N_MOD = 6
QK_SCALE = HEAD_DIM ** -0.5
LOG2E = 1.4426950408889634
ROPE_PAIR = HEAD_DIM // 4

LANE = 128
BF16_SUBLANES = 16
V7X_VMEM_BYTES = 64 * 1024 * 1024
VMEM_LIMIT = V7X_VMEM_BYTES - 1 * 1024 * 1024

ATTN_Q_W = ATTN_HEADS * HEAD_DIM
ATTN_KV_W = ATTN_KV_HEADS * HEAD_DIM
MLSTM_QK_W = MLSTM_HEADS * MLSTM_QK_DIM
MLSTM_V_W = MLSTM_HEADS * MLSTM_V_DIM
C_ROWS = MLSTM_V_DIM + BF16_SUBLANES
NORM_ROWS = 128
X1_LOOKAHEAD = 3

NEG = -1e30


def _params(sem):
    return pltpu.CompilerParams(dimension_semantics=sem, vmem_limit_bytes=VMEM_LIMIT)


def _sigmoid(x):
    return 1.0 / (1.0 + jnp.exp(-x))


def _adaln_kernel(c_ref, w_ref, b_ref, o_ref):
    cc = c_ref[...]
    s = (cc * _sigmoid(cc)).astype(BF16)
    o_ref[...] = jnp.dot(s, w_ref[...].astype(BF16), preferred_element_type=F32) + b_ref[...]


def _adaln(cvecs, w, b):
    R, D = cvecs.shape
    N = w.shape[1]
    tn = 1024
    return pl.pallas_call(
        _adaln_kernel,
        out_shape=jax.ShapeDtypeStruct((R, N), F32),
        grid=(N // tn,),
        in_specs=[pl.BlockSpec((R, D), lambda j: (0, 0)),
                  pl.BlockSpec((D, tn), lambda j: (0, j)),
                  pl.BlockSpec((1, tn), lambda j: (0, j))],
        out_specs=pl.BlockSpec((R, tn), lambda j: (0, j)),
        compiler_params=_params(("arbitrary",)),
        name="adaln",
    )(cvecs, w, b.reshape(1, N))


def _tile_groups(kinds, per):
    tiles = [tuple(kinds[t * per:(t + 1) * per]) for t in range(len(kinds) // per)]
    groups = []
    for t, tk in enumerate(tiles):
        if groups and groups[-1][2] == tk:
            groups[-1] = (groups[-1][0], t + 1, tk)
        else:
            groups.append((t, t + 1, tk))
    return tuple(groups)


def _snake(tile, step, nsteps):
    return jnp.where(tile % 2 == 0, step, nsteps - 1 - step)


def _inproj_kernel(*refs, t_groups, nn, rope, n_i, n_j):
    if rope:
        (x_ref, shift_ref, scale_ref, g_ref, wn_ref, wt_ref, wg_ref, cos_ref, sin_lo_ref, sin_hi_ref,
         p_ref, pt_ref, gt_ref, h_scr) = refs
    else:
        x_ref, shift_ref, scale_ref, g_ref, wn_ref, wt_ref, wg_ref, p_ref, pt_ref, gt_ref, h_scr = refs
    j = _snake(pl.program_id(0) * n_i + pl.program_id(1), pl.program_id(2), n_j)
    nt_dims = (((1,), (1,)), ((), ()))

    @pl.when(pl.program_id(2) == 0)
    def _():
        tm = h_scr.shape[0]
        rc = min(tm, NORM_ROWS)
        for r0 in range(0, tm, rc):
            xf = x_ref[0, r0:r0 + rc, :]
            ms = jnp.mean(xf * xf, axis=-1, keepdims=True)
            y = xf * lax.rsqrt(ms + NORM_EPS) * g_ref[...]
            h_scr[r0:r0 + rc, :] = (y * (1.0 + scale_ref[0]) + shift_ref[0]).astype(BF16)
        gt_ref[0] = lax.dot_general(wg_ref[...], h_scr[...], nt_dims, preferred_element_type=F32)

    @pl.when(j < nn)
    def _():
        p_ref[0] = jnp.dot(h_scr[...], wn_ref[...], preferred_element_type=F32).astype(BF16)

    for lo, hi, kinds in t_groups:
        @pl.when((j >= nn + lo) & (j < nn + hi))
        def _(kinds=kinds):
            acc = lax.dot_general(wt_ref[...], h_scr[...], nt_dims, preferred_element_type=F32)
            for u, kind in enumerate(kinds):
                a = acc[u * LANE:(u + 1) * LANE]
                if "rope" in kind:
                    a = (a * cos_ref[...] + pltpu.roll(a, HEAD_DIM - ROPE_PAIR, 0) * sin_lo_ref[...]
                         + pltpu.roll(a, ROPE_PAIR, 0) * sin_hi_ref[...])
                if "scale" in kind:
                    a = a * (QK_SCALE * LOG2E if "log2" in kind else QK_SCALE)
                pt_ref[0, u * LANE:(u + 1) * LANE, :] = a.astype(BF16)


def _inproj(x, shift, scale, gain, w_nat, w_t, w_g, kinds_t, rope_tabs, tm, tn):
    B, T, D = x.shape
    n_nat = w_nat.shape[1]
    n_t = w_t.shape[0]
    nn, ntt = n_nat // tn, n_t // tn
    per = tn // LANE
    n_i = T // tm
    bm = shift.shape[0]
    rope = rope_tabs is not None

    n_j = nn + ntt

    def next_tile(b, i, j):
        t = jnp.minimum(b * n_i + i + (j >= 1), B * n_i - 1)
        return t // n_i, t % n_i

    def mod_map(b, i, j):
        return (next_tile(b, i, j)[0] if bm == B else 0, 0, 0)

    def wtile(b, i, j):
        return _snake(b * n_i + i, j, n_j)

    in_specs = [pl.BlockSpec((1, tm, D), lambda b, i, j: next_tile(b, i, j) + (0,)),
                pl.BlockSpec((1, 1, D), mod_map),
                pl.BlockSpec((1, 1, D), mod_map),
                pl.BlockSpec((1, D), lambda b, i, j: (0, 0)),
                pl.BlockSpec((D, tn), lambda b, i, j: (0, jnp.minimum(wtile(b, i, j), nn - 1))),
                pl.BlockSpec((tn, D), lambda b, i, j: (jnp.maximum(wtile(b, i, j) - nn, 0), 0)),
                pl.BlockSpec(w_g.shape, lambda b, i, j: (0, 0))]
    args = [x, shift, scale, gain.reshape(1, D), w_nat, w_t, w_g]
    if rope:
        in_specs += [pl.BlockSpec((HEAD_DIM, tm), lambda b, i, j: (0, i))] * len(rope_tabs)
        args += list(rope_tabs)
    return pl.pallas_call(
        functools.partial(_inproj_kernel, t_groups=_tile_groups(kinds_t, per), nn=nn, rope=rope, n_i=n_i,
                          n_j=n_j),
        out_shape=(jax.ShapeDtypeStruct((B, T, n_nat), BF16),
                   jax.ShapeDtypeStruct((B, n_t, T), BF16),
                   jax.ShapeDtypeStruct((B, w_g.shape[0], T), F32)),
        grid=(B, n_i, n_j),
        in_specs=in_specs,
        out_specs=(pl.BlockSpec((1, tm, tn), lambda b, i, j: (b, i, jnp.minimum(wtile(b, i, j), nn - 1))),
                   pl.BlockSpec((1, tn, tm), lambda b, i, j: (b, jnp.maximum(wtile(b, i, j) - nn, 0), i)),
                   pl.BlockSpec((1, w_g.shape[0], tm), lambda b, i, j: (b, 0, i))),
        scratch_shapes=[pltpu.VMEM((tm, D), BF16)],
        compiler_params=_params(("arbitrary", "arbitrary", "arbitrary")),
        name="inproj_rope" if rope else "inproj_ctx",
    )(*args)


def _scan_lanes(x, op, reverse, fill):
    lane = lax.broadcasted_iota(jnp.int32, x.shape, 1)
    k = 1
    while k < CHUNK:
        if reverse:
            sh = jnp.where(lane < CHUNK - k, pltpu.roll(x, CHUNK - k, 1), fill)
        else:
            sh = jnp.where(lane >= k, pltpu.roll(x, k, 1), fill)
        x = op(x, sh)
        k *= 2
    return x


def _log_sigmoid(z):
    return jnp.minimum(z, 0.0) - jnp.log(1.0 + jnp.exp(-jnp.abs(z)))


def _lane_value(x, lane_idx):
    lane = lax.broadcasted_iota(jnp.int32, x.shape, 1)
    return jnp.broadcast_to(jnp.sum(jnp.where(lane == lane_idx, x, 0.0), axis=1, keepdims=True), x.shape)


def _gate_prep_kernel(gt_ref, bias_ref, row_ref, col_ref, *, nchunk):
    H = MLSTM_HEADS
    for c in range(nchunk):
        sl = slice(c * CHUNK, (c + 1) * CHUNK)
        z = gt_ref[0, :, sl] + bias_ref[...]
        li_f, lf_f = z[0:H] * LOG2E, _log_sigmoid(z[H:2 * H]) * LOG2E
        li_r, lf_r = z[2 * H:3 * H] * LOG2E, _log_sigmoid(z[3 * H:4 * H]) * LOG2E
        b_f = _scan_lanes(lf_f, jnp.add, False, 0.0)
        b_r = _scan_lanes(lf_r, jnp.add, True, 0.0)
        u_f = li_f - b_f
        u_r = li_r - b_r
        r_f = _scan_lanes(u_f, jnp.maximum, False, -jnp.inf)
        r_r = _scan_lanes(u_r, jnp.maximum, True, -jnp.inf)
        ends = [_lane_value(r_f, CHUNK - 1), _lane_value(r_r, 0), _lane_value(b_f, CHUNK - 1), _lane_value(b_r, 0)]
        for k, v in enumerate([r_f, r_r, b_f, b_r] + ends):
            row_ref[0, k * H:(k + 1) * H, sl] = v
        stack = jnp.concatenate([u_f, u_r, jnp.zeros((LANE - 2 * H, CHUNK), F32)], axis=0)
        col_ref[0, sl, :] = stack.T


def _gate_prep(g_t, bias):
    B, G, T = g_t.shape
    tg = min(T, 16 * CHUNK)
    H = MLSTM_HEADS
    return pl.pallas_call(
        functools.partial(_gate_prep_kernel, nchunk=tg // CHUNK),
        out_shape=(jax.ShapeDtypeStruct((B, 8 * H, T), F32),
                   jax.ShapeDtypeStruct((B, T, LANE), F32)),
        grid=(B, T // tg),
        in_specs=[pl.BlockSpec((1, G, tg), lambda b, i: (b, 0, i)),
                  pl.BlockSpec((G, CHUNK), lambda b, i: (0, 0))],
        out_specs=(pl.BlockSpec((1, 8 * H, tg), lambda b, i: (b, 0, i)),
                   pl.BlockSpec((1, tg, LANE), lambda b, i: (b, i, 0))),
        compiler_params=_params(("parallel", "parallel")),
        name="gate_prep",
    )(g_t, jnp.broadcast_to(bias.reshape(G, 1), (G, CHUNK)))


def _attn_kernel(*refs, nb, qb):
    sink_ref, qt_ref = refs[:2]
    k_refs = refs[2:qb + 4]
    v_refs = refs[qb + 4:2 * qb + 6]
    kx_ref, vx_ref, band_ref, o_ref = refs[2 * qb + 6:]
    cols = ATTN_GROUP * WINDOW
    row = lax.broadcasted_iota(jnp.int32, (3 * WINDOW, WINDOW), 0)
    lane = lax.broadcasted_iota(jnp.int32, (1, cols), 1)
    tn_dims = (((0,), (0,)), ((), ()))
    for q in range(qb):
        j = pl.program_id(1) * qb + q
        qs = slice(q * WINDOW, (q + 1) * WINDOW)
        prev_bias = jnp.where(j > 0, 0.0, NEG)
        next_bias = jnp.where(j < nb - 1, 0.0, NEG)
        edge = jnp.where(row < WINDOW, prev_bias, jnp.where(row >= 2 * WINDOW, next_bias, 0.0))
        bias = band_ref[...] + jnp.concatenate([edge] * ATTN_GROUP, axis=1)
        for g in range(ATTN_KV_HEADS):
            hs = slice(g * HEAD_DIM, (g + 1) * HEAD_DIM)
            qt = jnp.concatenate([qt_ref[0, (g * ATTN_GROUP + h) * HEAD_DIM:(g * ATTN_GROUP + h + 1) * HEAD_DIM, qs]
                                  for h in range(ATTN_GROUP)], axis=1)
            kt = jnp.concatenate([r[0, hs, :] for r in k_refs[q:q + 3]] + [kx_ref[0, hs, :]], axis=1)
            vt = jnp.concatenate([r[0, hs, :] for r in v_refs[q:q + 3]] + [vx_ref[0, hs, :]], axis=1)
            st = lax.dot_general(kt, qt, tn_dims, preferred_element_type=F32)
            s_loc = st[:3 * WINDOW] + bias
            s_ctx = st[3 * WINDOW:]
            sink = jnp.full((1, cols), sink_ref[g * ATTN_GROUP] * LOG2E, F32)
            for h in range(1, ATTN_GROUP):
                sink = jnp.where(lane >= h * WINDOW, sink_ref[g * ATTN_GROUP + h] * LOG2E, sink)
            m = jnp.maximum(jnp.maximum(jnp.max(s_loc, axis=0, keepdims=True),
                                        jnp.max(s_ctx, axis=0, keepdims=True)), sink)
            p_loc = jnp.exp2(s_loc - m)
            p_ctx = jnp.exp2(s_ctx - m)
            den = (jnp.sum(p_loc, axis=0, keepdims=True) + jnp.sum(p_ctx, axis=0, keepdims=True)
                   + jnp.exp2(sink - m))
            pt = jnp.concatenate([p_loc, p_ctx], axis=0).astype(BF16)
            ot = jnp.dot(vt, pt, preferred_element_type=F32) * (1.0 / den)
            for h in range(ATTN_GROUP):
                r0 = (g * ATTN_GROUP + h) * HEAD_DIM
                o_ref[0, r0:r0 + HEAD_DIM, qs] = ot[:, h * WINDOW:(h + 1) * WINDOW].astype(BF16)


def _attention(pt_lat, pt_ctx, sink, q_blk, k_blk, v_blk, kx_blk, vx_blk, qb):
    B, _, S = pt_lat.shape
    C = pt_ctx.shape[2]
    nb = S // WINDOW
    assert nb % qb == 0
    cols = ATTN_GROUP * WINDOW
    t = np.arange(cols)[None, :] % WINDOW
    d = np.arange(3 * WINDOW)[:, None] - t
    band = jnp.asarray(np.where((d >= 0) & (d <= 2 * WINDOW), 0.0, NEG), F32)

    def kspec(off, blk):
        return pl.BlockSpec((1, ATTN_KV_W, WINDOW), lambda b, j: (b, blk, jnp.clip(j * qb + off, 0, nb - 1)))

    offs = range(-1, qb + 1)
    return pl.pallas_call(
        functools.partial(_attn_kernel, nb=nb, qb=qb),
        out_shape=jax.ShapeDtypeStruct((B, ATTN_Q_W, S), BF16),
        grid=(B, nb // qb),
        in_specs=[pl.BlockSpec(memory_space=pltpu.SMEM),
                  pl.BlockSpec((1, ATTN_Q_W, qb * WINDOW), lambda b, j: (b, q_blk, j))]
                 + [kspec(o, k_blk) for o in offs] + [kspec(o, v_blk) for o in offs]
                 + [pl.BlockSpec((1, ATTN_KV_W, C), lambda b, j: (b, kx_blk, 0)),
                    pl.BlockSpec((1, ATTN_KV_W, C), lambda b, j: (b, vx_blk, 0)),
                    pl.BlockSpec((3 * WINDOW, cols), lambda b, j: (0, 0))],
        out_specs=pl.BlockSpec((1, ATTN_Q_W, qb * WINDOW), lambda b, j: (b, 0, j)),
        compiler_params=_params(("parallel", "arbitrary")),
        name="window_attn",
    )(sink, pt_lat, *([pt_lat] * (2 * len(offs))), pt_ctx, pt_ctx, band)


def _v_aug_t(vt):
    sub = lax.broadcasted_iota(jnp.int32, (BF16_SUBLANES, vt.shape[1]), 0)
    return jnp.concatenate([vt, jnp.where(sub == 0, 1.0, 0.0).astype(BF16)], axis=0)


def _state_update(ct_old, m_old, k, vaug_t, u_bc, r_end, b_end):
    m_end = jnp.maximum(m_old, r_end)
    ks = (k.astype(F32) * jnp.exp2(u_bc - m_end)).astype(BF16)
    ct_new = jnp.exp2(m_old - m_end) * ct_old + jnp.dot(vaug_t, ks, preferred_element_type=F32)
    return ct_new, b_end + m_end


def _row(row_ref, k, h, ts):
    i = k * MLSTM_HEADS + h
    return row_ref[0, i:i + 1, ts]


def _scan_kernel(k_ref, vt_ref, row_ref, col_ref, c0_ref, m0_ref, cs_ref, ms_ref, cf_ref, mf_ref,
                 c_scr, m_scr, *, reverse, nsteps, cps):
    H = MLSTM_HEADS
    n = pl.program_id(1)
    d = 1 if reverse else 0

    @pl.when(n == 0)
    def _():
        c_scr[...] = c0_ref[0]
        m_scr[...] = m0_ref[0]

    for c in (range(cps - 1, -1, -1) if reverse else range(cps)):
        ts = slice(c * CHUNK, (c + 1) * CHUNK)
        cols = col_ref[0, ts, :]
        for h in range(H):
            ct_old = c_scr[h]
            m_old = m_scr[h:h + 1, :]
            cs_ref[0, c, h] = ct_old.astype(BF16)
            ms_ref[0, c, h:h + 1, :] = m_old
            k = k_ref[0, ts, h * MLSTM_QK_DIM:(h + 1) * MLSTM_QK_DIM]
            vaug_t = _v_aug_t(vt_ref[0, h * MLSTM_V_DIM:(h + 1) * MLSTM_V_DIM, ts])
            u_bc = jnp.broadcast_to(cols[:, d * H + h:d * H + h + 1], (CHUNK, LANE))
            ct_new, m_new = _state_update(ct_old, m_old, k, vaug_t, u_bc, _row(row_ref, 4 + d, h, ts),
                                          _row(row_ref, 6 + d, h, ts))
            c_scr[h] = ct_new
            m_scr[h:h + 1, :] = m_new

    @pl.when(n == nsteps - 1)
    def _():
        cf_ref[0] = c_scr[...]
        mf_ref[0] = m_scr[...]


def _state_scan(p, k_blk, p_t, vt_blk, rows, cols, c0, m0, reverse, cps):
    B, T, _ = p.shape
    nc = T // CHUNK
    cps = min(cps, nc)
    ns = nc // cps
    assert ns * cps == nc
    H = MLSTM_HEADS
    tb = cps * CHUNK
    cidx = (lambda n: ns - 1 - n) if reverse else (lambda n: n)
    st = (H, C_ROWS, MLSTM_QK_DIM)
    return pl.pallas_call(
        functools.partial(_scan_kernel, reverse=reverse, nsteps=ns, cps=cps),
        out_shape=(jax.ShapeDtypeStruct((B, nc) + st, BF16),
                   jax.ShapeDtypeStruct((B, nc, H, LANE), F32),
                   jax.ShapeDtypeStruct((B,) + st, F32),
                   jax.ShapeDtypeStruct((B, H, LANE), F32)),
        grid=(B, ns),
        in_specs=[pl.BlockSpec((1, tb, MLSTM_QK_W), lambda b, n: (b, cidx(n), k_blk)),
                  pl.BlockSpec((1, MLSTM_V_W, tb), lambda b, n: (b, vt_blk, cidx(n))),
                  pl.BlockSpec((1, 8 * H, tb), lambda b, n: (b, 0, cidx(n))),
                  pl.BlockSpec((1, tb, LANE), lambda b, n: (b, cidx(n), 0)),
                  pl.BlockSpec((1,) + st, lambda b, n: (b, 0, 0, 0)),
                  pl.BlockSpec((1, H, LANE), lambda b, n: (b, 0, 0))],
        out_specs=(pl.BlockSpec((1, cps) + st, lambda b, n: (b, cidx(n), 0, 0, 0)),
                   pl.BlockSpec((1, cps, H, LANE), lambda b, n: (b, cidx(n), 0, 0)),
                   pl.BlockSpec((1,) + st, lambda b, n: (b, 0, 0, 0)),
                   pl.BlockSpec((1, H, LANE), lambda b, n: (b, 0, 0))),
        scratch_shapes=[pltpu.VMEM(st, F32), pltpu.VMEM((H, LANE), F32)],
        compiler_params=_params(("parallel", "arbitrary")),
        name="mlstm_scan_rev" if reverse else "mlstm_scan_fwd",
    )(p, p_t, rows, cols, c0, m0)


def _mlstm_out_kernel(k_ref, qt_ref, vt_ref, mot_ref, row_ref, col_ref, cr_ref, mr_ref, c0_ref, m0_ref, g_ref,
                      o_ref, c_scr, m_scr, *, cps):
    H = MLSTM_HEADS
    L = CHUNK
    n = pl.program_id(1)

    @pl.when(n == 0)
    def _():
        c_scr[...] = c0_ref[0]
        m_scr[...] = m0_ref[0]

    si = lax.broadcasted_iota(jnp.int32, (L, L), 0)
    ti = lax.broadcasted_iota(jnp.int32, (L, L), 1)
    for c in range(cps):
        ts = slice(c * L, (c + 1) * L)
        cols = col_ref[0, ts, :]
        for h in range(H):
            qt = qt_ref[0, h * MLSTM_QK_DIM:(h + 1) * MLSTM_QK_DIM, ts]
            k = k_ref[0, ts, h * MLSTM_QK_DIM:(h + 1) * MLSTM_QK_DIM]
            vs = slice(h * MLSTM_V_DIM, (h + 1) * MLSTM_V_DIM)
            vaug_t = _v_aug_t(vt_ref[0, vs, ts])
            pt = jnp.dot(k, qt, preferred_element_type=F32)
            qt_f = qt.astype(F32)
            ct_f = c_scr[h]
            hs_t = None
            for d in range(N_DIRS):
                u_bc = jnp.broadcast_to(cols[:, d * H + h:d * H + h + 1], (L, LANE))
                m0 = m_scr[h:h + 1, :] if d == 0 else mr_ref[0, c, h:h + 1, :]
                ct = ct_f.astype(BF16) if d == 0 else cr_ref[0, c, h]
                mx = jnp.maximum(_row(row_ref, d, h, ts), m0)
                valid = (si <= ti) if d == 0 else (si >= ti)
                s_t = (pt * jnp.exp2(jnp.where(valid, u_bc - mx, -jnp.inf))).astype(BF16)
                q_in = (qt_f * jnp.exp2(m0 - mx)).astype(BF16)
                tot = jnp.dot(jnp.concatenate([vaug_t, ct], axis=1), jnp.concatenate([s_t, q_in], axis=0),
                              preferred_element_type=F32)
                den = jnp.maximum(jnp.abs(tot[MLSTM_V_DIM:MLSTM_V_DIM + 1]),
                                  jnp.exp2(-_row(row_ref, 2 + d, h, ts) - mx))
                part = tot[:MLSTM_V_DIM] * (1.0 / den)
                hs_t = part if hs_t is None else hs_t + part
                if d == 0:
                    u_bc_f = u_bc
            hn = hs_t * lax.rsqrt(jnp.mean(hs_t * hs_t, axis=0, keepdims=True) + NORM_EPS) * g_ref[vs, :]
            o_ref[0, vs, ts] = (_sigmoid(mot_ref[0, vs, ts].astype(F32)) * hn).astype(BF16)
            ct_new, m_new = _state_update(ct_f, m_scr[h:h + 1, :], k, vaug_t, u_bc_f, _row(row_ref, 4, h, ts),
                                          _row(row_ref, 6, h, ts))
            c_scr[h] = ct_new
            m_scr[h:h + 1, :] = m_new


def _mlstm_out(p, k_blk, p_t, qt_blk, vt_blk, mot_blk, rows, cols, c_rev, m_rev, c0, m0, gain, cps):
    B, S, _ = p.shape
    nc = S // CHUNK
    ns = nc // cps
    assert ns * cps == nc
    H = MLSTM_HEADS
    tb = cps * CHUNK
    st = (H, C_ROWS, MLSTM_QK_DIM)
    gain_bc = jnp.broadcast_to(gain.reshape(MLSTM_V_W, 1), (MLSTM_V_W, LANE))
    return pl.pallas_call(
        functools.partial(_mlstm_out_kernel, cps=cps),
        out_shape=jax.ShapeDtypeStruct((B, MLSTM_V_W, S), BF16),
        grid=(B, ns),
        in_specs=[pl.BlockSpec((1, tb, MLSTM_QK_W), lambda b, n: (b, n, k_blk)),
                  pl.BlockSpec((1, MLSTM_QK_W, tb), lambda b, n: (b, qt_blk, n)),
                  pl.BlockSpec((1, MLSTM_V_W, tb), lambda b, n: (b, vt_blk, n)),
                  pl.BlockSpec((1, MLSTM_V_W, tb), lambda b, n: (b, mot_blk, n)),
                  pl.BlockSpec((1, 8 * H, tb), lambda b, n: (b, 0, n)),
                  pl.BlockSpec((1, tb, LANE), lambda b, n: (b, n, 0)),
                  pl.BlockSpec((1, cps) + st, lambda b, n: (b, n, 0, 0, 0)),
                  pl.BlockSpec((1, cps, H, LANE), lambda b, n: (b, n, 0, 0)),
                  pl.BlockSpec((1,) + st, lambda b, n: (b, 0, 0, 0)),
                  pl.BlockSpec((1, H, LANE), lambda b, n: (b, 0, 0)),
                  pl.BlockSpec((MLSTM_V_W, LANE), lambda b, n: (0, 0))],
        out_specs=pl.BlockSpec((1, MLSTM_V_W, tb), lambda b, n: (b, 0, n)),
        scratch_shapes=[pltpu.VMEM(st, F32), pltpu.VMEM((H, LANE), F32)],
        compiler_params=_params(("parallel", "arbitrary")),
        name="mlstm_out",
    )(p, p_t, p_t, p_t, rows, cols, c_rev, m_rev, c0, m0, gain_bc)


def _merge_kernel(attt_ref, memt_ref, ga_ref, gm_ref, x_ref, gate_ref, shift_ref, scale_ref, g2_ref,
                  wa_ref, wm_ref, wo_ref, o_ref, h_ref):
    tn_dims = (((0,), (0,)), ((), ()))
    a = lax.dot_general(attt_ref[0], wa_ref[...], tn_dims, preferred_element_type=F32)
    m = lax.dot_general(memt_ref[0], wm_ref[...], tn_dims, preferred_element_type=F32)
    y = _sigmoid(ga_ref[0].astype(F32)) * a + _sigmoid(gm_ref[0].astype(F32)) * m
    z = jnp.dot(y.astype(BF16), wo_ref[...], preferred_element_type=F32)
    x1 = x_ref[0] + gate_ref[0] * z
    o_ref[0] = x1
    ms = jnp.mean(x1 * x1, axis=-1, keepdims=True)
    h_ref[0] = (x1 * lax.rsqrt(ms + NORM_EPS) * g2_ref[...] * (1.0 + scale_ref[0]) + shift_ref[0]).astype(BF16)


def _merge(att_t, mem_t, p, ga_blk, gm_blk, x, gate1, shift2, scale2, norm2_g, w_ap, w_mp, w_out, tm):
    B, S, D = x.shape
    resident = functools.partial(pl.BlockSpec, pipeline_mode=pl.Buffered(1))
    mod = pl.BlockSpec((1, 1, D), lambda b, i: (b, 0, 0))
    return pl.pallas_call(
        _merge_kernel,
        out_shape=(jax.ShapeDtypeStruct((B, S, D), F32), jax.ShapeDtypeStruct((B, S, D), BF16)),
        grid=(B, S // tm),
        in_specs=[pl.BlockSpec((1, ATTN_Q_W, tm), lambda b, i: (b, 0, i)),
                  pl.BlockSpec((1, MLSTM_V_W, tm), lambda b, i: (b, 0, i)),
                  pl.BlockSpec((1, tm, D), lambda b, i: (b, i, ga_blk)),
                  pl.BlockSpec((1, tm, D), lambda b, i: (b, i, gm_blk)),
                  pl.BlockSpec((1, tm, D), lambda b, i: (b, i, 0)),
                  mod, mod, mod,
                  pl.BlockSpec((1, D), lambda b, i: (0, 0)),
                  resident(w_ap.shape, lambda b, i: (0, 0)),
                  resident(w_mp.shape, lambda b, i: (0, 0)),
                  resident(w_out.shape, lambda b, i: (0, 0))],
        out_specs=(pl.BlockSpec((1, tm, D), lambda b, i: (b, i, 0)),
                   pl.BlockSpec((1, tm, D), lambda b, i: (b, i, 0))),
        compiler_params=_params(("parallel", "parallel")),
        name="merge_outproj",
    )(att_t, mem_t, p, p, x, gate1, shift2, scale2, norm2_g.reshape(1, D), w_ap, w_mp, w_out)


def _ffn_kernel(x1_hbm, h_ref, gate_ref, wg_ref, wu_ref, wo_ref, gf_ref, o_ref, x1_buf, x1_sem, *, nf, tm):
    b, i, f = pl.program_id(0), pl.program_id(1), pl.program_id(2)
    rc = min(tm, NORM_ROWS)
    x1_copy = pltpu.make_async_copy(x1_hbm.at[b, pl.ds(pl.multiple_of(i * tm, tm), tm), :], x1_buf, x1_sem)

    @pl.when(f == 0)
    def _():
        o_ref[...] = jnp.zeros_like(o_ref)

    @pl.when(f == max(nf - 1 - X1_LOOKAHEAD, 0))
    def _():
        x1_copy.start()

    h = h_ref[0]
    gt = jnp.dot(h, wg_ref[...], preferred_element_type=F32)
    up = jnp.dot(h, wu_ref[...], preferred_element_type=F32)
    act = (gt * _sigmoid(gt) * up).astype(BF16)
    nc = wo_ref.shape[0]
    for c0 in range(0, o_ref.shape[2], nc):
        o_ref[0, :, c0:c0 + nc] += jnp.dot(act, wo_ref[:, c0:c0 + nc], preferred_element_type=F32)

    @pl.when(f == nf - 1)
    def _():
        x1_copy.wait()
        for r0 in range(0, tm, rc):
            x2 = x1_buf[r0:r0 + rc, :] + gate_ref[0] * o_ref[0, r0:r0 + rc, :]
            ms = jnp.mean(x2 * x2, axis=-1, keepdims=True)
            o_ref[0, r0:r0 + rc, :] = x2 * lax.rsqrt(ms + NORM_EPS) * gf_ref[...]


def _ffn(x1, h2, gate2, w_in, w_out, final_g, tm, tf):
    B, S, D = x1.shape
    dff = w_out.shape[0]
    nf = dff // tf
    n_i = S // tm

    def ft(b, i, f):
        return _snake(b * n_i + i, f, nf)

    return pl.pallas_call(
        functools.partial(_ffn_kernel, nf=nf, tm=tm),
        out_shape=jax.ShapeDtypeStruct((B, S, D), F32),
        grid=(B, S // tm, nf),
        in_specs=[pl.BlockSpec(memory_space=pl.ANY),
                  pl.BlockSpec((1, tm, D), lambda b, i, f: (b, i, 0)),
                  pl.BlockSpec((1, 1, D), lambda b, i, f: (b, 0, 0)),
                  pl.BlockSpec((D, tf), lambda b, i, f: (0, ft(b, i, f))),
                  pl.BlockSpec((D, tf), lambda b, i, f: (0, nf + ft(b, i, f))),
                  pl.BlockSpec((tf, D), lambda b, i, f: (ft(b, i, f), 0)),
                  pl.BlockSpec((1, D), lambda b, i, f: (0, 0))],
        out_specs=pl.BlockSpec((1, tm, D), lambda b, i, f: (b, i, 0)),
        scratch_shapes=[pltpu.VMEM((tm, D), F32), pltpu.SemaphoreType.DMA(())],
        compiler_params=_params(("arbitrary", "arbitrary", "arbitrary")),
        name="ffn_final_norm",
    )(x1, h2, gate2, w_in, w_in, w_out, final_g.reshape(1, D))


def _rope_tables(S):
    pos = jnp.arange(S)
    rows = (pos // GRID_W).astype(F32)
    cols = (pos % GRID_W).astype(F32)
    inv_freq = ROPE_BASE ** (-jnp.arange(ROPE_PAIR, dtype=F32) / ROPE_PAIR)
    ar = rows[:, None] * inv_freq[None, :]
    ac = cols[:, None] * inv_freq[None, :]
    zero = jnp.zeros_like(ar)
    cos = jnp.concatenate([jnp.cos(ar), jnp.cos(ar), jnp.cos(ac), jnp.cos(ac)], axis=1)
    sin_lo = jnp.concatenate([-jnp.sin(ar), zero, -jnp.sin(ac), zero], axis=1)
    sin_hi = jnp.concatenate([zero, jnp.sin(ar), zero, jnp.sin(ac)], axis=1)
    return cos.T, sin_lo.T, sin_hi.T


def kernel(x, c, ctx, c_ctx, w_ada, b_ada, norm1_g, w_in, b_gates, attn_sink, mlstm_norm_g, w_attn_proj,
           w_mlstm_proj, w_out, norm2_g, w_ffn_in, w_ffn_out, final_norm_g):
    B, S, D = x.shape
    C = ctx.shape[1]
    assert w_ada.shape[0] == 1, "single-layer configuration"
    assert S % 512 == 0 and C % CHUNK == 0 and S % GRID_W == 0
    H = MLSTM_HEADS
    tn = 1024

    rows = -(-(B + 1) // 8) * 8
    cvecs = jnp.concatenate([c, c_ctx[None], jnp.zeros((rows - B - 1, D), F32)], axis=0)
    mod = _adaln(cvecs, w_ada[0], b_ada[0])
    shift1, scale1, gate1, shift2, scale2, gate2 = [mod[:B, k * D:(k + 1) * D].reshape(B, 1, D) for k in range(N_MOD)]
    shift_c = mod[B:B + 1, 0:D].reshape(1, 1, D)
    scale_c = mod[B:B + 1, D:2 * D].reshape(1, 1, D)

    wi = w_in[0]
    o = 0
    parts = {}
    for name, width in (("a_k", ATTN_KV_W), ("a_v", ATTN_KV_W), ("m_k", MLSTM_QK_W), ("m_v", MLSTM_V_W),
                        ("m_g", N_GATE), ("a_q", ATTN_Q_W), ("m_q", MLSTM_QK_W), ("m_o", MLSTM_V_W),
                        ("g_att", D), ("g_mem", D)):
        parts[name] = wi[:, o:o + width]
        o += width
    w_nat = jnp.concatenate([parts["g_att"], parts["g_mem"], parts["m_k"]], axis=1).astype(BF16)
    w_t = jnp.concatenate([parts["m_v"], parts["m_o"], parts["a_q"], parts["m_q"], parts["a_k"], parts["a_v"]],
                          axis=1).T.astype(BF16)
    w_g = parts["m_g"].T.astype(BF16)
    kinds_t = ([""] * (2 * MLSTM_V_W // LANE) + ["rope scale log2"] * (ATTN_Q_W // LANE)
               + ["scale"] * (MLSTM_QK_W // LANE) + ["rope"] * (ATTN_KV_W // LANE) + [""] * (ATTN_KV_W // LANE))
    assert (2 * D) % MLSTM_QK_W == 0 and (2 * MLSTM_V_W) % ATTN_Q_W == 0
    ga_blk, gm_blk, mk_blk = 0, 1, 2 * D // MLSTM_QK_W
    vt_blk, mot_blk = 0, 1
    aq_blk = 2 * MLSTM_V_W // ATTN_Q_W
    qt_blk = (2 * MLSTM_V_W + ATTN_Q_W) // MLSTM_QK_W
    kt_blk = (2 * MLSTM_V_W + ATTN_Q_W + MLSTM_QK_W) // ATTN_KV_W
    vtt_blk = kt_blk + 1
    w_nat_c = parts["m_k"].astype(BF16)
    w_t_c = jnp.concatenate([parts["m_v"], parts["a_k"], parts["a_v"]], axis=1).T.astype(BF16)
    kx_blk, vx_blk = MLSTM_V_W // ATTN_KV_W, MLSTM_V_W // ATTN_KV_W + 1

    p_lat, pt_lat, gt_lat = _inproj(x, shift1, scale1, norm1_g[0], w_nat, w_t, w_g, kinds_t,
                                    _rope_tables(S), tm=1024 if S % 1024 == 0 else 512, tn=tn)
    p_ctx, pt_ctx, gt_ctx = _inproj(ctx, shift_c, scale_c, norm1_g[0], w_nat_c, w_t_c, w_g,
                                    [""] * (w_t_c.shape[0] // LANE), None, tm=min(C, 256), tn=tn)

    att_t = _attention(pt_lat, pt_ctx, attn_sink[0], aq_blk, kt_blk, vtt_blk, kx_blk, vx_blk,
                       qb=4 if (S // WINDOW) % 4 == 0 else 1)

    rows_lat, cols_lat = _gate_prep(gt_lat, b_gates[0])
    rows_ctx, cols_ctx = _gate_prep(gt_ctx, b_gates[0])
    c_zero = jnp.zeros((B, H, C_ROWS, MLSTM_QK_DIM), F32)
    m_zero = jnp.zeros((B, H, LANE), F32)
    _, _, cf_ctx, mf_ctx = _state_scan(p_ctx, 0, pt_ctx, 0, rows_ctx, cols_ctx, c_zero, m_zero, reverse=False,
                                       cps=4)
    _, _, cr_ctx, mr_ctx = _state_scan(p_ctx, 0, pt_ctx, 0, rows_ctx, cols_ctx, c_zero, m_zero, reverse=True,
                                       cps=4)
    c_rev, m_rev, _, _ = _state_scan(p_lat, mk_blk, pt_lat, vt_blk, rows_lat, cols_lat, cr_ctx, mr_ctx,
                                     reverse=True, cps=16)
    mem_t = _mlstm_out(p_lat, mk_blk, pt_lat, qt_blk, vt_blk, mot_blk, rows_lat, cols_lat, c_rev, m_rev,
                       cf_ctx, mf_ctx, mlstm_norm_g[0], cps=4)

    x1, h2 = _merge(att_t, mem_t, p_lat, ga_blk, gm_blk, x, gate1, shift2, scale2, norm2_g[0],
                    w_attn_proj[0].astype(BF16), w_mlstm_proj[0].astype(BF16), w_out[0].astype(BF16), tm=256)
    return _ffn(x1, h2, gate2, w_ffn_in[0].astype(BF16), w_ffn_out[0].astype(BF16), final_norm_g,
                tm=1024 if S % 1024 == 0 else 512, tf=512)
```

```python
import functools

import numpy as np
import jax
import jax.numpy as jnp
from jax import lax
from jax.experimental import pallas as pl
from jax.experimental.pallas import tpu as pltpu

F32 = jnp.float32
BF16 = jnp.bfloat16

GRID_W = 64
ATTN_HEADS = 16
ATTN_KV_HEADS = 4
ATTN_GROUP = ATTN_HEADS // ATTN_KV_HEADS
HEAD_DIM = 128
WINDOW = 128
ROPE_BASE = 10000.0
MLSTM_HEADS = 8
MLSTM_QK_DIM = 128
MLSTM_V_DIM = 256
CHUNK = 128
N_DIRS = 2
N_GATE = N_DIRS * 2 * MLSTM_HEADS
NORM_EPS = 1e-6
N_MOD = 6
QK_SCALE = HEAD_DIM ** -0.5
LOG2E = 1.4426950408889634
ROPE_PAIR = HEAD_DIM // 4

LANE = 128
BF16_SUBLANES = 16
V7X_VMEM_BYTES = 64 * 1024 * 1024
VMEM_LIMIT = V7X_VMEM_BYTES - 1 * 1024 * 1024

ATTN_Q_W = ATTN_HEADS * HEAD_DIM
ATTN_KV_W = ATTN_KV_HEADS * HEAD_DIM
MLSTM_QK_W = MLSTM_HEADS * MLSTM_QK_DIM
MLSTM_V_W = MLSTM_HEADS * MLSTM_V_DIM
C_ROWS = MLSTM_V_DIM + BF16_SUBLANES
NORM_ROWS = 128
X1_LOOKAHEAD = 3

NEG = -1e30


def _params(sem):
    return pltpu.CompilerParams(dimension_semantics=sem, vmem_limit_bytes=VMEM_LIMIT)


def _sigmoid(x):
    return 1.0 / (1.0 + jnp.exp(-x))


def _adaln_kernel(c_ref, w_ref, b_ref, o_ref):
    cc = c_ref[...]
    s = (cc * _sigmoid(cc)).astype(BF16)
    o_ref[...] = jnp.dot(s, w_ref[...].astype(BF16), preferred_element_type=F32) + b_ref[...]


def _adaln(cvecs, w, b):
    R, D = cvecs.shape
    N = w.shape[1]
    tn = 1024
    return pl.pallas_call(
        _adaln_kernel,
        out_shape=jax.ShapeDtypeStruct((R, N), F32),
        grid=(N // tn,),
        in_specs=[pl.BlockSpec((R, D), lambda j: (0, 0)),
                  pl.BlockSpec((D, tn), lambda j: (0, j)),
                  pl.BlockSpec((1, tn), lambda j: (0, j))],
        out_specs=pl.BlockSpec((R, tn), lambda j: (0, j)),
        compiler_params=_params(("arbitrary",)),
        name="adaln",
    )(cvecs, w, b.reshape(1, N))


def _tile_groups(kinds, per):
    tiles = [tuple(kinds[t * per:(t + 1) * per]) for t in range(len(kinds) // per)]
    groups = []
    for t, tk in enumerate(tiles):
        if groups and groups[-1][2] == tk:
            groups[-1] = (groups[-1][0], t + 1, tk)
        else:
            groups.append((t, t + 1, tk))
    return tuple(groups)


def _snake(tile, step, nsteps):
    return jnp.where(tile % 2 == 0, step, nsteps - 1 - step)


def _inproj_kernel(*refs, t_groups, nn, rope, n_i, n_j):
    if rope:
        (x_ref, shift_ref, scale_ref, g_ref, wn_ref, wt_ref, wg_ref, cos_ref, sin_lo_ref, sin_hi_ref,
         p_ref, pt_ref, gt_ref, h_scr) = refs
    else:
        x_ref, shift_ref, scale_ref, g_ref, wn_ref, wt_ref, wg_ref, p_ref, pt_ref, gt_ref, h_scr = refs
    j = _snake(pl.program_id(0) * n_i + pl.program_id(1), pl.program_id(2), n_j)
    nt_dims = (((1,), (1,)), ((), ()))

    @pl.when(pl.program_id(2) == 0)
    def _():
        tm = h_scr.shape[0]
        rc = min(tm, NORM_ROWS)
        for r0 in range(0, tm, rc):
            xf = x_ref[0, r0:r0 + rc, :]
            ms = jnp.mean(xf * xf, axis=-1, keepdims=True)
            y = xf * lax.rsqrt(ms + NORM_EPS) * g_ref[...]
            h_scr[r0:r0 + rc, :] = (y * (1.0 + scale_ref[0]) + shift_ref[0]).astype(BF16)
        gt_ref[0] = lax.dot_general(wg_ref[...], h_scr[...], nt_dims, preferred_element_type=F32)

    @pl.when(j < nn)
    def _():
        p_ref[0] = jnp.dot(h_scr[...], wn_ref[...], preferred_element_type=F32).astype(BF16)

    for lo, hi, kinds in t_groups:
        @pl.when((j >= nn + lo) & (j < nn + hi))
        def _(kinds=kinds):
            acc = lax.dot_general(wt_ref[...], h_scr[...], nt_dims, preferred_element_type=F32)
            for u, kind in enumerate(kinds):
                a = acc[u * LANE:(u + 1) * LANE]
                if "rope" in kind:
                    a = (a * cos_ref[...] + pltpu.roll(a, HEAD_DIM - ROPE_PAIR, 0) * sin_lo_ref[...]
                         + pltpu.roll(a, ROPE_PAIR, 0) * sin_hi_ref[...])
                if "scale" in kind:
                    a = a * (QK_SCALE * LOG2E if "log2" in kind else QK_SCALE)
                pt_ref[0, u * LANE:(u + 1) * LANE, :] = a.astype(BF16)


def _inproj(x, shift, scale, gain, w_nat, w_t, w_g, kinds_t, rope_tabs, tm, tn):
    B, T, D = x.shape
    n_nat = w_nat.shape[1]
    n_t = w_t.shape[0]
    nn, ntt = n_nat // tn, n_t // tn
    per = tn // LANE
    n_i = T // tm
    bm = shift.shape[0]
    rope = rope_tabs is not None

    n_j = nn + ntt

    def next_tile(b, i, j):
        t = jnp.minimum(b * n_i + i + (j >= 1), B * n_i - 1)
        return t // n_i, t % n_i

    def mod_map(b, i, j):
        return (next_tile(b, i, j)[0] if bm == B else 0, 0, 0)

    def wtile(b, i, j):
        return _snake(b * n_i + i, j, n_j)

    in_specs = [pl.BlockSpec((1, tm, D), lambda b, i, j: next_tile(b, i, j) + (0,)),
                pl.BlockSpec((1, 1, D), mod_map),
                pl.BlockSpec((1, 1, D), mod_map),
                pl.BlockSpec((1, D), lambda b, i, j: (0, 0)),
                pl.BlockSpec((D, tn), lambda b, i, j: (0, jnp.minimum(wtile(b, i, j), nn - 1))),
                pl.BlockSpec((tn, D), lambda b, i, j: (jnp.maximum(wtile(b, i, j) - nn, 0), 0)),
                pl.BlockSpec(w_g.shape, lambda b, i, j: (0, 0))]
    args = [x, shift, scale, gain.reshape(1, D), w_nat, w_t, w_g]
    if rope:
        in_specs += [pl.BlockSpec((HEAD_DIM, tm), lambda b, i, j: (0, i))] * len(rope_tabs)
        args += list(rope_tabs)
    return pl.pallas_call(
        functools.partial(_inproj_kernel, t_groups=_tile_groups(kinds_t, per), nn=nn, rope=rope, n_i=n_i,
                          n_j=n_j),
        out_shape=(jax.ShapeDtypeStruct((B, T, n_nat), BF16),
                   jax.ShapeDtypeStruct((B, n_t, T), BF16),
                   jax.ShapeDtypeStruct((B, w_g.shape[0], T), F32)),
        grid=(B, n_i, n_j),
        in_specs=in_specs,
        out_specs=(pl.BlockSpec((1, tm, tn), lambda b, i, j: (b, i, jnp.minimum(wtile(b, i, j), nn - 1))),
                   pl.BlockSpec((1, tn, tm), lambda b, i, j: (b, jnp.maximum(wtile(b, i, j) - nn, 0), i)),
                   pl.BlockSpec((1, w_g.shape[0], tm), lambda b, i, j: (b, 0, i))),
        scratch_shapes=[pltpu.VMEM((tm, D), BF16)],
        compiler_params=_params(("arbitrary", "arbitrary", "arbitrary")),
        name="inproj_rope" if rope else "inproj_ctx",
    )(*args)


def _scan_lanes(x, op, reverse, fill):
    lane = lax.broadcasted_iota(jnp.int32, x.shape, 1)
    k = 1
    while k < CHUNK:
        if reverse:
            sh = jnp.where(lane < CHUNK - k, pltpu.roll(x, CHUNK - k, 1), fill)
        else:
            sh = jnp.where(lane >= k, pltpu.roll(x, k, 1), fill)
        x = op(x, sh)
        k *= 2
    return x


def _log_sigmoid(z):
    return jnp.minimum(z, 0.0) - jnp.log(1.0 + jnp.exp(-jnp.abs(z)))


def _lane_value(x, lane_idx):
    lane = lax.broadcasted_iota(jnp.int32, x.shape, 1)
    return jnp.broadcast_to(jnp.sum(jnp.where(lane == lane_idx, x, 0.0), axis=1, keepdims=True), x.shape)


def _gate_prep_kernel(gt_ref, bias_ref, row_ref, col_ref, *, nchunk):
    H = MLSTM_HEADS
    for c in range(nchunk):
        sl = slice(c * CHUNK, (c + 1) * CHUNK)
        z = gt_ref[0, :, sl] + bias_ref[...]
        li_f, lf_f = z[0:H] * LOG2E, _log_sigmoid(z[H:2 * H]) * LOG2E
        li_r, lf_r = z[2 * H:3 * H] * LOG2E, _log_sigmoid(z[3 * H:4 * H]) * LOG2E
        b_f = _scan_lanes(lf_f, jnp.add, False, 0.0)
        b_r = _scan_lanes(lf_r, jnp.add, True, 0.0)
        u_f = li_f - b_f
        u_r = li_r - b_r
        r_f = _scan_lanes(u_f, jnp.maximum, False, -jnp.inf)
        r_r = _scan_lanes(u_r, jnp.maximum, True, -jnp.inf)
        ends = [_lane_value(r_f, CHUNK - 1), _lane_value(r_r, 0), _lane_value(b_f, CHUNK - 1), _lane_value(b_r, 0)]
        for k, v in enumerate([r_f, r_r, b_f, b_r] + ends):
            row_ref[0, k * H:(k + 1) * H, sl] = v
        stack = jnp.concatenate([u_f, u_r, jnp.zeros((LANE - 2 * H, CHUNK), F32)], axis=0)
        col_ref[0, sl, :] = stack.T


def _gate_prep(g_t, bias):
    B, G, T = g_t.shape
    tg = min(T, 8 * CHUNK)
    H = MLSTM_HEADS
    return pl.pallas_call(
        functools.partial(_gate_prep_kernel, nchunk=tg // CHUNK),
        out_shape=(jax.ShapeDtypeStruct((B, 8 * H, T), F32),
                   jax.ShapeDtypeStruct((B, T, LANE), F32)),
        grid=(B, T // tg),
        in_specs=[pl.BlockSpec((1, G, tg), lambda b, i: (b, 0, i)),
                  pl.BlockSpec((G, CHUNK), lambda b, i: (0, 0))],
        out_specs=(pl.BlockSpec((1, 8 * H, tg), lambda b, i: (b, 0, i)),
                   pl.BlockSpec((1, tg, LANE), lambda b, i: (b, i, 0))),
        compiler_params=_params(("parallel", "parallel")),
        name="gate_prep",
    )(g_t, jnp.broadcast_to(bias.reshape(G, 1), (G, CHUNK)))


def _attn_kernel(sink_ref, qt_ref, kp_ref, km_ref, kn_ref, vp_ref, vm_ref, vn_ref, kx_ref, vx_ref, band_ref,
                 o_ref, *, nb, qb):
    cols = ATTN_GROUP * WINDOW
    row = lax.broadcasted_iota(jnp.int32, (3 * WINDOW, WINDOW), 0)
    lane = lax.broadcasted_iota(jnp.int32, (1, cols), 1)
    tn_dims = (((0,), (0,)), ((), ()))

    def key_blocks(prev_ref, main_ref, next_ref, hs, q):
        blk = lambda i: main_ref[0, hs, i * WINDOW:(i + 1) * WINDOW]
        return [prev_ref[0, hs, :] if q == 0 else blk(q - 1), blk(q),
                next_ref[0, hs, :] if q == qb - 1 else blk(q + 1)]

    for q in range(qb):
        j = pl.program_id(1) * qb + q
        qs = slice(q * WINDOW, (q + 1) * WINDOW)
        prev_bias = jnp.where(j > 0, 0.0, NEG)
        next_bias = jnp.where(j < nb - 1, 0.0, NEG)
        edge = jnp.where(row < WINDOW, prev_bias, jnp.where(row >= 2 * WINDOW, next_bias, 0.0))
        bias = band_ref[...] + jnp.concatenate([edge] * ATTN_GROUP, axis=1)
        for g in range(ATTN_KV_HEADS):
            hs = slice(g * HEAD_DIM, (g + 1) * HEAD_DIM)
            qt = jnp.concatenate([qt_ref[0, (g * ATTN_GROUP + h) * HEAD_DIM:(g * ATTN_GROUP + h + 1) * HEAD_DIM, qs]
                                  for h in range(ATTN_GROUP)], axis=1)
            kt = jnp.concatenate(key_blocks(kp_ref, km_ref, kn_ref, hs, q) + [kx_ref[0, hs, :]], axis=1)
            vt = jnp.concatenate(key_blocks(vp_ref, vm_ref, vn_ref, hs, q) + [vx_ref[0, hs, :]], axis=1)
            st = lax.dot_general(kt, qt, tn_dims, preferred_element_type=F32)
            s_loc = st[:3 * WINDOW] + bias
            s_ctx = st[3 * WINDOW:]
            sink = jnp.full((1, cols), sink_ref[g * ATTN_GROUP] * LOG2E, F32)
            for h in range(1, ATTN_GROUP):
                sink = jnp.where(lane >= h * WINDOW, sink_ref[g * ATTN_GROUP + h] * LOG2E, sink)
            m = jnp.maximum(jnp.maximum(jnp.max(s_loc, axis=0, keepdims=True),
                                        jnp.max(s_ctx, axis=0, keepdims=True)), sink)
            p_loc = jnp.exp2(s_loc - m)
            p_ctx = jnp.exp2(s_ctx - m)
            den = (jnp.sum(p_loc, axis=0, keepdims=True) + jnp.sum(p_ctx, axis=0, keepdims=True)
                   + jnp.exp2(sink - m))
            pt = jnp.concatenate([p_loc, p_ctx], axis=0).astype(BF16)
            ot = jnp.dot(vt, pt, preferred_element_type=F32) * (1.0 / den)
            for h in range(ATTN_GROUP):
                r0 = (g * ATTN_GROUP + h) * HEAD_DIM
                o_ref[0, r0:r0 + HEAD_DIM, qs] = ot[:, h * WINDOW:(h + 1) * WINDOW].astype(BF16)


def _attention(pt_lat, pt_ctx, sink, q_blk, k_blk, v_blk, kx_blk, vx_blk, qb):
    B, _, S = pt_lat.shape
    C = pt_ctx.shape[2]
    nb = S // WINDOW
    assert nb % qb == 0
    cols = ATTN_GROUP * WINDOW
    t = np.arange(cols)[None, :] % WINDOW
    d = np.arange(3 * WINDOW)[:, None] - t
    band = jnp.asarray(np.where((d >= 0) & (d <= 2 * WINDOW), 0.0, NEG), F32)

    def kspecs(blk):
        edge = lambda off: pl.BlockSpec((1, ATTN_KV_W, WINDOW),
                                        lambda b, j: (b, blk, jnp.clip(j * qb + off, 0, nb - 1)))
        return [edge(-1), pl.BlockSpec((1, ATTN_KV_W, qb * WINDOW), lambda b, j: (b, blk, j)), edge(qb)]

    return pl.pallas_call(
        functools.partial(_attn_kernel, nb=nb, qb=qb),
        out_shape=jax.ShapeDtypeStruct((B, ATTN_Q_W, S), BF16),
        grid=(B, nb // qb),
        in_specs=[pl.BlockSpec(memory_space=pltpu.SMEM),
                  pl.BlockSpec((1, ATTN_Q_W, qb * WINDOW), lambda b, j: (b, q_blk, j))]
                 + kspecs(k_blk) + kspecs(v_blk)
                 + [pl.BlockSpec((1, ATTN_KV_W, C), lambda b, j: (b, kx_blk, 0)),
                    pl.BlockSpec((1, ATTN_KV_W, C), lambda b, j: (b, vx_blk, 0)),
                    pl.BlockSpec((3 * WINDOW, cols), lambda b, j: (0, 0))],
        out_specs=pl.BlockSpec((1, ATTN_Q_W, qb * WINDOW), lambda b, j: (b, 0, j)),
        compiler_params=_params(("parallel", "arbitrary")),
        name="window_attn",
    )(sink, pt_lat, *([pt_lat] * 6), pt_ctx, pt_ctx, band)


def _v_aug_t(vt):
    sub = lax.broadcasted_iota(jnp.int32, (BF16_SUBLANES, vt.shape[1]), 0)
    return jnp.concatenate([vt, jnp.where(sub == 0, 1.0, 0.0).astype(BF16)], axis=0)


def _state_update(ct_old, m_old, k, vaug_t, u_bc, r_end, b_end):
    m_end = jnp.maximum(m_old, r_end)
    ks = (k.astype(F32) * jnp.exp2(u_bc - m_end)).astype(BF16)
    ct_new = jnp.exp2(m_old - m_end) * ct_old + jnp.dot(vaug_t, ks, preferred_element_type=F32)
    return ct_new, b_end + m_end


def _row(row_ref, k, h, ts):
    i = k * MLSTM_HEADS + h
    return row_ref[0, i:i + 1, ts]


def _scan_kernel(k_ref, vt_ref, row_ref, col_ref, c0_ref, m0_ref, cs_ref, ms_ref, cf_ref, mf_ref,
                 c_scr, m_scr, *, reverse, nsteps, cps):
    H = MLSTM_HEADS
    n = pl.program_id(1)
    d = 1 if reverse else 0

    @pl.when(n == 0)
    def _():
        c_scr[...] = c0_ref[0]
        m_scr[...] = m0_ref[0]

    for c in (range(cps - 1, -1, -1) if reverse else range(cps)):
        ts = slice(c * CHUNK, (c + 1) * CHUNK)
        cols = col_ref[0, ts, :]
        for h in range(H):
            ct_old = c_scr[h]
            m_old = m_scr[h:h + 1, :]
            cs_ref[0, c, h] = ct_old.astype(BF16)
            ms_ref[0, c, h:h + 1, :] = m_old
            k = k_ref[0, ts, h * MLSTM_QK_DIM:(h + 1) * MLSTM_QK_DIM]
            vaug_t = _v_aug_t(vt_ref[0, h * MLSTM_V_DIM:(h + 1) * MLSTM_V_DIM, ts])
            u_bc = jnp.broadcast_to(cols[:, d * H + h:d * H + h + 1], (CHUNK, LANE))
            ct_new, m_new = _state_update(ct_old, m_old, k, vaug_t, u_bc, _row(row_ref, 4 + d, h, ts),
                                          _row(row_ref, 6 + d, h, ts))
            c_scr[h] = ct_new
            m_scr[h:h + 1, :] = m_new

    @pl.when(n == nsteps - 1)
    def _():
        cf_ref[0] = c_scr[...]
        mf_ref[0] = m_scr[...]


def _state_scan(p, k_blk, p_t, vt_blk, rows, cols, c0, m0, reverse, cps):
    B, T, _ = p.shape
    nc = T // CHUNK
    cps = min(cps, nc)
    ns = nc // cps
    assert ns * cps == nc
    H = MLSTM_HEADS
    tb = cps * CHUNK
    cidx = (lambda n: ns - 1 - n) if reverse else (lambda n: n)
    st = (H, C_ROWS, MLSTM_QK_DIM)
    return pl.pallas_call(
        functools.partial(_scan_kernel, reverse=reverse, nsteps=ns, cps=cps),
        out_shape=(jax.ShapeDtypeStruct((B, nc) + st, BF16),
                   jax.ShapeDtypeStruct((B, nc, H, LANE), F32),
                   jax.ShapeDtypeStruct((B,) + st, F32),
                   jax.ShapeDtypeStruct((B, H, LANE), F32)),
        grid=(B, ns),
        in_specs=[pl.BlockSpec((1, tb, MLSTM_QK_W), lambda b, n: (b, cidx(n), k_blk)),
                  pl.BlockSpec((1, MLSTM_V_W, tb), lambda b, n: (b, vt_blk, cidx(n))),
                  pl.BlockSpec((1, 8 * H, tb), lambda b, n: (b, 0, cidx(n))),
                  pl.BlockSpec((1, tb, LANE), lambda b, n: (b, cidx(n), 0)),
                  pl.BlockSpec((1,) + st, lambda b, n: (b, 0, 0, 0)),
                  pl.BlockSpec((1, H, LANE), lambda b, n: (b, 0, 0))],
        out_specs=(pl.BlockSpec((1, cps) + st, lambda b, n: (b, cidx(n), 0, 0, 0)),
                   pl.BlockSpec((1, cps, H, LANE), lambda b, n: (b, cidx(n), 0, 0)),
                   pl.BlockSpec((1,) + st, lambda b, n: (b, 0, 0, 0)),
                   pl.BlockSpec((1, H, LANE), lambda b, n: (b, 0, 0))),
        scratch_shapes=[pltpu.VMEM(st, F32), pltpu.VMEM((H, LANE), F32)],
        compiler_params=_params(("parallel", "arbitrary")),
        name="mlstm_scan_rev" if reverse else "mlstm_scan_fwd",
    )(p, p_t, rows, cols, c0, m0)


def _mlstm_out_kernel(k_ref, qt_ref, vt_ref, mot_ref, row_ref, col_ref, cr_ref, mr_ref, c0_ref, m0_ref, g_ref,
                      o_ref, c_scr, m_scr, *, cps):
    H = MLSTM_HEADS
    L = CHUNK
    n = pl.program_id(1)

    @pl.when(n == 0)
    def _():
        c_scr[...] = c0_ref[0]
        m_scr[...] = m0_ref[0]

    si = lax.broadcasted_iota(jnp.int32, (L, L), 0)
    ti = lax.broadcasted_iota(jnp.int32, (L, L), 1)
    for c in range(cps):
        ts = slice(c * L, (c + 1) * L)
        cols = col_ref[0, ts, :]
        for h in range(H):
            qt = qt_ref[0, h * MLSTM_QK_DIM:(h + 1) * MLSTM_QK_DIM, ts]
            k = k_ref[0, ts, h * MLSTM_QK_DIM:(h + 1) * MLSTM_QK_DIM]
            vs = slice(h * MLSTM_V_DIM, (h + 1) * MLSTM_V_DIM)
            vaug_t = _v_aug_t(vt_ref[0, vs, ts])
            pt = jnp.dot(k, qt, preferred_element_type=F32)
            qt_f = qt.astype(F32)
            ct_f = c_scr[h]
            hs_t = None
            for d in range(N_DIRS):
                u_bc = jnp.broadcast_to(cols[:, d * H + h:d * H + h + 1], (L, LANE))
                m0 = m_scr[h:h + 1, :] if d == 0 else mr_ref[0, c, h:h + 1, :]
                ct = ct_f.astype(BF16) if d == 0 else cr_ref[0, c, h]
                mx = jnp.maximum(_row(row_ref, d, h, ts), m0)
                valid = (si <= ti) if d == 0 else (si >= ti)
                s_t = (pt * jnp.exp2(jnp.where(valid, u_bc - mx, -jnp.inf))).astype(BF16)
                q_in = (qt_f * jnp.exp2(m0 - mx)).astype(BF16)
                tot = jnp.dot(jnp.concatenate([vaug_t, ct], axis=1), jnp.concatenate([s_t, q_in], axis=0),
                              preferred_element_type=F32)
                den = jnp.maximum(jnp.abs(tot[MLSTM_V_DIM:MLSTM_V_DIM + 1]),
                                  jnp.exp2(-_row(row_ref, 2 + d, h, ts) - mx))
                part = tot[:MLSTM_V_DIM] * (1.0 / den)
                hs_t = part if hs_t is None else hs_t + part
                if d == 0:
                    u_bc_f = u_bc
            hn = hs_t * lax.rsqrt(jnp.mean(hs_t * hs_t, axis=0, keepdims=True) + NORM_EPS) * g_ref[vs, :]
            o_ref[0, vs, ts] = (_sigmoid(mot_ref[0, vs, ts].astype(F32)) * hn).astype(BF16)
            ct_new, m_new = _state_update(ct_f, m_scr[h:h + 1, :], k, vaug_t, u_bc_f, _row(row_ref, 4, h, ts),
                                          _row(row_ref, 6, h, ts))
            c_scr[h] = ct_new
            m_scr[h:h + 1, :] = m_new


def _mlstm_out(p, k_blk, p_t, qt_blk, vt_blk, mot_blk, rows, cols, c_rev, m_rev, c0, m0, gain, cps):
    B, S, _ = p.shape
    nc = S // CHUNK
    ns = nc // cps
    assert ns * cps == nc
    H = MLSTM_HEADS
    tb = cps * CHUNK
    st = (H, C_ROWS, MLSTM_QK_DIM)
    gain_bc = jnp.broadcast_to(gain.reshape(MLSTM_V_W, 1), (MLSTM_V_W, LANE))
    return pl.pallas_call(
        functools.partial(_mlstm_out_kernel, cps=cps),
        out_shape=jax.ShapeDtypeStruct((B, MLSTM_V_W, S), BF16),
        grid=(B, ns),
        in_specs=[pl.BlockSpec((1, tb, MLSTM_QK_W), lambda b, n: (b, n, k_blk)),
                  pl.BlockSpec((1, MLSTM_QK_W, tb), lambda b, n: (b, qt_blk, n)),
                  pl.BlockSpec((1, MLSTM_V_W, tb), lambda b, n: (b, vt_blk, n)),
                  pl.BlockSpec((1, MLSTM_V_W, tb), lambda b, n: (b, mot_blk, n)),
                  pl.BlockSpec((1, 8 * H, tb), lambda b, n: (b, 0, n)),
                  pl.BlockSpec((1, tb, LANE), lambda b, n: (b, n, 0)),
                  pl.BlockSpec((1, cps) + st, lambda b, n: (b, n, 0, 0, 0)),
                  pl.BlockSpec((1, cps, H, LANE), lambda b, n: (b, n, 0, 0)),
                  pl.BlockSpec((1,) + st, lambda b, n: (b, 0, 0, 0)),
                  pl.BlockSpec((1, H, LANE), lambda b, n: (b, 0, 0)),
                  pl.BlockSpec((MLSTM_V_W, LANE), lambda b, n: (0, 0))],
        out_specs=pl.BlockSpec((1, MLSTM_V_W, tb), lambda b, n: (b, 0, n)),
        scratch_shapes=[pltpu.VMEM(st, F32), pltpu.VMEM((H, LANE), F32)],
        compiler_params=_params(("parallel", "arbitrary")),
        name="mlstm_out",
    )(p, p_t, p_t, p_t, rows, cols, c_rev, m_rev, c0, m0, gain_bc)


def _merge_kernel(attt_ref, memt_ref, ga_ref, gm_ref, x_ref, gate_ref, shift_ref, scale_ref, g2_ref,
                  wa_ref, wm_ref, wo_ref, o_ref, h_ref):
    tn_dims = (((0,), (0,)), ((), ()))
    a = lax.dot_general(attt_ref[0], wa_ref[...], tn_dims, preferred_element_type=F32)
    m = lax.dot_general(memt_ref[0], wm_ref[...], tn_dims, preferred_element_type=F32)
    y = _sigmoid(ga_ref[0].astype(F32)) * a + _sigmoid(gm_ref[0].astype(F32)) * m
    z = jnp.dot(y.astype(BF16), wo_ref[...], preferred_element_type=F32)
    x1 = x_ref[0] + gate_ref[0] * z
    o_ref[0] = x1
    ms = jnp.mean(x1 * x1, axis=-1, keepdims=True)
    h_ref[0] = (x1 * lax.rsqrt(ms + NORM_EPS) * g2_ref[...] * (1.0 + scale_ref[0]) + shift_ref[0]).astype(BF16)


def _merge(att_t, mem_t, p, ga_blk, gm_blk, x, gate1, shift2, scale2, norm2_g, w_ap, w_mp, w_out, tm):
    B, S, D = x.shape
    resident = functools.partial(pl.BlockSpec, pipeline_mode=pl.Buffered(1))
    mod = pl.BlockSpec((1, 1, D), lambda b, i: (b, 0, 0))
    return pl.pallas_call(
        _merge_kernel,
        out_shape=(jax.ShapeDtypeStruct((B, S, D), F32), jax.ShapeDtypeStruct((B, S, D), BF16)),
        grid=(B, S // tm),
        in_specs=[pl.BlockSpec((1, ATTN_Q_W, tm), lambda b, i: (b, 0, i)),
                  pl.BlockSpec((1, MLSTM_V_W, tm), lambda b, i: (b, 0, i)),
                  pl.BlockSpec((1, tm, D), lambda b, i: (b, i, ga_blk)),
                  pl.BlockSpec((1, tm, D), lambda b, i: (b, i, gm_blk)),
                  pl.BlockSpec((1, tm, D), lambda b, i: (b, i, 0)),
                  mod, mod, mod,
                  pl.BlockSpec((1, D), lambda b, i: (0, 0)),
                  resident(w_ap.shape, lambda b, i: (0, 0)),
                  resident(w_mp.shape, lambda b, i: (0, 0)),
                  resident(w_out.shape, lambda b, i: (0, 0))],
        out_specs=(pl.BlockSpec((1, tm, D), lambda b, i: (b, i, 0)),
                   pl.BlockSpec((1, tm, D), lambda b, i: (b, i, 0))),
        compiler_params=_params(("parallel", "parallel")),
        name="merge_outproj",
    )(att_t, mem_t, p, p, x, gate1, shift2, scale2, norm2_g.reshape(1, D), w_ap, w_mp, w_out)


def _ffn_kernel(x1_hbm, h_ref, gate_ref, wg_ref, wu_ref, wo_ref, gf_ref, o_ref, x1_buf, x1_sem, *, nf, tm):
    b, i, f = pl.program_id(0), pl.program_id(1), pl.program_id(2)
    rc = min(tm, NORM_ROWS)
    x1_copy = pltpu.make_async_copy(x1_hbm.at[b, pl.ds(pl.multiple_of(i * tm, tm), tm), :], x1_buf, x1_sem)

    @pl.when(f == 0)
    def _():
        o_ref[...] = jnp.zeros_like(o_ref)

    @pl.when(f == max(nf - 1 - X1_LOOKAHEAD, 0))
    def _():
        x1_copy.start()

    h = h_ref[0]
    gt = jnp.dot(h, wg_ref[...], preferred_element_type=F32)
    up = jnp.dot(h, wu_ref[...], preferred_element_type=F32)
    act = (gt * _sigmoid(gt) * up).astype(BF16)
    nc = wo_ref.shape[0]
    for c0 in range(0, o_ref.shape[2], nc):
        o_ref[0, :, c0:c0 + nc] += jnp.dot(act, wo_ref[:, c0:c0 + nc], preferred_element_type=F32)

    @pl.when(f == nf - 1)
    def _():
        x1_copy.wait()
        for r0 in range(0, tm, rc):
            x2 = x1_buf[r0:r0 + rc, :] + gate_ref[0] * o_ref[0, r0:r0 + rc, :]
            ms = jnp.mean(x2 * x2, axis=-1, keepdims=True)
            o_ref[0, r0:r0 + rc, :] = x2 * lax.rsqrt(ms + NORM_EPS) * gf_ref[...]


def _ffn(x1, h2, gate2, w_in, w_out, final_g, tm, tf):
    B, S, D = x1.shape
    dff = w_out.shape[0]
    nf = dff // tf
    n_i = S // tm

    def ft(b, i, f):
        return _snake(b * n_i + i, f, nf)

    return pl.pallas_call(
        functools.partial(_ffn_kernel, nf=nf, tm=tm),
        out_shape=jax.ShapeDtypeStruct((B, S, D), F32),
        grid=(B, S // tm, nf),
        in_specs=[pl.BlockSpec(memory_space=pl.ANY),
                  pl.BlockSpec((1, tm, D), lambda b, i, f: (b, i, 0)),
                  pl.BlockSpec((1, 1, D), lambda b, i, f: (b, 0, 0)),
                  pl.BlockSpec((D, tf), lambda b, i, f: (0, ft(b, i, f))),
                  pl.BlockSpec((D, tf), lambda b, i, f: (0, nf + ft(b, i, f))),
                  pl.BlockSpec((tf, D), lambda b, i, f: (ft(b, i, f), 0)),
                  pl.BlockSpec((1, D), lambda b, i, f: (0, 0))],
        out_specs=pl.BlockSpec((1, tm, D), lambda b, i, f: (b, i, 0)),
        scratch_shapes=[pltpu.VMEM((tm, D), F32), pltpu.SemaphoreType.DMA(())],
        compiler_params=_params(("arbitrary", "arbitrary", "arbitrary")),
        name="ffn_final_norm",
    )(x1, h2, gate2, w_in, w_in, w_out, final_g.reshape(1, D))


def _rope_tables(S):
    pos = jnp.arange(S)
    rows = (pos // GRID_W).astype(F32)
    cols = (pos % GRID_W).astype(F32)
    inv_freq = ROPE_BASE ** (-jnp.arange(ROPE_PAIR, dtype=F32) / ROPE_PAIR)
    ar = rows[:, None] * inv_freq[None, :]
    ac = cols[:, None] * inv_freq[None, :]
    zero = jnp.zeros_like(ar)
    cos = jnp.concatenate([jnp.cos(ar), jnp.cos(ar), jnp.cos(ac), jnp.cos(ac)], axis=1)
    sin_lo = jnp.concatenate([-jnp.sin(ar), zero, -jnp.sin(ac), zero], axis=1)
    sin_hi = jnp.concatenate([zero, jnp.sin(ar), zero, jnp.sin(ac)], axis=1)
    return cos.T, sin_lo.T, sin_hi.T


def kernel(x, c, ctx, c_ctx, w_ada, b_ada, norm1_g, w_in, b_gates, attn_sink, mlstm_norm_g, w_attn_proj,
           w_mlstm_proj, w_out, norm2_g, w_ffn_in, w_ffn_out, final_norm_g):
    B, S, D = x.shape
    C = ctx.shape[1]
    assert w_ada.shape[0] == 1, "single-layer configuration"
    assert S % 512 == 0 and C % CHUNK == 0 and S % GRID_W == 0
    H = MLSTM_HEADS
    tn = 1024

    rows = -(-(B + 1) // 8) * 8
    cvecs = jnp.concatenate([c, c_ctx[None], jnp.zeros((rows - B - 1, D), F32)], axis=0)
    mod = _adaln(cvecs, w_ada[0], b_ada[0])
    shift1, scale1, gate1, shift2, scale2, gate2 = [mod[:B, k * D:(k + 1) * D].reshape(B, 1, D) for k in range(N_MOD)]
    shift_c = mod[B:B + 1, 0:D].reshape(1, 1, D)
    scale_c = mod[B:B + 1, D:2 * D].reshape(1, 1, D)

    wi = w_in[0]
    o = 0
    parts = {}
    for name, width in (("a_k", ATTN_KV_W), ("a_v", ATTN_KV_W), ("m_k", MLSTM_QK_W), ("m_v", MLSTM_V_W),
                        ("m_g", N_GATE), ("a_q", ATTN_Q_W), ("m_q", MLSTM_QK_W), ("m_o", MLSTM_V_W),
                        ("g_att", D), ("g_mem", D)):
        parts[name] = wi[:, o:o + width]
        o += width
    w_nat = jnp.concatenate([parts["g_att"], parts["g_mem"], parts["m_k"]], axis=1).astype(BF16)
    w_t = jnp.concatenate([parts["m_v"], parts["m_o"], parts["a_q"], parts["m_q"], parts["a_k"], parts["a_v"]],
                          axis=1).T.astype(BF16)
    w_g = parts["m_g"].T.astype(BF16)
    kinds_t = ([""] * (2 * MLSTM_V_W // LANE) + ["rope scale log2"] * (ATTN_Q_W // LANE)
               + ["scale"] * (MLSTM_QK_W // LANE) + ["rope"] * (ATTN_KV_W // LANE) + [""] * (ATTN_KV_W // LANE))
    assert (2 * D) % MLSTM_QK_W == 0 and (2 * MLSTM_V_W) % ATTN_Q_W == 0
    ga_blk, gm_blk, mk_blk = 0, 1, 2 * D // MLSTM_QK_W
    vt_blk, mot_blk = 0, 1
    aq_blk = 2 * MLSTM_V_W // ATTN_Q_W
    qt_blk = (2 * MLSTM_V_W + ATTN_Q_W) // MLSTM_QK_W
    kt_blk = (2 * MLSTM_V_W + ATTN_Q_W + MLSTM_QK_W) // ATTN_KV_W
    vtt_blk = kt_blk + 1
    w_nat_c = parts["m_k"].astype(BF16)
    w_t_c = jnp.concatenate([parts["m_v"], parts["a_k"], parts["a_v"]], axis=1).T.astype(BF16)
    kx_blk, vx_blk = MLSTM_V_W // ATTN_KV_W, MLSTM_V_W // ATTN_KV_W + 1

    p_lat, pt_lat, gt_lat = _inproj(x, shift1, scale1, norm1_g[0], w_nat, w_t, w_g, kinds_t,
                                    _rope_tables(S), tm=1024 if S % 1024 == 0 else 512, tn=tn)
    p_ctx, pt_ctx, gt_ctx = _inproj(ctx, shift_c, scale_c, norm1_g[0], w_nat_c, w_t_c, w_g,
                                    [""] * (w_t_c.shape[0] // LANE), None, tm=min(C, 256), tn=tn)

    att_t = _attention(pt_lat, pt_ctx, attn_sink[0], aq_blk, kt_blk, vtt_blk, kx_blk, vx_blk,
                       qb=8 if (S // WINDOW) % 8 == 0 else 1)

    rows_lat, cols_lat = _gate_prep(gt_lat, b_gates[0])
    rows_ctx, cols_ctx = _gate_prep(gt_ctx, b_gates[0])
    c_zero = jnp.zeros((B, H, C_ROWS, MLSTM_QK_DIM), F32)
    m_zero = jnp.zeros((B, H, LANE), F32)
    _, _, cf_ctx, mf_ctx = _state_scan(p_ctx, 0, pt_ctx, 0, rows_ctx, cols_ctx, c_zero, m_zero, reverse=False,
                                       cps=4)
    _, _, cr_ctx, mr_ctx = _state_scan(p_ctx, 0, pt_ctx, 0, rows_ctx, cols_ctx, c_zero, m_zero, reverse=True,
                                       cps=4)
    c_rev, m_rev, _, _ = _state_scan(p_lat, mk_blk, pt_lat, vt_blk, rows_lat, cols_lat, cr_ctx, mr_ctx,
                                     reverse=True, cps=16)
    mem_t = _mlstm_out(p_lat, mk_blk, pt_lat, qt_blk, vt_blk, mot_blk, rows_lat, cols_lat, c_rev, m_rev,
                       cf_ctx, mf_ctx, mlstm_norm_g[0], cps=4)

    x1, h2 = _merge(att_t, mem_t, p_lat, ga_blk, gm_blk, x, gate1, shift2, scale2, norm2_g[0],
                    w_attn_proj[0].astype(BF16), w_mlstm_proj[0].astype(BF16), w_out[0].astype(BF16), tm=256)
    return _ffn(x1, h2, gate2, w_ffn_in[0].astype(BF16), w_ffn_out[0].astype(BF16), final_norm_g,
                tm=1024 if S % 1024 == 0 else 512, tf=512)
```

```python
import functools

import numpy as np
import jax
import jax.numpy as jnp
from jax import lax
from jax.experimental import pallas as pl
from jax.experimental.pallas import tpu as pltpu

F32 = jnp.float32
BF16 = jnp.bfloat16

GRID_W = 64
ATTN_HEADS = 16
ATTN_KV_HEADS = 4
ATTN_GROUP = ATTN_HEADS // ATTN_KV_HEADS
HEAD_DIM = 128
WINDOW = 128
ROPE_BASE = 10000.0
MLSTM_HEADS = 8
MLSTM_QK_DIM = 128
MLSTM_V_DIM = 256
CHUNK = 128
N_DIRS = 2
N_GATE = N_DIRS * 2 * MLSTM_HEADS
NORM_EPS = 1e-6
N_MOD = 6
QK_SCALE = HEAD_DIM ** -0.5
LOG2E = 1.4426950408889634
ROPE_PAIR = HEAD_DIM // 4

LANE = 128
BF16_SUBLANES = 16
V7X_VMEM_BYTES = 64 * 1024 * 1024
VMEM_LIMIT = V7X_VMEM_BYTES - 1 * 1024 * 1024

ATTN_Q_W = ATTN_HEADS * HEAD_DIM
ATTN_KV_W = ATTN_KV_HEADS * HEAD_DIM
MLSTM_QK_W = MLSTM_HEADS * MLSTM_QK_DIM
MLSTM_V_W = MLSTM_HEADS * MLSTM_V_DIM
C_ROWS = MLSTM_V_DIM + BF16_SUBLANES
NORM_ROWS = 128
X1_LOOKAHEAD = 3

NEG = -1e30


def _params(sem):
    return pltpu.CompilerParams(dimension_semantics=sem, vmem_limit_bytes=VMEM_LIMIT)


def _sigmoid(x):
    return 1.0 / (1.0 + jnp.exp(-x))


def _adaln_kernel(c_ref, w_ref, b_ref, o_ref):
    cc = c_ref[...]
    s = (cc * _sigmoid(cc)).astype(BF16)
    o_ref[...] = jnp.dot(s, w_ref[...].astype(BF16), preferred_element_type=F32) + b_ref[...]


def _adaln(cvecs, w, b):
    R, D = cvecs.shape
    N = w.shape[1]
    tn = 1024
    return pl.pallas_call(
        _adaln_kernel,
        out_shape=jax.ShapeDtypeStruct((R, N), F32),
        grid=(N // tn,),
        in_specs=[pl.BlockSpec((R, D), lambda j: (0, 0)),
                  pl.BlockSpec((D, tn), lambda j: (0, j)),
                  pl.BlockSpec((1, tn), lambda j: (0, j))],
        out_specs=pl.BlockSpec((R, tn), lambda j: (0, j)),
        compiler_params=_params(("arbitrary",)),
        name="adaln",
    )(cvecs, w, b.reshape(1, N))


def _tile_groups(kinds, per):
    tiles = [tuple(kinds[t * per:(t + 1) * per]) for t in range(len(kinds) // per)]
    groups = []
    for t, tk in enumerate(tiles):
        if groups and groups[-1][2] == tk:
            groups[-1] = (groups[-1][0], t + 1, tk)
        else:
            groups.append((t, t + 1, tk))
    return tuple(groups)


def _snake(tile, step, nsteps):
    return jnp.where(tile % 2 == 0, step, nsteps - 1 - step)


def _inproj_kernel(*refs, t_groups, nn, rope, n_i, n_j):
    if rope:
        (x_ref, shift_ref, scale_ref, g_ref, wn_ref, wt_ref, wg_ref, cos_ref, sin_lo_ref, sin_hi_ref,
         p_ref, pt_ref, gt_ref, h_scr) = refs
    else:
        x_ref, shift_ref, scale_ref, g_ref, wn_ref, wt_ref, wg_ref, p_ref, pt_ref, gt_ref, h_scr = refs
    j = _snake(pl.program_id(0) * n_i + pl.program_id(1), pl.program_id(2), n_j)
    nt_dims = (((1,), (1,)), ((), ()))

    @pl.when(pl.program_id(2) == 0)
    def _():
        tm = h_scr.shape[0]
        rc = min(tm, NORM_ROWS)
        for r0 in range(0, tm, rc):
            xf = x_ref[0, r0:r0 + rc, :]
            ms = jnp.mean(xf * xf, axis=-1, keepdims=True)
            y = xf * lax.rsqrt(ms + NORM_EPS) * g_ref[...]
            h_scr[r0:r0 + rc, :] = (y * (1.0 + scale_ref[0]) + shift_ref[0]).astype(BF16)
        gt_ref[0] = lax.dot_general(wg_ref[...], h_scr[...], nt_dims, preferred_element_type=F32)

    @pl.when(j < nn)
    def _():
        p_ref[0] = jnp.dot(h_scr[...], wn_ref[...], preferred_element_type=F32).astype(BF16)

    for lo, hi, kinds in t_groups:
        @pl.when((j >= nn + lo) & (j < nn + hi))
        def _(kinds=kinds):
            acc = lax.dot_general(wt_ref[...], h_scr[...], nt_dims, preferred_element_type=F32)
            for u, kind in enumerate(kinds):
                a = acc[u * LANE:(u + 1) * LANE]
                if "rope" in kind:
                    a = (a * cos_ref[...] + pltpu.roll(a, HEAD_DIM - ROPE_PAIR, 0) * sin_lo_ref[...]
                         + pltpu.roll(a, ROPE_PAIR, 0) * sin_hi_ref[...])
                if "scale" in kind:
                    a = a * (QK_SCALE * LOG2E if "log2" in kind else QK_SCALE)
                pt_ref[0, u * LANE:(u + 1) * LANE, :] = a.astype(BF16)


def _inproj(x, shift, scale, gain, w_nat, w_t, w_g, kinds_t, rope_tabs, tm, tn):
    B, T, D = x.shape
    n_nat = w_nat.shape[1]
    n_t = w_t.shape[0]
    nn, ntt = n_nat // tn, n_t // tn
    per = tn // LANE
    n_i = T // tm
    bm = shift.shape[0]
    rope = rope_tabs is not None

    n_j = nn + ntt

    def next_tile(b, i, j):
        t = jnp.minimum(b * n_i + i + (j >= 1), B * n_i - 1)
        return t // n_i, t % n_i

    def mod_map(b, i, j):
        return (next_tile(b, i, j)[0] if bm == B else 0, 0, 0)

    def wtile(b, i, j):
        return _snake(b * n_i + i, j, n_j)

    in_specs = [pl.BlockSpec((1, tm, D), lambda b, i, j: next_tile(b, i, j) + (0,)),
                pl.BlockSpec((1, 1, D), mod_map),
                pl.BlockSpec((1, 1, D), mod_map),
                pl.BlockSpec((1, D), lambda b, i, j: (0, 0)),
                pl.BlockSpec((D, tn), lambda b, i, j: (0, jnp.minimum(wtile(b, i, j), nn - 1))),
                pl.BlockSpec((tn, D), lambda b, i, j: (jnp.maximum(wtile(b, i, j) - nn, 0), 0)),
                pl.BlockSpec(w_g.shape, lambda b, i, j: (0, 0))]
    args = [x, shift, scale, gain.reshape(1, D), w_nat, w_t, w_g]
    if rope:
        in_specs += [pl.BlockSpec((HEAD_DIM, tm), lambda b, i, j: (0, i))] * len(rope_tabs)
        args += list(rope_tabs)
    return pl.pallas_call(
        functools.partial(_inproj_kernel, t_groups=_tile_groups(kinds_t, per), nn=nn, rope=rope, n_i=n_i,
                          n_j=n_j),
        out_shape=(jax.ShapeDtypeStruct((B, T, n_nat), BF16),
                   jax.ShapeDtypeStruct((B, n_t, T), BF16),
                   jax.ShapeDtypeStruct((B, w_g.shape[0], T), F32)),
        grid=(B, n_i, n_j),
        in_specs=in_specs,
        out_specs=(pl.BlockSpec((1, tm, tn), lambda b, i, j: (b, i, jnp.minimum(wtile(b, i, j), nn - 1))),
                   pl.BlockSpec((1, tn, tm), lambda b, i, j: (b, jnp.maximum(wtile(b, i, j) - nn, 0), i)),
                   pl.BlockSpec((1, w_g.shape[0], tm), lambda b, i, j: (b, 0, i))),
        scratch_shapes=[pltpu.VMEM((tm, D), BF16)],
        compiler_params=_params(("arbitrary", "arbitrary", "arbitrary")),
        name="inproj_rope" if rope else "inproj_ctx",
    )(*args)


def _scan_lanes(x, op, reverse, fill):
    lane = lax.broadcasted_iota(jnp.int32, x.shape, 1)
    k = 1
    while k < CHUNK:
        if reverse:
            sh = jnp.where(lane < CHUNK - k, pltpu.roll(x, CHUNK - k, 1), fill)
        else:
            sh = jnp.where(lane >= k, pltpu.roll(x, k, 1), fill)
        x = op(x, sh)
        k *= 2
    return x


def _log_sigmoid(z):
    return jnp.minimum(z, 0.0) - jnp.log(1.0 + jnp.exp(-jnp.abs(z)))


def _lane_value(x, lane_idx):
    lane = lax.broadcasted_iota(jnp.int32, x.shape, 1)
    return jnp.broadcast_to(jnp.sum(jnp.where(lane == lane_idx, x, 0.0), axis=1, keepdims=True), x.shape)


def _gate_prep_kernel(gt_ref, bias_ref, row_ref, col_ref, *, nchunk):
    H = MLSTM_HEADS
    for c in range(nchunk):
        sl = slice(c * CHUNK, (c + 1) * CHUNK)
        z = gt_ref[0, :, sl] + bias_ref[...]
        li_f, lf_f = z[0:H] * LOG2E, _log_sigmoid(z[H:2 * H]) * LOG2E
        li_r, lf_r = z[2 * H:3 * H] * LOG2E, _log_sigmoid(z[3 * H:4 * H]) * LOG2E
        b_f = _scan_lanes(lf_f, jnp.add, False, 0.0)
        b_r = _scan_lanes(lf_r, jnp.add, True, 0.0)
        u_f = li_f - b_f
        u_r = li_r - b_r
        r_f = _scan_lanes(u_f, jnp.maximum, False, -jnp.inf)
        r_r = _scan_lanes(u_r, jnp.maximum, True, -jnp.inf)
        ends = [_lane_value(r_f, CHUNK - 1), _lane_value(r_r, 0), _lane_value(b_f, CHUNK - 1), _lane_value(b_r, 0)]
        for k, v in enumerate([r_f, r_r, b_f, b_r] + ends):
            row_ref[0, k * H:(k + 1) * H, sl] = v
        stack = jnp.concatenate([u_f, u_r, jnp.zeros((LANE - 2 * H, CHUNK), F32)], axis=0)
        col_ref[0, sl, :] = stack.T


def _gate_prep(g_t, bias):
    B, G, T = g_t.shape
    tg = min(T, 8 * CHUNK)
    H = MLSTM_HEADS
    return pl.pallas_call(
        functools.partial(_gate_prep_kernel, nchunk=tg // CHUNK),
        out_shape=(jax.ShapeDtypeStruct((B, 8 * H, T), F32),
                   jax.ShapeDtypeStruct((B, T, LANE), F32)),
        grid=(B, T // tg),
        in_specs=[pl.BlockSpec((1, G, tg), lambda b, i: (b, 0, i)),
                  pl.BlockSpec((G, CHUNK), lambda b, i: (0, 0))],
        out_specs=(pl.BlockSpec((1, 8 * H, tg), lambda b, i: (b, 0, i)),
                   pl.BlockSpec((1, tg, LANE), lambda b, i: (b, i, 0))),
        compiler_params=_params(("parallel", "parallel")),
        name="gate_prep",
    )(g_t, jnp.broadcast_to(bias.reshape(G, 1), (G, CHUNK)))


def _attn_kernel(sink_ref, qt_ref, kp_ref, km_ref, kn_ref, vp_ref, vm_ref, vn_ref, kx_ref, vx_ref, band_ref,
                 o_ref, *, nb, qb):
    cols = ATTN_GROUP * WINDOW
    row = lax.broadcasted_iota(jnp.int32, (3 * WINDOW, WINDOW), 0)
    lane = lax.broadcasted_iota(jnp.int32, (1, cols), 1)
    tn_dims = (((0,), (0,)), ((), ()))

    def key_blocks(prev_ref, main_ref, next_ref, hs, q):
        blk = lambda i: main_ref[0, hs, i * WINDOW:(i + 1) * WINDOW]
        return [prev_ref[0, hs, :] if q == 0 else blk(q - 1), blk(q),
                next_ref[0, hs, :] if q == qb - 1 else blk(q + 1)]

    for q in range(qb):
        j = pl.program_id(1) * qb + q
        qs = slice(q * WINDOW, (q + 1) * WINDOW)
        prev_bias = jnp.where(j > 0, 0.0, NEG)
        next_bias = jnp.where(j < nb - 1, 0.0, NEG)
        edge = jnp.where(row < WINDOW, prev_bias, jnp.where(row >= 2 * WINDOW, next_bias, 0.0))
        bias = band_ref[...] + jnp.concatenate([edge] * ATTN_GROUP, axis=1)
        for g in range(ATTN_KV_HEADS):
            hs = slice(g * HEAD_DIM, (g + 1) * HEAD_DIM)
            qt = jnp.concatenate([qt_ref[0, (g * ATTN_GROUP + h) * HEAD_DIM:(g * ATTN_GROUP + h + 1) * HEAD_DIM, qs]
                                  for h in range(ATTN_GROUP)], axis=1)
            kt = jnp.concatenate(key_blocks(kp_ref, km_ref, kn_ref, hs, q) + [kx_ref[0, hs, :]], axis=1)
            vt = jnp.concatenate(key_blocks(vp_ref, vm_ref, vn_ref, hs, q) + [vx_ref[0, hs, :]], axis=1)
            st = lax.dot_general(kt, qt, tn_dims, preferred_element_type=F32)
            s_loc = st[:3 * WINDOW] + bias
            s_ctx = st[3 * WINDOW:]
            sink = jnp.full((1, cols), sink_ref[g * ATTN_GROUP] * LOG2E, F32)
            for h in range(1, ATTN_GROUP):
                sink = jnp.where(lane >= h * WINDOW, sink_ref[g * ATTN_GROUP + h] * LOG2E, sink)
            m = jnp.maximum(jnp.maximum(jnp.max(s_loc, axis=0, keepdims=True),
                                        jnp.max(s_ctx, axis=0, keepdims=True)), sink)
            p_loc = jnp.exp2(s_loc - m)
            p_ctx = jnp.exp2(s_ctx - m)
            den = (jnp.sum(p_loc, axis=0, keepdims=True) + jnp.sum(p_ctx, axis=0, keepdims=True)
                   + jnp.exp2(sink - m))
            pt = jnp.concatenate([p_loc, p_ctx], axis=0).astype(BF16)
            ot = jnp.dot(vt, pt, preferred_element_type=F32) * (1.0 / den)
            for h in range(ATTN_GROUP):
                r0 = (g * ATTN_GROUP + h) * HEAD_DIM
                o_ref[0, r0:r0 + HEAD_DIM, qs] = ot[:, h * WINDOW:(h + 1) * WINDOW].astype(BF16)


def _attention(pt_lat, pt_ctx, sink, q_blk, k_blk, v_blk, kx_blk, vx_blk, qb):
    B, _, S = pt_lat.shape
    C = pt_ctx.shape[2]
    nb = S // WINDOW
    assert nb % qb == 0
    cols = ATTN_GROUP * WINDOW
    t = np.arange(cols)[None, :] % WINDOW
    d = np.arange(3 * WINDOW)[:, None] - t
    band = jnp.asarray(np.where((d >= 0) & (d <= 2 * WINDOW), 0.0, NEG), F32)

    def kspecs(blk):
        edge = lambda off: pl.BlockSpec((1, ATTN_KV_W, WINDOW),
                                        lambda b, j: (b, blk, jnp.clip(j * qb + off, 0, nb - 1)))
        return [edge(-1), pl.BlockSpec((1, ATTN_KV_W, qb * WINDOW), lambda b, j: (b, blk, j)), edge(qb)]

    return pl.pallas_call(
        functools.partial(_attn_kernel, nb=nb, qb=qb),
        out_shape=jax.ShapeDtypeStruct((B, ATTN_Q_W, S), BF16),
        grid=(B, nb // qb),
        in_specs=[pl.BlockSpec(memory_space=pltpu.SMEM),
                  pl.BlockSpec((1, ATTN_Q_W, qb * WINDOW), lambda b, j: (b, q_blk, j))]
                 + kspecs(k_blk) + kspecs(v_blk)
                 + [pl.BlockSpec((1, ATTN_KV_W, C), lambda b, j: (b, kx_blk, 0)),
                    pl.BlockSpec((1, ATTN_KV_W, C), lambda b, j: (b, vx_blk, 0)),
                    pl.BlockSpec((3 * WINDOW, cols), lambda b, j: (0, 0))],
        out_specs=pl.BlockSpec((1, ATTN_Q_W, qb * WINDOW), lambda b, j: (b, 0, j)),
        compiler_params=_params(("parallel", "arbitrary")),
        name="window_attn",
    )(sink, pt_lat, *([pt_lat] * 6), pt_ctx, pt_ctx, band)


def _v_aug_t(vt):
    sub = lax.broadcasted_iota(jnp.int32, (BF16_SUBLANES, vt.shape[1]), 0)
    return jnp.concatenate([vt, jnp.where(sub == 0, 1.0, 0.0).astype(BF16)], axis=0)


def _state_update(ct_old, m_old, k, vaug_t, u_bc, r_end, b_end):
    m_end = jnp.maximum(m_old, r_end)
    ks = (k.astype(F32) * jnp.exp2(u_bc - m_end)).astype(BF16)
    ct_new = jnp.exp2(m_old - m_end) * ct_old + jnp.dot(vaug_t, ks, preferred_element_type=F32)
    return ct_new, b_end + m_end


def _row(row_ref, k, h, ts):
    i = k * MLSTM_HEADS + h
    return row_ref[0, i:i + 1, ts]


def _scan_kernel(k_ref, vt_ref, row_ref, col_ref, c0_ref, m0_ref, cs_ref, ms_ref, cf_ref, mf_ref,
                 c_scr, m_scr, *, reverse, nsteps, cps):
    H = MLSTM_HEADS
    n = pl.program_id(1)
    d = 1 if reverse else 0

    @pl.when(n == 0)
    def _():
        c_scr[...] = c0_ref[0]
        m_scr[...] = m0_ref[0]

    for c in (range(cps - 1, -1, -1) if reverse else range(cps)):
        ts = slice(c * CHUNK, (c + 1) * CHUNK)
        cols = col_ref[0, ts, :]
        for h in range(H):
            ct_old = c_scr[h]
            m_old = m_scr[h:h + 1, :]
            cs_ref[0, c, h] = ct_old.astype(BF16)
            ms_ref[0, c, h:h + 1, :] = m_old
            k = k_ref[0, ts, h * MLSTM_QK_DIM:(h + 1) * MLSTM_QK_DIM]
            vaug_t = _v_aug_t(vt_ref[0, h * MLSTM_V_DIM:(h + 1) * MLSTM_V_DIM, ts])
            u_bc = jnp.broadcast_to(cols[:, d * H + h:d * H + h + 1], (CHUNK, LANE))
            ct_new, m_new = _state_update(ct_old, m_old, k, vaug_t, u_bc, _row(row_ref, 4 + d, h, ts),
                                          _row(row_ref, 6 + d, h, ts))
            c_scr[h] = ct_new
            m_scr[h:h + 1, :] = m_new

    @pl.when(n == nsteps - 1)
    def _():
        cf_ref[0] = c_scr[...]
        mf_ref[0] = m_scr[...]


def _state_scan(p, k_blk, p_t, vt_blk, rows, cols, c0, m0, reverse, cps):
    B, T, _ = p.shape
    nc = T // CHUNK
    cps = min(cps, nc)
    ns = nc // cps
    assert ns * cps == nc
    H = MLSTM_HEADS
    tb = cps * CHUNK
    cidx = (lambda n: ns - 1 - n) if reverse else (lambda n: n)
    st = (H, C_ROWS, MLSTM_QK_DIM)
    return pl.pallas_call(
        functools.partial(_scan_kernel, reverse=reverse, nsteps=ns, cps=cps),
        out_shape=(jax.ShapeDtypeStruct((B, nc) + st, BF16),
                   jax.ShapeDtypeStruct((B, nc, H, LANE), F32),
                   jax.ShapeDtypeStruct((B,) + st, F32),
                   jax.ShapeDtypeStruct((B, H, LANE), F32)),
        grid=(B, ns),
        in_specs=[pl.BlockSpec((1, tb, MLSTM_QK_W), lambda b, n: (b, cidx(n), k_blk)),
                  pl.BlockSpec((1, MLSTM_V_W, tb), lambda b, n: (b, vt_blk, cidx(n))),
                  pl.BlockSpec((1, 8 * H, tb), lambda b, n: (b, 0, cidx(n))),
                  pl.BlockSpec((1, tb, LANE), lambda b, n: (b, cidx(n), 0)),
                  pl.BlockSpec((1,) + st, lambda b, n: (b, 0, 0, 0)),
                  pl.BlockSpec((1, H, LANE), lambda b, n: (b, 0, 0))],
        out_specs=(pl.BlockSpec((1, cps) + st, lambda b, n: (b, cidx(n), 0, 0, 0)),
                   pl.BlockSpec((1, cps, H, LANE), lambda b, n: (b, cidx(n), 0, 0)),
                   pl.BlockSpec((1,) + st, lambda b, n: (b, 0, 0, 0)),
                   pl.BlockSpec((1, H, LANE), lambda b, n: (b, 0, 0))),
        scratch_shapes=[pltpu.VMEM(st, F32), pltpu.VMEM((H, LANE), F32)],
        compiler_params=_params(("parallel", "arbitrary")),
        name="mlstm_scan_rev" if reverse else "mlstm_scan_fwd",
    )(p, p_t, rows, cols, c0, m0)


def _mlstm_out_kernel(k_ref, qt_ref, vt_ref, mot_ref, row_ref, col_ref, cr_ref, mr_ref, c0_ref, m0_ref, g_ref,
                      o_ref, c_scr, m_scr, *, cps):
    H = MLSTM_HEADS
    L = CHUNK
    n = pl.program_id(1)

    @pl.when(n == 0)
    def _():
        c_scr[...] = c0_ref[0]
        m_scr[...] = m0_ref[0]

    si = lax.broadcasted_iota(jnp.int32, (L, L), 0)
    ti = lax.broadcasted_iota(jnp.int32, (L, L), 1)
    for c in range(cps):
        ts = slice(c * L, (c + 1) * L)
        cols = col_ref[0, ts, :]
        for h in range(H):
            qt = qt_ref[0, h * MLSTM_QK_DIM:(h + 1) * MLSTM_QK_DIM, ts]
            k = k_ref[0, ts, h * MLSTM_QK_DIM:(h + 1) * MLSTM_QK_DIM]
            vs = slice(h * MLSTM_V_DIM, (h + 1) * MLSTM_V_DIM)
            vaug_t = _v_aug_t(vt_ref[0, vs, ts])
            pt = jnp.dot(k, qt, preferred_element_type=F32)
            qt_f = qt.astype(F32)
            ct_f = c_scr[h]
            hs_t = None
            for d in range(N_DIRS):
                u_bc = jnp.broadcast_to(cols[:, d * H + h:d * H + h + 1], (L, LANE))
                m0 = m_scr[h:h + 1, :] if d == 0 else mr_ref[0, c, h:h + 1, :]
                ct = ct_f.astype(BF16) if d == 0 else cr_ref[0, c, h]
                mx = jnp.maximum(_row(row_ref, d, h, ts), m0)
                valid = (si <= ti) if d == 0 else (si >= ti)
                s_t = (pt * jnp.exp2(jnp.where(valid, u_bc - mx, -jnp.inf))).astype(BF16)
                q_in = (qt_f * jnp.exp2(m0 - mx)).astype(BF16)
                tot = jnp.dot(jnp.concatenate([vaug_t, ct], axis=1), jnp.concatenate([s_t, q_in], axis=0),
                              preferred_element_type=F32)
                den = jnp.maximum(jnp.abs(tot[MLSTM_V_DIM:MLSTM_V_DIM + 1]),
                                  jnp.exp2(-_row(row_ref, 2 + d, h, ts) - mx))
                part = tot[:MLSTM_V_DIM] * (1.0 / den)
                hs_t = part if hs_t is None else hs_t + part
                if d == 0:
                    u_bc_f = u_bc
            hn = hs_t * lax.rsqrt(jnp.mean(hs_t * hs_t, axis=0, keepdims=True) + NORM_EPS) * g_ref[vs, :]
            o_ref[0, vs, ts] = (_sigmoid(mot_ref[0, vs, ts].astype(F32)) * hn).astype(BF16)
            ct_new, m_new = _state_update(ct_f, m_scr[h:h + 1, :], k, vaug_t, u_bc_f, _row(row_ref, 4, h, ts),
                                          _row(row_ref, 6, h, ts))
            c_scr[h] = ct_new
            m_scr[h:h + 1, :] = m_new


def _mlstm_out(p, k_blk, p_t, qt_blk, vt_blk, mot_blk, rows, cols, c_rev, m_rev, c0, m0, gain, cps):
    B, S, _ = p.shape
    nc = S // CHUNK
    ns = nc // cps
    assert ns * cps == nc
    H = MLSTM_HEADS
    tb = cps * CHUNK
    st = (H, C_ROWS, MLSTM_QK_DIM)
    gain_bc = jnp.broadcast_to(gain.reshape(MLSTM_V_W, 1), (MLSTM_V_W, LANE))
    return pl.pallas_call(
        functools.partial(_mlstm_out_kernel, cps=cps),
        out_shape=jax.ShapeDtypeStruct((B, MLSTM_V_W, S), BF16),
        grid=(B, ns),
        in_specs=[pl.BlockSpec((1, tb, MLSTM_QK_W), lambda b, n: (b, n, k_blk)),
                  pl.BlockSpec((1, MLSTM_QK_W, tb), lambda b, n: (b, qt_blk, n)),
                  pl.BlockSpec((1, MLSTM_V_W, tb), lambda b, n: (b, vt_blk, n)),
                  pl.BlockSpec((1, MLSTM_V_W, tb), lambda b, n: (b, mot_blk, n)),
                  pl.BlockSpec((1, 8 * H, tb), lambda b, n: (b, 0, n)),
                  pl.BlockSpec((1, tb, LANE), lambda b, n: (b, n, 0)),
                  pl.BlockSpec((1, cps) + st, lambda b, n: (b, n, 0, 0, 0)),
                  pl.BlockSpec((1, cps, H, LANE), lambda b, n: (b, n, 0, 0)),
                  pl.BlockSpec((1,) + st, lambda b, n: (b, 0, 0, 0)),
                  pl.BlockSpec((1, H, LANE), lambda b, n: (b, 0, 0)),
                  pl.BlockSpec((MLSTM_V_W, LANE), lambda b, n: (0, 0))],
        out_specs=pl.BlockSpec((1, MLSTM_V_W, tb), lambda b, n: (b, 0, n)),
        scratch_shapes=[pltpu.VMEM(st, F32), pltpu.VMEM((H, LANE), F32)],
        compiler_params=_params(("parallel", "arbitrary")),
        name="mlstm_out",
    )(p, p_t, p_t, p_t, rows, cols, c_rev, m_rev, c0, m0, gain_bc)


def _merge_kernel(attt_ref, memt_ref, ga_ref, gm_ref, x_ref, gate_ref, shift_ref, scale_ref, g2_ref,
                  wa_ref, wm_ref, wo_ref, o_ref, h_ref):
    tn_dims = (((0,), (0,)), ((), ()))
    a = lax.dot_general(attt_ref[0], wa_ref[...], tn_dims, preferred_element_type=F32)
    m = lax.dot_general(memt_ref[0], wm_ref[...], tn_dims, preferred_element_type=F32)
    y = _sigmoid(ga_ref[0].astype(F32)) * a + _sigmoid(gm_ref[0].astype(F32)) * m
    z = jnp.dot(y.astype(BF16), wo_ref[...], preferred_element_type=F32)
    x1 = x_ref[0] + gate_ref[0] * z
    o_ref[0] = x1
    ms = jnp.mean(x1 * x1, axis=-1, keepdims=True)
    h_ref[0] = (x1 * lax.rsqrt(ms + NORM_EPS) * g2_ref[...] * (1.0 + scale_ref[0]) + shift_ref[0]).astype(BF16)


def _merge(att_t, mem_t, p, ga_blk, gm_blk, x, gate1, shift2, scale2, norm2_g, w_ap, w_mp, w_out, tm):
    B, S, D = x.shape
    resident = functools.partial(pl.BlockSpec, pipeline_mode=pl.Buffered(1))
    mod = pl.BlockSpec((1, 1, D), lambda b, i: (b, 0, 0))
    return pl.pallas_call(
        _merge_kernel,
        out_shape=(jax.ShapeDtypeStruct((B, S, D), F32), jax.ShapeDtypeStruct((B, S, D), BF16)),
        grid=(B, S // tm),
        in_specs=[pl.BlockSpec((1, ATTN_Q_W, tm), lambda b, i: (b, 0, i)),
                  pl.BlockSpec((1, MLSTM_V_W, tm), lambda b, i: (b, 0, i)),
                  pl.BlockSpec((1, tm, D), lambda b, i: (b, i, ga_blk)),
                  pl.BlockSpec((1, tm, D), lambda b, i: (b, i, gm_blk)),
                  pl.BlockSpec((1, tm, D), lambda b, i: (b, i, 0)),
                  mod, mod, mod,
                  pl.BlockSpec((1, D), lambda b, i: (0, 0)),
                  resident(w_ap.shape, lambda b, i: (0, 0)),
                  resident(w_mp.shape, lambda b, i: (0, 0)),
                  resident(w_out.shape, lambda b, i: (0, 0))],
        out_specs=(pl.BlockSpec((1, tm, D), lambda b, i: (b, i, 0)),
                   pl.BlockSpec((1, tm, D), lambda b, i: (b, i, 0))),
        compiler_params=_params(("parallel", "parallel")),
        name="merge_outproj",
    )(att_t, mem_t, p, p, x, gate1, shift2, scale2, norm2_g.reshape(1, D), w_ap, w_mp, w_out)


def _ffn_kernel(x1_hbm, h_ref, gate_ref, wg_ref, wu_ref, wo_ref, gf_ref, o_ref, x1_buf, x1_sem, *, nf, tm):
    b, i, f = pl.program_id(0), pl.program_id(1), pl.program_id(2)
    rc = min(tm, NORM_ROWS)
    x1_copy = pltpu.make_async_copy(x1_hbm.at[b, pl.ds(pl.multiple_of(i * tm, tm), tm), :], x1_buf, x1_sem)

    @pl.when(f == max(nf - 1 - X1_LOOKAHEAD, 0))
    def _():
        x1_copy.start()

    def d_ff_tile(first):
        h = h_ref[0]
        gt = jnp.dot(h, wg_ref[...], preferred_element_type=F32)
        up = jnp.dot(h, wu_ref[...], preferred_element_type=F32)
        act = (gt * _sigmoid(gt) * up).astype(BF16)
        nc = wo_ref.shape[0]
        for c0 in range(0, o_ref.shape[2], nc):
            part = jnp.dot(act, wo_ref[:, c0:c0 + nc], preferred_element_type=F32)
            if first:
                o_ref[0, :, c0:c0 + nc] = part
            else:
                o_ref[0, :, c0:c0 + nc] += part

    pl.when(f == 0)(functools.partial(d_ff_tile, True))
    pl.when(f > 0)(functools.partial(d_ff_tile, False))

    @pl.when(f == nf - 1)
    def _():
        x1_copy.wait()
        for r0 in range(0, tm, rc):
            x2 = x1_buf[r0:r0 + rc, :] + gate_ref[0] * o_ref[0, r0:r0 + rc, :]
            ms = jnp.mean(x2 * x2, axis=-1, keepdims=True)
            o_ref[0, r0:r0 + rc, :] = x2 * lax.rsqrt(ms + NORM_EPS) * gf_ref[...]


def _ffn(x1, h2, gate2, w_in, w_out, final_g, tm, tf):
    B, S, D = x1.shape
    dff = w_out.shape[0]
    nf = dff // tf
    n_i = S // tm

    def ft(b, i, f):
        return _snake(b * n_i + i, f, nf)

    return pl.pallas_call(
        functools.partial(_ffn_kernel, nf=nf, tm=tm),
        out_shape=jax.ShapeDtypeStruct((B, S, D), F32),
        grid=(B, S // tm, nf),
        in_specs=[pl.BlockSpec(memory_space=pl.ANY),
                  pl.BlockSpec((1, tm, D), lambda b, i, f: (b, i, 0)),
                  pl.BlockSpec((1, 1, D), lambda b, i, f: (b, 0, 0)),
                  pl.BlockSpec((D, tf), lambda b, i, f: (0, ft(b, i, f))),
                  pl.BlockSpec((D, tf), lambda b, i, f: (0, nf + ft(b, i, f))),
                  pl.BlockSpec((tf, D), lambda b, i, f: (ft(b, i, f), 0)),
                  pl.BlockSpec((1, D), lambda b, i, f: (0, 0))],
        out_specs=pl.BlockSpec((1, tm, D), lambda b, i, f: (b, i, 0)),
        scratch_shapes=[pltpu.VMEM((tm, D), F32), pltpu.SemaphoreType.DMA(())],
        compiler_params=_params(("arbitrary", "arbitrary", "arbitrary")),
        name="ffn_final_norm",
    )(x1, h2, gate2, w_in, w_in, w_out, final_g.reshape(1, D))


def _rope_tables(S):
    pos = jnp.arange(S)
    rows = (pos // GRID_W).astype(F32)
    cols = (pos % GRID_W).astype(F32)
    inv_freq = ROPE_BASE ** (-jnp.arange(ROPE_PAIR, dtype=F32) / ROPE_PAIR)
    ar = rows[:, None] * inv_freq[None, :]
    ac = cols[:, None] * inv_freq[None, :]
    zero = jnp.zeros_like(ar)
    cos = jnp.concatenate([jnp.cos(ar), jnp.cos(ar), jnp.cos(ac), jnp.cos(ac)], axis=1)
    sin_lo = jnp.concatenate([-jnp.sin(ar), zero, -jnp.sin(ac), zero], axis=1)
    sin_hi = jnp.concatenate([zero, jnp.sin(ar), zero, jnp.sin(ac)], axis=1)
    return cos.T, sin_lo.T, sin_hi.T


def kernel(x, c, ctx, c_ctx, w_ada, b_ada, norm1_g, w_in, b_gates, attn_sink, mlstm_norm_g, w_attn_proj,
           w_mlstm_proj, w_out, norm2_g, w_ffn_in, w_ffn_out, final_norm_g):
    B, S, D = x.shape
    C = ctx.shape[1]
    assert w_ada.shape[0] == 1, "single-layer configuration"
    assert S % 512 == 0 and C % CHUNK == 0 and S % GRID_W == 0
    H = MLSTM_HEADS
    tn = 1024

    rows = -(-(B + 1) // 8) * 8
    cvecs = jnp.concatenate([c, c_ctx[None], jnp.zeros((rows - B - 1, D), F32)], axis=0)
    mod = _adaln(cvecs, w_ada[0], b_ada[0])
    shift1, scale1, gate1, shift2, scale2, gate2 = [mod[:B, k * D:(k + 1) * D].reshape(B, 1, D) for k in range(N_MOD)]
    shift_c = mod[B:B + 1, 0:D].reshape(1, 1, D)
    scale_c = mod[B:B + 1, D:2 * D].reshape(1, 1, D)

    wi = w_in[0]
    o = 0
    parts = {}
    for name, width in (("a_k", ATTN_KV_W), ("a_v", ATTN_KV_W), ("m_k", MLSTM_QK_W), ("m_v", MLSTM_V_W),
                        ("m_g", N_GATE), ("a_q", ATTN_Q_W), ("m_q", MLSTM_QK_W), ("m_o", MLSTM_V_W),
                        ("g_att", D), ("g_mem", D)):
        parts[name] = wi[:, o:o + width]
        o += width
    w_nat = jnp.concatenate([parts["g_att"], parts["g_mem"], parts["m_k"]], axis=1).astype(BF16)
    w_t = jnp.concatenate([parts["m_v"], parts["m_o"], parts["a_q"], parts["m_q"], parts["a_k"], parts["a_v"]],
                          axis=1).T.astype(BF16)
    w_g = parts["m_g"].T.astype(BF16)
    kinds_t = ([""] * (2 * MLSTM_V_W // LANE) + ["rope scale log2"] * (ATTN_Q_W // LANE)
               + ["scale"] * (MLSTM_QK_W // LANE) + ["rope"] * (ATTN_KV_W // LANE) + [""] * (ATTN_KV_W // LANE))
    assert (2 * D) % MLSTM_QK_W == 0 and (2 * MLSTM_V_W) % ATTN_Q_W == 0
    ga_blk, gm_blk, mk_blk = 0, 1, 2 * D // MLSTM_QK_W
    vt_blk, mot_blk = 0, 1
    aq_blk = 2 * MLSTM_V_W // ATTN_Q_W
    qt_blk = (2 * MLSTM_V_W + ATTN_Q_W) // MLSTM_QK_W
    kt_blk = (2 * MLSTM_V_W + ATTN_Q_W + MLSTM_QK_W) // ATTN_KV_W
    vtt_blk = kt_blk + 1
    w_nat_c = parts["m_k"].astype(BF16)
    w_t_c = jnp.concatenate([parts["m_v"], parts["a_k"], parts["a_v"]], axis=1).T.astype(BF16)
    kx_blk, vx_blk = MLSTM_V_W // ATTN_KV_W, MLSTM_V_W // ATTN_KV_W + 1

    p_lat, pt_lat, gt_lat = _inproj(x, shift1, scale1, norm1_g[0], w_nat, w_t, w_g, kinds_t,
                                    _rope_tables(S), tm=1024 if S % 1024 == 0 else 512, tn=tn)
    p_ctx, pt_ctx, gt_ctx = _inproj(ctx, shift_c, scale_c, norm1_g[0], w_nat_c, w_t_c, w_g,
                                    [""] * (w_t_c.shape[0] // LANE), None, tm=min(C, 256), tn=tn)

    att_t = _attention(pt_lat, pt_ctx, attn_sink[0], aq_blk, kt_blk, vtt_blk, kx_blk, vx_blk,
                       qb=8 if (S // WINDOW) % 8 == 0 else 1)

    rows_lat, cols_lat = _gate_prep(gt_lat, b_gates[0])
    rows_ctx, cols_ctx = _gate_prep(gt_ctx, b_gates[0])
    c_zero = jnp.zeros((B, H, C_ROWS, MLSTM_QK_DIM), F32)
    m_zero = jnp.zeros((B, H, LANE), F32)
    _, _, cf_ctx, mf_ctx = _state_scan(p_ctx, 0, pt_ctx, 0, rows_ctx, cols_ctx, c_zero, m_zero, reverse=False,
                                       cps=4)
    _, _, cr_ctx, mr_ctx = _state_scan(p_ctx, 0, pt_ctx, 0, rows_ctx, cols_ctx, c_zero, m_zero, reverse=True,
                                       cps=4)
    c_rev, m_rev, _, _ = _state_scan(p_lat, mk_blk, pt_lat, vt_blk, rows_lat, cols_lat, cr_ctx, mr_ctx,
                                     reverse=True, cps=16)
    mem_t = _mlstm_out(p_lat, mk_blk, pt_lat, qt_blk, vt_blk, mot_blk, rows_lat, cols_lat, c_rev, m_rev,
                       cf_ctx, mf_ctx, mlstm_norm_g[0], cps=8 if (S // CHUNK) % 8 == 0 else 1)

    x1, h2 = _merge(att_t, mem_t, p_lat, ga_blk, gm_blk, x, gate1, shift2, scale2, norm2_g[0],
                    w_attn_proj[0].astype(BF16), w_mlstm_proj[0].astype(BF16), w_out[0].astype(BF16), tm=256)
    return _ffn(x1, h2, gate2, w_ffn_in[0].astype(BF16), w_ffn_out[0].astype(BF16), final_norm_g,
                tm=1024 if S % 1024 == 0 else 512, tf=512)
```

```python
import functools

import numpy as np
import jax
import jax.numpy as jnp
from jax import lax
from jax.experimental import pallas as pl
from jax.experimental.pallas import tpu as pltpu

F32 = jnp.float32
BF16 = jnp.bfloat16

GRID_W = 64
ATTN_HEADS = 16
ATTN_KV_HEADS = 4
ATTN_GROUP = ATTN_HEADS // ATTN_KV_HEADS
HEAD_DIM = 128
WINDOW = 128
ROPE_BASE = 10000.0
MLSTM_HEADS = 8
MLSTM_QK_DIM = 128
MLSTM_V_DIM = 256
CHUNK = 128
N_DIRS = 2
N_GATE = N_DIRS * 2 * MLSTM_HEADS
NORM_EPS = 1e-6
N_MOD = 6
QK_SCALE = HEAD_DIM ** -0.5
LOG2E = 1.4426950408889634
ROPE_PAIR = HEAD_DIM // 4

LANE = 128
BF16_SUBLANES = 16
V7X_VMEM_BYTES = 64 * 1024 * 1024
VMEM_LIMIT = V7X_VMEM_BYTES - 1 * 1024 * 1024

ATTN_Q_W = ATTN_HEADS * HEAD_DIM
ATTN_KV_W = ATTN_KV_HEADS * HEAD_DIM
MLSTM_QK_W = MLSTM_HEADS * MLSTM_QK_DIM
MLSTM_V_W = MLSTM_HEADS * MLSTM_V_DIM
C_ROWS = MLSTM_V_DIM + BF16_SUBLANES
NORM_ROWS = 128
X1_LOOKAHEAD = 3
MERGE_ROWS = 256
SLAB_BLOCKS = MERGE_ROWS // 128

NEG = -1e30


def _params(sem):
    return pltpu.CompilerParams(dimension_semantics=sem, vmem_limit_bytes=VMEM_LIMIT)


def _sigmoid(x):
    return 1.0 / (1.0 + jnp.exp(-x))


def _adaln_kernel(c_ref, w_ref, b_ref, o_ref):
    cc = c_ref[...]
    s = (cc * _sigmoid(cc)).astype(BF16)
    o_ref[...] = jnp.dot(s, w_ref[...].astype(BF16), preferred_element_type=F32) + b_ref[...]


def _adaln(cvecs, w, b):
    R, D = cvecs.shape
    N = w.shape[1]
    tn = 1024
    return pl.pallas_call(
        _adaln_kernel,
        out_shape=jax.ShapeDtypeStruct((R, N), F32),
        grid=(N // tn,),
        in_specs=[pl.BlockSpec((R, D), lambda j: (0, 0)),
                  pl.BlockSpec((D, tn), lambda j: (0, j)),
                  pl.BlockSpec((1, tn), lambda j: (0, j))],
        out_specs=pl.BlockSpec((R, tn), lambda j: (0, j)),
        compiler_params=_params(("arbitrary",)),
        name="adaln",
    )(cvecs, w, b.reshape(1, N))


def _tile_groups(kinds, per):
    tiles = [tuple(kinds[t * per:(t + 1) * per]) for t in range(len(kinds) // per)]
    groups = []
    for t, tk in enumerate(tiles):
        if groups and groups[-1][2] == tk:
            groups[-1] = (groups[-1][0], t + 1, tk)
        else:
            groups.append((t, t + 1, tk))
    return tuple(groups)


def _snake(tile, step, nsteps):
    return jnp.where(tile % 2 == 0, step, nsteps - 1 - step)


def _inproj_kernel(*refs, t_groups, nn, rope, n_i, n_j):
    if rope:
        (x_ref, shift_ref, scale_ref, g_ref, wn_ref, wt_ref, wg_ref, cos_ref, sin_lo_ref, sin_hi_ref,
         p_ref, pt_ref, gt_ref, h_scr) = refs
    else:
        x_ref, shift_ref, scale_ref, g_ref, wn_ref, wt_ref, wg_ref, p_ref, pt_ref, gt_ref, h_scr = refs
    j = _snake(pl.program_id(0) * n_i + pl.program_id(1), pl.program_id(2), n_j)
    nt_dims = (((1,), (1,)), ((), ()))

    @pl.when(pl.program_id(2) == 0)
    def _():
        tm = h_scr.shape[0]
        rc = min(tm, NORM_ROWS)
        for r0 in range(0, tm, rc):
            xf = x_ref[0, r0:r0 + rc, :]
            ms = jnp.mean(xf * xf, axis=-1, keepdims=True)
            y = xf * lax.rsqrt(ms + NORM_EPS) * g_ref[...]
            h_scr[r0:r0 + rc, :] = (y * (1.0 + scale_ref[0]) + shift_ref[0]).astype(BF16)
        gt_ref[0] = lax.dot_general(wg_ref[...], h_scr[...], nt_dims, preferred_element_type=F32)

    @pl.when(j < nn)
    def _():
        p_ref[0] = jnp.dot(h_scr[...], wn_ref[...], preferred_element_type=F32).astype(BF16)

    for lo, hi, kinds in t_groups:
        @pl.when((j >= nn + lo) & (j < nn + hi))
        def _(kinds=kinds):
            acc = lax.dot_general(wt_ref[...], h_scr[...], nt_dims, preferred_element_type=F32)
            for u, kind in enumerate(kinds):
                a = acc[u * LANE:(u + 1) * LANE]
                if "rope" in kind:
                    a = (a * cos_ref[...] + pltpu.roll(a, HEAD_DIM - ROPE_PAIR, 0) * sin_lo_ref[...]
                         + pltpu.roll(a, ROPE_PAIR, 0) * sin_hi_ref[...])
                if "scale" in kind:
                    a = a * (QK_SCALE * LOG2E if "log2" in kind else QK_SCALE)
                pt_ref[0, u * LANE:(u + 1) * LANE, :] = a.astype(BF16)


def _inproj(x, shift, scale, gain, w_nat, w_t, w_g, kinds_t, rope_tabs, tm, tn):
    B, T, D = x.shape
    n_nat = w_nat.shape[1]
    n_t = w_t.shape[0]
    nn, ntt = n_nat // tn, n_t // tn
    per = tn // LANE
    n_i = T // tm
    bm = shift.shape[0]
    rope = rope_tabs is not None

    n_j = nn + ntt

    def next_tile(b, i, j):
        t = jnp.minimum(b * n_i + i + (j >= 1), B * n_i - 1)
        return t // n_i, t % n_i

    def mod_map(b, i, j):
        return (next_tile(b, i, j)[0] if bm == B else 0, 0, 0)

    def wtile(b, i, j):
        return _snake(b * n_i + i, j, n_j)

    in_specs = [pl.BlockSpec((1, tm, D), lambda b, i, j: next_tile(b, i, j) + (0,)),
                pl.BlockSpec((1, 1, D), mod_map),
                pl.BlockSpec((1, 1, D), mod_map),
                pl.BlockSpec((1, D), lambda b, i, j: (0, 0)),
                pl.BlockSpec((D, tn), lambda b, i, j: (0, jnp.minimum(wtile(b, i, j), nn - 1))),
                pl.BlockSpec((tn, D), lambda b, i, j: (jnp.maximum(wtile(b, i, j) - nn, 0), 0)),
                pl.BlockSpec(w_g.shape, lambda b, i, j: (0, 0))]
    args = [x, shift, scale, gain.reshape(1, D), w_nat, w_t, w_g]
    if rope:
        in_specs += [pl.BlockSpec((HEAD_DIM, tm), lambda b, i, j: (0, i))] * len(rope_tabs)
        args += list(rope_tabs)
    return pl.pallas_call(
        functools.partial(_inproj_kernel, t_groups=_tile_groups(kinds_t, per), nn=nn, rope=rope, n_i=n_i,
                          n_j=n_j),
        out_shape=(jax.ShapeDtypeStruct((B, T, n_nat), BF16),
                   jax.ShapeDtypeStruct((B, n_t, T), BF16),
                   jax.ShapeDtypeStruct((B, w_g.shape[0], T), F32)),
        grid=(B, n_i, n_j),
        in_specs=in_specs,
        out_specs=(pl.BlockSpec((1, tm, tn), lambda b, i, j: (b, i, jnp.minimum(wtile(b, i, j), nn - 1))),
                   pl.BlockSpec((1, tn, tm), lambda b, i, j: (b, jnp.maximum(wtile(b, i, j) - nn, 0), i)),
                   pl.BlockSpec((1, w_g.shape[0], tm), lambda b, i, j: (b, 0, i))),
        scratch_shapes=[pltpu.VMEM((tm, D), BF16)],
        compiler_params=_params(("arbitrary", "arbitrary", "arbitrary")),
        name="inproj_rope" if rope else "inproj_ctx",
    )(*args)


def _scan_lanes(x, op, reverse, fill):
    lane = lax.broadcasted_iota(jnp.int32, x.shape, 1)
    k = 1
    while k < CHUNK:
        if reverse:
            sh = jnp.where(lane < CHUNK - k, pltpu.roll(x, CHUNK - k, 1), fill)
        else:
            sh = jnp.where(lane >= k, pltpu.roll(x, k, 1), fill)
        x = op(x, sh)
        k *= 2
    return x


def _log_sigmoid(z):
    return jnp.minimum(z, 0.0) - jnp.log(1.0 + jnp.exp(-jnp.abs(z)))


def _lane_value(x, lane_idx):
    lane = lax.broadcasted_iota(jnp.int32, x.shape, 1)
    return jnp.broadcast_to(jnp.sum(jnp.where(lane == lane_idx, x, 0.0), axis=1, keepdims=True), x.shape)


def _gate_prep_kernel(gt_ref, bias_ref, row_ref, col_ref, *, nchunk):
    H = MLSTM_HEADS
    for c in range(nchunk):
        sl = slice(c * CHUNK, (c + 1) * CHUNK)
        z = gt_ref[0, :, sl] + bias_ref[...]
        li_f, lf_f = z[0:H] * LOG2E, _log_sigmoid(z[H:2 * H]) * LOG2E
        li_r, lf_r = z[2 * H:3 * H] * LOG2E, _log_sigmoid(z[3 * H:4 * H]) * LOG2E
        b_f = _scan_lanes(lf_f, jnp.add, False, 0.0)
        b_r = _scan_lanes(lf_r, jnp.add, True, 0.0)
        u_f = li_f - b_f
        u_r = li_r - b_r
        r_f = _scan_lanes(u_f, jnp.maximum, False, -jnp.inf)
        r_r = _scan_lanes(u_r, jnp.maximum, True, -jnp.inf)
        ends = [_lane_value(r_f, CHUNK - 1), _lane_value(r_r, 0), _lane_value(b_f, CHUNK - 1), _lane_value(b_r, 0)]
        for k, v in enumerate([r_f, r_r, b_f, b_r] + ends):
            row_ref[0, k * H:(k + 1) * H, sl] = v
        stack = jnp.concatenate([u_f, u_r, jnp.zeros((LANE - 2 * H, CHUNK), F32)], axis=0)
        col_ref[0, sl, :] = stack.T


def _gate_prep(g_t, bias):
    B, G, T = g_t.shape
    tg = min(T, 8 * CHUNK)
    H = MLSTM_HEADS
    return pl.pallas_call(
        functools.partial(_gate_prep_kernel, nchunk=tg // CHUNK),
        out_shape=(jax.ShapeDtypeStruct((B, 8 * H, T), F32),
                   jax.ShapeDtypeStruct((B, T, LANE), F32)),
        grid=(B, T // tg),
        in_specs=[pl.BlockSpec((1, G, tg), lambda b, i: (b, 0, i)),
                  pl.BlockSpec((G, CHUNK), lambda b, i: (0, 0))],
        out_specs=(pl.BlockSpec((1, 8 * H, tg), lambda b, i: (b, 0, i)),
                   pl.BlockSpec((1, tg, LANE), lambda b, i: (b, i, 0))),
        compiler_params=_params(("parallel", "parallel")),
        name="gate_prep",
    )(g_t, jnp.broadcast_to(bias.reshape(G, 1), (G, CHUNK)))


def _attn_kernel(sink_ref, qt_ref, kp_ref, km_ref, kn_ref, vp_ref, vm_ref, vn_ref, kx_ref, vx_ref, band_ref,
                 o_ref, *, nb, qb):
    cols = ATTN_GROUP * WINDOW
    row = lax.broadcasted_iota(jnp.int32, (3 * WINDOW, WINDOW), 0)
    lane = lax.broadcasted_iota(jnp.int32, (1, cols), 1)
    tn_dims = (((0,), (0,)), ((), ()))

    def key_blocks(prev_ref, main_ref, next_ref, hs, q):
        blk = lambda i: main_ref[0, hs, i * WINDOW:(i + 1) * WINDOW]
        return [prev_ref[0, hs, :] if q == 0 else blk(q - 1), blk(q),
                next_ref[0, hs, :] if q == qb - 1 else blk(q + 1)]

    for q in range(qb):
        j = pl.program_id(1) * qb + q
        qs = slice(q * WINDOW, (q + 1) * WINDOW)
        prev_bias = jnp.where(j > 0, 0.0, NEG)
        next_bias = jnp.where(j < nb - 1, 0.0, NEG)
        edge = jnp.where(row < WINDOW, prev_bias, jnp.where(row >= 2 * WINDOW, next_bias, 0.0))
        bias = band_ref[...] + jnp.concatenate([edge] * ATTN_GROUP, axis=1)
        for g in range(ATTN_KV_HEADS):
            hs = slice(g * HEAD_DIM, (g + 1) * HEAD_DIM)
            qt = jnp.concatenate([qt_ref[0, (g * ATTN_GROUP + h) * HEAD_DIM:(g * ATTN_GROUP + h + 1) * HEAD_DIM, qs]
                                  for h in range(ATTN_GROUP)], axis=1)
            kt = jnp.concatenate(key_blocks(kp_ref, km_ref, kn_ref, hs, q) + [kx_ref[0, hs, :]], axis=1)
            vt = jnp.concatenate(key_blocks(vp_ref, vm_ref, vn_ref, hs, q) + [vx_ref[0, hs, :]], axis=1)
            st = lax.dot_general(kt, qt, tn_dims, preferred_element_type=F32)
            s_loc = st[:3 * WINDOW] + bias
            s_ctx = st[3 * WINDOW:]
            sink = jnp.full((1, cols), sink_ref[g * ATTN_GROUP] * LOG2E, F32)
            for h in range(1, ATTN_GROUP):
                sink = jnp.where(lane >= h * WINDOW, sink_ref[g * ATTN_GROUP + h] * LOG2E, sink)
            m = jnp.maximum(jnp.maximum(jnp.max(s_loc, axis=0, keepdims=True),
                                        jnp.max(s_ctx, axis=0, keepdims=True)), sink)
            p_loc = jnp.exp2(s_loc - m)
            p_ctx = jnp.exp2(s_ctx - m)
            den = (jnp.sum(p_loc, axis=0, keepdims=True) + jnp.sum(p_ctx, axis=0, keepdims=True)
                   + jnp.exp2(sink - m))
            pt = jnp.concatenate([p_loc, p_ctx], axis=0).astype(BF16)
            ot = jnp.dot(vt, pt, preferred_element_type=F32) * (1.0 / den)
            for h in range(ATTN_GROUP):
                r0 = (g * ATTN_GROUP + h) * HEAD_DIM
                o_ref[0, q // SLAB_BLOCKS, r0:r0 + HEAD_DIM, (q % SLAB_BLOCKS) * WINDOW:(q % SLAB_BLOCKS + 1) * WINDOW] = (
                    ot[:, h * WINDOW:(h + 1) * WINDOW].astype(BF16))


def _attention(pt_lat, pt_ctx, sink, q_blk, k_blk, v_blk, kx_blk, vx_blk, qb):
    B, _, S = pt_lat.shape
    C = pt_ctx.shape[2]
    nb = S // WINDOW
    assert nb % qb == 0 and qb % SLAB_BLOCKS == 0
    cols = ATTN_GROUP * WINDOW
    t = np.arange(cols)[None, :] % WINDOW
    d = np.arange(3 * WINDOW)[:, None] - t
    band = jnp.asarray(np.where((d >= 0) & (d <= 2 * WINDOW), 0.0, NEG), F32)

    def kspecs(blk):
        edge = lambda off: pl.BlockSpec((1, ATTN_KV_W, WINDOW),
                                        lambda b, j: (b, blk, jnp.clip(j * qb + off, 0, nb - 1)))
        return [edge(-1), pl.BlockSpec((1, ATTN_KV_W, qb * WINDOW), lambda b, j: (b, blk, j)), edge(qb)]

    return pl.pallas_call(
        functools.partial(_attn_kernel, nb=nb, qb=qb),
        out_shape=jax.ShapeDtypeStruct((B, S // MERGE_ROWS, ATTN_Q_W, MERGE_ROWS), BF16),
        grid=(B, nb // qb),
        in_specs=[pl.BlockSpec(memory_space=pltpu.SMEM),
                  pl.BlockSpec((1, ATTN_Q_W, qb * WINDOW), lambda b, j: (b, q_blk, j))]
                 + kspecs(k_blk) + kspecs(v_blk)
                 + [pl.BlockSpec((1, ATTN_KV_W, C), lambda b, j: (b, kx_blk, 0)),
                    pl.BlockSpec((1, ATTN_KV_W, C), lambda b, j: (b, vx_blk, 0)),
                    pl.BlockSpec((3 * WINDOW, cols), lambda b, j: (0, 0))],
        out_specs=pl.BlockSpec((1, qb // SLAB_BLOCKS, ATTN_Q_W, MERGE_ROWS), lambda b, j: (b, j, 0, 0)),
        compiler_params=_params(("parallel", "arbitrary")),
        name="window_attn",
    )(sink, pt_lat, *([pt_lat] * 6), pt_ctx, pt_ctx, band)


def _v_aug_t(vt):
    sub = lax.broadcasted_iota(jnp.int32, (BF16_SUBLANES, vt.shape[1]), 0)
    return jnp.concatenate([vt, jnp.where(sub == 0, 1.0, 0.0).astype(BF16)], axis=0)


def _state_update(ct_old, m_old, k, vaug_t, u_bc, r_end, b_end):
    m_end = jnp.maximum(m_old, r_end)
    ks = (k.astype(F32) * jnp.exp2(u_bc - m_end)).astype(BF16)
    ct_new = jnp.exp2(m_old - m_end) * ct_old + jnp.dot(vaug_t, ks, preferred_element_type=F32)
    return ct_new, b_end + m_end


def _row(row_ref, k, h, ts):
    i = k * MLSTM_HEADS + h
    return row_ref[0, i:i + 1, ts]


def _scan_kernel(k_ref, vt_ref, row_ref, col_ref, c0_ref, m0_ref, cs_ref, ms_ref, cf_ref, mf_ref,
                 c_scr, m_scr, *, reverse, nsteps, cps):
    H = MLSTM_HEADS
    n = pl.program_id(1)
    d = 1 if reverse else 0

    @pl.when(n == 0)
    def _():
        c_scr[...] = c0_ref[0]
        m_scr[...] = m0_ref[0]

    for c in (range(cps - 1, -1, -1) if reverse else range(cps)):
        ts = slice(c * CHUNK, (c + 1) * CHUNK)
        cols = col_ref[0, ts, :]
        for h in range(H):
            ct_old = c_scr[h]
            m_old = m_scr[h:h + 1, :]
            cs_ref[0, c, h] = ct_old.astype(BF16)
            ms_ref[0, c, h:h + 1, :] = m_old
            k = k_ref[0, ts, h * MLSTM_QK_DIM:(h + 1) * MLSTM_QK_DIM]
            vaug_t = _v_aug_t(vt_ref[0, h * MLSTM_V_DIM:(h + 1) * MLSTM_V_DIM, ts])
            u_bc = jnp.broadcast_to(cols[:, d * H + h:d * H + h + 1], (CHUNK, LANE))
            ct_new, m_new = _state_update(ct_old, m_old, k, vaug_t, u_bc, _row(row_ref, 4 + d, h, ts),
                                          _row(row_ref, 6 + d, h, ts))
            c_scr[h] = ct_new
            m_scr[h:h + 1, :] = m_new

    @pl.when(n == nsteps - 1)
    def _():
        cf_ref[0] = c_scr[...]
        mf_ref[0] = m_scr[...]


def _state_scan(p, k_blk, p_t, vt_blk, rows, cols, c0, m0, reverse, cps):
    B, T, _ = p.shape
    nc = T // CHUNK
    cps = min(cps, nc)
    ns = nc // cps
    assert ns * cps == nc
    H = MLSTM_HEADS
    tb = cps * CHUNK
    cidx = (lambda n: ns - 1 - n) if reverse else (lambda n: n)
    st = (H, C_ROWS, MLSTM_QK_DIM)
    return pl.pallas_call(
        functools.partial(_scan_kernel, reverse=reverse, nsteps=ns, cps=cps),
        out_shape=(jax.ShapeDtypeStruct((B, nc) + st, BF16),
                   jax.ShapeDtypeStruct((B, nc, H, LANE), F32),
                   jax.ShapeDtypeStruct((B,) + st, F32),
                   jax.ShapeDtypeStruct((B, H, LANE), F32)),
        grid=(B, ns),
        in_specs=[pl.BlockSpec((1, tb, MLSTM_QK_W), lambda b, n: (b, cidx(n), k_blk)),
                  pl.BlockSpec((1, MLSTM_V_W, tb), lambda b, n: (b, vt_blk, cidx(n))),
                  pl.BlockSpec((1, 8 * H, tb), lambda b, n: (b, 0, cidx(n))),
                  pl.BlockSpec((1, tb, LANE), lambda b, n: (b, cidx(n), 0)),
                  pl.BlockSpec((1,) + st, lambda b, n: (b, 0, 0, 0)),
                  pl.BlockSpec((1, H, LANE), lambda b, n: (b, 0, 0))],
        out_specs=(pl.BlockSpec((1, cps) + st, lambda b, n: (b, cidx(n), 0, 0, 0)),
                   pl.BlockSpec((1, cps, H, LANE), lambda b, n: (b, cidx(n), 0, 0)),
                   pl.BlockSpec((1,) + st, lambda b, n: (b, 0, 0, 0)),
                   pl.BlockSpec((1, H, LANE), lambda b, n: (b, 0, 0))),
        scratch_shapes=[pltpu.VMEM(st, F32), pltpu.VMEM((H, LANE), F32)],
        compiler_params=_params(("parallel", "arbitrary")),
        name="mlstm_scan_rev" if reverse else "mlstm_scan_fwd",
    )(p, p_t, rows, cols, c0, m0)


def _mlstm_out_kernel(k_ref, qt_ref, vt_ref, mot_ref, row_ref, col_ref, cr_ref, mr_ref, c0_ref, m0_ref, g_ref,
                      o_ref, c_scr, m_scr, *, cps):
    H = MLSTM_HEADS
    L = CHUNK
    n = pl.program_id(1)

    @pl.when(n == 0)
    def _():
        c_scr[...] = c0_ref[0]
        m_scr[...] = m0_ref[0]

    si = lax.broadcasted_iota(jnp.int32, (L, L), 0)
    ti = lax.broadcasted_iota(jnp.int32, (L, L), 1)
    for c in range(cps):
        ts = slice(c * L, (c + 1) * L)
        cols = col_ref[0, ts, :]
        for h in range(H):
            qt = qt_ref[0, h * MLSTM_QK_DIM:(h + 1) * MLSTM_QK_DIM, ts]
            k = k_ref[0, ts, h * MLSTM_QK_DIM:(h + 1) * MLSTM_QK_DIM]
            vs = slice(h * MLSTM_V_DIM, (h + 1) * MLSTM_V_DIM)
            vaug_t = _v_aug_t(vt_ref[0, vs, ts])
            pt = jnp.dot(k, qt, preferred_element_type=F32)
            qt_f = qt.astype(F32)
            ct_f = c_scr[h]
            hs_t = None
            for d in range(N_DIRS):
                u_bc = jnp.broadcast_to(cols[:, d * H + h:d * H + h + 1], (L, LANE))
                m0 = m_scr[h:h + 1, :] if d == 0 else mr_ref[0, c, h:h + 1, :]
                ct = ct_f.astype(BF16) if d == 0 else cr_ref[0, c, h]
                mx = jnp.maximum(_row(row_ref, d, h, ts), m0)
                valid = (si <= ti) if d == 0 else (si >= ti)
                s_t = (pt * jnp.exp2(jnp.where(valid, u_bc - mx, -jnp.inf))).astype(BF16)
                q_in = (qt_f * jnp.exp2(m0 - mx)).astype(BF16)
                tot = jnp.dot(jnp.concatenate([vaug_t, ct], axis=1), jnp.concatenate([s_t, q_in], axis=0),
                              preferred_element_type=F32)
                den = jnp.maximum(jnp.abs(tot[MLSTM_V_DIM:MLSTM_V_DIM + 1]),
                                  jnp.exp2(-_row(row_ref, 2 + d, h, ts) - mx))
                part = tot[:MLSTM_V_DIM] * (1.0 / den)
                hs_t = part if hs_t is None else hs_t + part
                if d == 0:
                    u_bc_f = u_bc
            hn = hs_t * lax.rsqrt(jnp.mean(hs_t * hs_t, axis=0, keepdims=True) + NORM_EPS) * g_ref[vs, :]
            o_ref[0, c // SLAB_BLOCKS, vs, (c % SLAB_BLOCKS) * L:(c % SLAB_BLOCKS + 1) * L] = (
                _sigmoid(mot_ref[0, vs, ts].astype(F32)) * hn).astype(BF16)
            ct_new, m_new = _state_update(ct_f, m_scr[h:h + 1, :], k, vaug_t, u_bc_f, _row(row_ref, 4, h, ts),
                                          _row(row_ref, 6, h, ts))
            c_scr[h] = ct_new
            m_scr[h:h + 1, :] = m_new


def _mlstm_out(p, k_blk, p_t, qt_blk, vt_blk, mot_blk, rows, cols, c_rev, m_rev, c0, m0, gain, cps):
    B, S, _ = p.shape
    nc = S // CHUNK
    ns = nc // cps
    assert ns * cps == nc and cps % SLAB_BLOCKS == 0
    H = MLSTM_HEADS
    tb = cps * CHUNK
    st = (H, C_ROWS, MLSTM_QK_DIM)
    gain_bc = jnp.broadcast_to(gain.reshape(MLSTM_V_W, 1), (MLSTM_V_W, LANE))
    return pl.pallas_call(
        functools.partial(_mlstm_out_kernel, cps=cps),
        out_shape=jax.ShapeDtypeStruct((B, S // MERGE_ROWS, MLSTM_V_W, MERGE_ROWS), BF16),
        grid=(B, ns),
        in_specs=[pl.BlockSpec((1, tb, MLSTM_QK_W), lambda b, n: (b, n, k_blk)),
                  pl.BlockSpec((1, MLSTM_QK_W, tb), lambda b, n: (b, qt_blk, n)),
                  pl.BlockSpec((1, MLSTM_V_W, tb), lambda b, n: (b, vt_blk, n)),
                  pl.BlockSpec((1, MLSTM_V_W, tb), lambda b, n: (b, mot_blk, n)),
                  pl.BlockSpec((1, 8 * H, tb), lambda b, n: (b, 0, n)),
                  pl.BlockSpec((1, tb, LANE), lambda b, n: (b, n, 0)),
                  pl.BlockSpec((1, cps) + st, lambda b, n: (b, n, 0, 0, 0)),
                  pl.BlockSpec((1, cps, H, LANE), lambda b, n: (b, n, 0, 0)),
                  pl.BlockSpec((1,) + st, lambda b, n: (b, 0, 0, 0)),
                  pl.BlockSpec((1, H, LANE), lambda b, n: (b, 0, 0)),
                  pl.BlockSpec((MLSTM_V_W, LANE), lambda b, n: (0, 0))],
        out_specs=pl.BlockSpec((1, cps // SLAB_BLOCKS, MLSTM_V_W, MERGE_ROWS), lambda b, n: (b, n, 0, 0)),
        scratch_shapes=[pltpu.VMEM(st, F32), pltpu.VMEM((H, LANE), F32)],
        compiler_params=_params(("parallel", "arbitrary")),
        name="mlstm_out",
    )(p, p_t, p_t, p_t, rows, cols, c_rev, m_rev, c0, m0, gain_bc)


def _merge_kernel(attt_ref, memt_ref, ga_ref, gm_ref, x_ref, gate_ref, shift_ref, scale_ref, g2_ref,
                  wa_ref, wm_ref, wo_ref, o_ref, h_ref):
    tn_dims = (((0,), (0,)), ((), ()))
    a = lax.dot_general(attt_ref[0, 0], wa_ref[...], tn_dims, preferred_element_type=F32)
    m = lax.dot_general(memt_ref[0, 0], wm_ref[...], tn_dims, preferred_element_type=F32)
    y = _sigmoid(ga_ref[0].astype(F32)) * a + _sigmoid(gm_ref[0].astype(F32)) * m
    z = jnp.dot(y.astype(BF16), wo_ref[...], preferred_element_type=F32)
    x1 = x_ref[0] + gate_ref[0] * z
    o_ref[0] = x1
    ms = jnp.mean(x1 * x1, axis=-1, keepdims=True)
    h_ref[0] = (x1 * lax.rsqrt(ms + NORM_EPS) * g2_ref[...] * (1.0 + scale_ref[0]) + shift_ref[0]).astype(BF16)


def _merge(att_t, mem_t, p, ga_blk, gm_blk, x, gate1, shift2, scale2, norm2_g, w_ap, w_mp, w_out, tm):
    B, S, D = x.shape
    resident = functools.partial(pl.BlockSpec, pipeline_mode=pl.Buffered(1))
    mod = pl.BlockSpec((1, 1, D), lambda b, i: (b, 0, 0))
    return pl.pallas_call(
        _merge_kernel,
        out_shape=(jax.ShapeDtypeStruct((B, S, D), F32), jax.ShapeDtypeStruct((B, S, D), BF16)),
        grid=(B, S // tm),
        in_specs=[pl.BlockSpec((1, 1, ATTN_Q_W, tm), lambda b, i: (b, i, 0, 0)),
                  pl.BlockSpec((1, 1, MLSTM_V_W, tm), lambda b, i: (b, i, 0, 0)),
                  pl.BlockSpec((1, tm, D), lambda b, i: (b, i, ga_blk)),
                  pl.BlockSpec((1, tm, D), lambda b, i: (b, i, gm_blk)),
                  pl.BlockSpec((1, tm, D), lambda b, i: (b, i, 0)),
                  mod, mod, mod,
                  pl.BlockSpec((1, D), lambda b, i: (0, 0)),
                  resident(w_ap.shape, lambda b, i: (0, 0)),
                  resident(w_mp.shape, lambda b, i: (0, 0)),
                  resident(w_out.shape, lambda b, i: (0, 0))],
        out_specs=(pl.BlockSpec((1, tm, D), lambda b, i: (b, i, 0)),
                   pl.BlockSpec((1, tm, D), lambda b, i: (b, i, 0))),
        compiler_params=_params(("parallel", "parallel")),
        name="merge_outproj",
    )(att_t, mem_t, p, p, x, gate1, shift2, scale2, norm2_g.reshape(1, D), w_ap, w_mp, w_out)


def _ffn_kernel(x1_hbm, h_ref, gate_ref, wg_ref, wu_ref, wo_ref, gf_ref, o_ref, x1_buf, x1_sem, *, nf, tm):
    b, i, f = pl.program_id(0), pl.program_id(1), pl.program_id(2)
    rc = min(tm, NORM_ROWS)
    x1_copy = pltpu.make_async_copy(x1_hbm.at[b, pl.ds(pl.multiple_of(i * tm, tm), tm), :], x1_buf, x1_sem)

    @pl.when(f == max(nf - 1 - X1_LOOKAHEAD, 0))
    def _():
        x1_copy.start()

    def d_ff_tile(first):
        h = h_ref[0]
        gt = jnp.dot(h, wg_ref[...], preferred_element_type=F32)
        up = jnp.dot(h, wu_ref[...], preferred_element_type=F32)
        act = (gt * _sigmoid(gt) * up).astype(BF16)
        nc = wo_ref.shape[0]
        for c0 in range(0, o_ref.shape[2], nc):
            part = jnp.dot(act, wo_ref[:, c0:c0 + nc], preferred_element_type=F32)
            if first:
                o_ref[0, :, c0:c0 + nc] = part
            else:
                o_ref[0, :, c0:c0 + nc] += part

    pl.when(f == 0)(functools.partial(d_ff_tile, True))
    pl.when(f > 0)(functools.partial(d_ff_tile, False))

    @pl.when(f == nf - 1)
    def _():
        x1_copy.wait()
        for r0 in range(0, tm, rc):
            x2 = x1_buf[r0:r0 + rc, :] + gate_ref[0] * o_ref[0, r0:r0 + rc, :]
            ms = jnp.mean(x2 * x2, axis=-1, keepdims=True)
            o_ref[0, r0:r0 + rc, :] = x2 * lax.rsqrt(ms + NORM_EPS) * gf_ref[...]


def _ffn(x1, h2, gate2, w_in, w_out, final_g, tm, tf):
    B, S, D = x1.shape
    dff = w_out.shape[0]
    nf = dff // tf
    n_i = S // tm

    def ft(b, i, f):
        return _snake(b * n_i + i, f, nf)

    return pl.pallas_call(
        functools.partial(_ffn_kernel, nf=nf, tm=tm),
        out_shape=jax.ShapeDtypeStruct((B, S, D), F32),
        grid=(B, S // tm, nf),
        in_specs=[pl.BlockSpec(memory_space=pl.ANY),
                  pl.BlockSpec((1, tm, D), lambda b, i, f: (b, i, 0)),
                  pl.BlockSpec((1, 1, D), lambda b, i, f: (b, 0, 0)),
                  pl.BlockSpec((D, tf), lambda b, i, f: (0, ft(b, i, f))),
                  pl.BlockSpec((D, tf), lambda b, i, f: (0, nf + ft(b, i, f))),
                  pl.BlockSpec((tf, D), lambda b, i, f: (ft(b, i, f), 0)),
                  pl.BlockSpec((1, D), lambda b, i, f: (0, 0))],
        out_specs=pl.BlockSpec((1, tm, D), lambda b, i, f: (b, i, 0)),
        scratch_shapes=[pltpu.VMEM((tm, D), F32), pltpu.SemaphoreType.DMA(())],
        compiler_params=_params(("arbitrary", "arbitrary", "arbitrary")),
        name="ffn_final_norm",
    )(x1, h2, gate2, w_in, w_in, w_out, final_g.reshape(1, D))


def _rope_tables(S):
    pos = jnp.arange(S)
    rows = (pos // GRID_W).astype(F32)
    cols = (pos % GRID_W).astype(F32)
    inv_freq = ROPE_BASE ** (-jnp.arange(ROPE_PAIR, dtype=F32) / ROPE_PAIR)
    ar = rows[:, None] * inv_freq[None, :]
    ac = cols[:, None] * inv_freq[None, :]
    zero = jnp.zeros_like(ar)
    cos = jnp.concatenate([jnp.cos(ar), jnp.cos(ar), jnp.cos(ac), jnp.cos(ac)], axis=1)
    sin_lo = jnp.concatenate([-jnp.sin(ar), zero, -jnp.sin(ac), zero], axis=1)
    sin_hi = jnp.concatenate([zero, jnp.sin(ar), zero, jnp.sin(ac)], axis=1)
    return cos.T, sin_lo.T, sin_hi.T


def kernel(x, c, ctx, c_ctx, w_ada, b_ada, norm1_g, w_in, b_gates, attn_sink, mlstm_norm_g, w_attn_proj,
           w_mlstm_proj, w_out, norm2_g, w_ffn_in, w_ffn_out, final_norm_g):
    B, S, D = x.shape
    C = ctx.shape[1]
    assert w_ada.shape[0] == 1, "single-layer configuration"
    assert S % 512 == 0 and C % CHUNK == 0 and S % GRID_W == 0
    H = MLSTM_HEADS
    tn = 1024

    rows = -(-(B + 1) // 8) * 8
    cvecs = jnp.concatenate([c, c_ctx[None], jnp.zeros((rows - B - 1, D), F32)], axis=0)
    mod = _adaln(cvecs, w_ada[0], b_ada[0])
    shift1, scale1, gate1, shift2, scale2, gate2 = [mod[:B, k * D:(k + 1) * D].reshape(B, 1, D) for k in range(N_MOD)]
    shift_c = mod[B:B + 1, 0:D].reshape(1, 1, D)
    scale_c = mod[B:B + 1, D:2 * D].reshape(1, 1, D)

    wi = w_in[0]
    o = 0
    parts = {}
    for name, width in (("a_k", ATTN_KV_W), ("a_v", ATTN_KV_W), ("m_k", MLSTM_QK_W), ("m_v", MLSTM_V_W),
                        ("m_g", N_GATE), ("a_q", ATTN_Q_W), ("m_q", MLSTM_QK_W), ("m_o", MLSTM_V_W),
                        ("g_att", D), ("g_mem", D)):
        parts[name] = wi[:, o:o + width]
        o += width
    w_nat = jnp.concatenate([parts["g_att"], parts["g_mem"], parts["m_k"]], axis=1).astype(BF16)
    w_t = jnp.concatenate([parts["m_v"], parts["m_o"], parts["a_q"], parts["m_q"], parts["a_k"], parts["a_v"]],
                          axis=1).T.astype(BF16)
    w_g = parts["m_g"].T.astype(BF16)
    kinds_t = ([""] * (2 * MLSTM_V_W // LANE) + ["rope scale log2"] * (ATTN_Q_W // LANE)
               + ["scale"] * (MLSTM_QK_W // LANE) + ["rope"] * (ATTN_KV_W // LANE) + [""] * (ATTN_KV_W // LANE))
    assert (2 * D) % MLSTM_QK_W == 0 and (2 * MLSTM_V_W) % ATTN_Q_W == 0
    ga_blk, gm_blk, mk_blk = 0, 1, 2 * D // MLSTM_QK_W
    vt_blk, mot_blk = 0, 1
    aq_blk = 2 * MLSTM_V_W // ATTN_Q_W
    qt_blk = (2 * MLSTM_V_W + ATTN_Q_W) // MLSTM_QK_W
    kt_blk = (2 * MLSTM_V_W + ATTN_Q_W + MLSTM_QK_W) // ATTN_KV_W
    vtt_blk = kt_blk + 1
    w_nat_c = parts["m_k"].astype(BF16)
    w_t_c = jnp.concatenate([parts["m_v"], parts["a_k"], parts["a_v"]], axis=1).T.astype(BF16)
    kx_blk, vx_blk = MLSTM_V_W // ATTN_KV_W, MLSTM_V_W // ATTN_KV_W + 1

    p_lat, pt_lat, gt_lat = _inproj(x, shift1, scale1, norm1_g[0], w_nat, w_t, w_g, kinds_t,
                                    _rope_tables(S), tm=1024 if S % 1024 == 0 else 512, tn=tn)
    p_ctx, pt_ctx, gt_ctx = _inproj(ctx, shift_c, scale_c, norm1_g[0], w_nat_c, w_t_c, w_g,
                                    [""] * (w_t_c.shape[0] // LANE), None, tm=min(C, 256), tn=tn)

    att_t = _attention(pt_lat, pt_ctx, attn_sink[0], aq_blk, kt_blk, vtt_blk, kx_blk, vx_blk,
                       qb=8 if (S // WINDOW) % 8 == 0 else SLAB_BLOCKS)

    rows_lat, cols_lat = _gate_prep(gt_lat, b_gates[0])
    rows_ctx, cols_ctx = _gate_prep(gt_ctx, b_gates[0])
    c_zero = jnp.zeros((B, H, C_ROWS, MLSTM_QK_DIM), F32)
    m_zero = jnp.zeros((B, H, LANE), F32)
    _, _, cf_ctx, mf_ctx = _state_scan(p_ctx, 0, pt_ctx, 0, rows_ctx, cols_ctx, c_zero, m_zero, reverse=False,
                                       cps=4)
    _, _, cr_ctx, mr_ctx = _state_scan(p_ctx, 0, pt_ctx, 0, rows_ctx, cols_ctx, c_zero, m_zero, reverse=True,
                                       cps=4)
    c_rev, m_rev, _, _ = _state_scan(p_lat, mk_blk, pt_lat, vt_blk, rows_lat, cols_lat, cr_ctx, mr_ctx,
                                     reverse=True, cps=16)
    mem_t = _mlstm_out(p_lat, mk_blk, pt_lat, qt_blk, vt_blk, mot_blk, rows_lat, cols_lat, c_rev, m_rev,
                       cf_ctx, mf_ctx, mlstm_norm_g[0], cps=8 if (S // CHUNK) % 8 == 0 else SLAB_BLOCKS)

    x1, h2 = _merge(att_t, mem_t, p_lat, ga_blk, gm_blk, x, gate1, shift2, scale2, norm2_g[0],
                    w_attn_proj[0].astype(BF16), w_mlstm_proj[0].astype(BF16), w_out[0].astype(BF16),
                    tm=MERGE_ROWS)
    return _ffn(x1, h2, gate2, w_ffn_in[0].astype(BF16), w_ffn_out[0].astype(BF16), final_norm_g,
                tm=1024 if S % 1024 == 0 else 512, tf=512)
```
